```python
import math
import jax, jax.numpy as jnp
from jax import lax
import numpy as np

D_MODEL = 1024
BATCH = 4
SEQ = 8192
DEPTH = 1

MIX_WIDTH = D_MODEL
D_RG = MIX_WIDTH // 2
D_S5 = MIX_WIDTH - D_RG
RG_HEADS = 8
RG_HD = D_RG // RG_HEADS
CONV_W = 4
RG_C = 8.0
S5_CH = 16
S5_GROUPS = D_S5 // S5_CH
S5_STATE = 64
MEM_LEN = 256
XA_HEADS = 4
XA_HD = D_MODEL // XA_HEADS
N_EXPERTS = 32
TOP_K = 4
D_FF = D_MODEL
SWIGLU_LIMIT = 7.0
SWIGLU_ALPHA = 1.702
MOE_BLOCK = 128
NORM_EPS = 1e-6

kernel_name = "hymba_rglru_s5_memxattn_moe"


def rmsnorm(x, g):
    xf = x.astype(jnp.float32)
    y = xf * lax.rsqrt(jnp.mean(xf * xf, axis=-1, keepdims=True) + NORM_EPS)
    return (y * g.astype(jnp.float32)).astype(x.dtype)


def causal_depthwise_conv(x, w, b):
    s = x.shape[1]
    xp = jnp.pad(x, ((0, 0), (CONV_W - 1, 0), (0, 0)))
    y = xp[:, 0:s] * w[0]
    for k in range(1, CONV_W):
        y = y + xp[:, k:k + s] * w[k]
    return y + b


def _linear_combine(c1, c2):
    a1, b1 = c1
    a2, b2 = c2
    return a1 * a2, a2 * b1 + b2


def rg_lru_mixer(xa, gate, conv_w, conv_b, rg_wa, rg_ba, rg_wx, rg_bx, rg_lambda):
    bsz, s, _ = xa.shape
    xc = causal_depthwise_conv(xa, conv_w, conv_b)
    xh = xc.reshape(bsz, s, RG_HEADS, RG_HD)
    r = jax.nn.sigmoid(jnp.einsum('bshi,hij->bshj', xh, rg_wa).reshape(bsz, s, D_RG) + rg_ba)
    i = jax.nn.sigmoid(jnp.einsum('bshi,hij->bshj', xh, rg_wx).reshape(bsz, s, D_RG) + rg_bx)
    log_a = -RG_C * r * jax.nn.softplus(-rg_lambda)
    a = jnp.exp(log_a)
    mult = jnp.sqrt(-jnp.expm1(2.0 * log_a))
    is_start = (jnp.arange(s) == 0)[None, :, None]
    mult = jnp.where(is_start, jnp.ones_like(mult), mult)
    b = mult * (i * xc)
    _, h = lax.associative_scan(_linear_combine, (a, b), axis=1)
    return h * jax.nn.gelu(gate)


def _complex_linear_combine(c1, c2):
    ar1, ai1, br1, bi1 = c1
    ar2, ai2, br2, bi2 = c2
    ar = ar2 * ar1 - ai2 * ai1
    ai = ar2 * ai1 + ai2 * ar1
    br = ar2 * br1 - ai2 * bi1 + br2
    bi = ar2 * bi1 + ai2 * br1 + bi2
    return ar, ai, br, bi


def s5_mixer(u, lam_re, lam_im, b_re, b_im, c_re, c_im, d_skip, log_dt, glu_w, glu_b):
    bsz, s, _ = u.shape
    ug = u.reshape(bsz, s, S5_GROUPS, S5_CH)
    dt = jnp.exp(log_dt)[:, None]
    mag = jnp.exp(dt * lam_re)
    abar_re = mag * jnp.cos(dt * lam_im)
    abar_im = mag * jnp.sin(dt * lam_im)
    den = lam_re * lam_re + lam_im * lam_im
    num_re = abar_re - 1.0
    coef_re = (num_re * lam_re + abar_im * lam_im) / den
    coef_im = (abar_im * lam_re - num_re * lam_im) / den
    bbar_re = coef_re[..., None] * b_re - coef_im[..., None] * b_im
    bbar_im = coef_re[..., None] * b_im + coef_im[..., None] * b_re
    bu_re = jnp.einsum('bsgc,gpc->bsgp', ug, bbar_re)
    bu_im = jnp.einsum('bsgc,gpc->bsgp', ug, bbar_im)
    ar = jnp.broadcast_to(abar_re[None, None], (1, s, S5_GROUPS, S5_STATE)).astype(bu_re.dtype)
    ai = jnp.broadcast_to(abar_im[None, None], (1, s, S5_GROUPS, S5_STATE)).astype(bu_re.dtype)
    _, _, h_re, h_im = lax.associative_scan(_complex_linear_combine, (ar, ai, bu_re, bu_im), axis=1)
    y = (jnp.einsum('bsgp,gcp->bsgc', h_re, c_re) - jnp.einsum('bsgp,gcp->bsgc', h_im, c_im)
         + d_skip.reshape(S5_GROUPS, S5_CH) * ug)
    z = jax.nn.gelu(y)
    gate = jax.nn.sigmoid(jnp.einsum('bsgc,gcd->bsgd', z, glu_w) + glu_b.reshape(S5_GROUPS, S5_CH))
    return (z * gate).reshape(bsz, s, D_S5)


def memory_cross_attention(xn, mem, mem_g, xa_wq, xa_wkv, xa_wo):
    bsz, s, _ = xn.shape
    m = mem.shape[1]
    q = (xn @ xa_wq).reshape(bsz, s, XA_HEADS, XA_HD)
    kv = (rmsnorm(mem, mem_g) @ xa_wkv).reshape(bsz, m, 2, XA_HEADS, XA_HD)
    k, v = kv[:, :, 0], kv[:, :, 1]
    scores = jnp.einsum('bshd,bmhd->bhsm', q, k).astype(jnp.float32) * (XA_HD ** -0.5)
    p = jax.nn.softmax(scores, axis=-1).astype(v.dtype)
    o = jnp.einsum('bhsm,bmhd->bshd', p, v).reshape(bsz, s, D_MODEL)
    return o @ xa_wo


def clamped_swiglu_expert(xb, w_gu, b_gu, w_down, b_down):
    gu = xb @ w_gu + b_gu
    g, u = gu[:, :D_FF], gu[:, D_FF:]
    g = jnp.minimum(g, SWIGLU_LIMIT)
    u = jnp.clip(u, -SWIGLU_LIMIT, SWIGLU_LIMIT)
    h = (u + 1.0) * (g * jax.nn.sigmoid(SWIGLU_ALPHA * g))
    return h @ w_down + b_down


def moe_ffn(xn, router_w, router_b, exp_w_gu, exp_b_gu, exp_w_down, exp_b_down):
    bsz, s, d = xn.shape
    n = bsz * s
    xt = xn.reshape(n, d)
    logits = (xt @ router_w + router_b).astype(jnp.float32)
    top_val, top_idx = lax.top_k(logits, TOP_K)
    gates = jax.nn.softmax(top_val, axis=-1).astype(xt.dtype)
    n_assign = n * TOP_K
    e_flat = top_idx.reshape(n_assign)
    t_flat = jnp.repeat(jnp.arange(n, dtype=jnp.int32), TOP_K)
    w_flat = gates.reshape(n_assign)
    order = jnp.argsort(e_flat)
    e_s, t_s, w_s = e_flat[order], t_flat[order], w_flat[order]
    counts = jnp.bincount(e_flat, length=N_EXPERTS)
    padded = ((counts + MOE_BLOCK - 1) // MOE_BLOCK) * MOE_BLOCK
    start = jnp.cumsum(counts) - counts
    pend = jnp.cumsum(padded)
    pstart = pend - padded
    dest = pstart[e_s] + (jnp.arange(n_assign) - start[e_s])
    n_slots = n_assign + N_EXPERTS * MOE_BLOCK
    n_blocks = n_slots // MOE_BLOCK
    slot_tok = jnp.zeros((n_slots,), jnp.int32).at[dest].set(t_s)
    slot_w = jnp.zeros((n_slots,), xt.dtype).at[dest].set(w_s)
    block_exp = jnp.minimum(
        jnp.searchsorted(pend, jnp.arange(n_blocks) * MOE_BLOCK, side='right'), N_EXPERTS - 1)

    def run_block(args):
        tok, e = args
        return clamped_swiglu_expert(xt[tok], exp_w_gu[e], exp_b_gu[e], exp_w_down[e], exp_b_down[e])

    yb = lax.map(run_block, (slot_tok.reshape(n_blocks, MOE_BLOCK), block_exp))
    y = jnp.zeros((n, d), xt.dtype).at[slot_tok].add(yb.reshape(n_slots, d) * slot_w[:, None])
    return y.reshape(bsz, s, d)


def setup_inputs(seed: int = 0) -> dict:
    key = jax.random.key(seed)
    ks = jax.random.split(key, 40)
    f32 = jnp.float32
    nrm = lambda k, shape, scale: jax.random.normal(k, shape, f32) * scale
    gain = lambda k, n: 1.0 + 0.01 * jax.random.normal(k, (n,), f32)
    a0 = jax.random.uniform(ks[10], (D_RG,), f32, 0.9, 0.999)
    sa = a0 ** (1.0 / RG_C)
    rg_lambda = jnp.log(sa) - jnp.log1p(-sa)
    n_idx = jnp.arange(S5_STATE, dtype=f32)
    lam_re = -0.5 + 0.01 * jax.random.normal(ks[11], (S5_GROUPS, S5_STATE), f32)
    lam_im = jnp.pi * n_idx[None, :] + 0.01 * jax.random.normal(ks[12], (S5_GROUPS, S5_STATE), f32)
    log_dt = jax.random.uniform(ks[13], (S5_GROUPS,), f32, math.log(0.001), math.log(0.1))
    return {
        "x": jax.random.normal(ks[0], (BATCH, SEQ, D_MODEL), f32),
        "mem": jax.random.normal(ks[1], (BATCH, MEM_LEN, D_MODEL), f32),
        "norm_mix_g": gain(ks[2], D_MODEL),
        "w_in": nrm(ks[3], (D_MODEL, 2 * D_RG + D_S5), D_MODEL ** -0.5),
        "conv_w": nrm(ks[4], (CONV_W, D_RG), CONV_W ** -0.5),
        "conv_b": nrm(ks[5], (D_RG,), 0.01),
        "rg_wa": nrm(ks[6], (RG_HEADS, RG_HD, RG_HD), RG_HD ** -0.5),
        "rg_ba": nrm(ks[7], (D_RG,), 0.01),
        "rg_wx": nrm(ks[8], (RG_HEADS, RG_HD, RG_HD), RG_HD ** -0.5),
        "rg_bx": nrm(ks[9], (D_RG,), 0.01),
        "rg_lambda": rg_lambda,
        "s5_lambda_re": lam_re,
        "s5_lambda_im": lam_im,
        "s5_b_re": nrm(ks[14], (S5_GROUPS, S5_STATE, S5_CH), (2.0 * S5_CH) ** -0.5),
        "s5_b_im": nrm(ks[15], (S5_GROUPS, S5_STATE, S5_CH), (2.0 * S5_CH) ** -0.5),
        "s5_c_re": nrm(ks[16], (S5_GROUPS, S5_CH, S5_STATE), (S5_STATE) ** -0.5),
        "s5_c_im": nrm(ks[17], (S5_GROUPS, S5_CH, S5_STATE), (S5_STATE) ** -0.5),
        "s5_d": nrm(ks[18], (D_S5,), 1.0),
        "s5_log_dt": log_dt,
        "s5_glu_w": nrm(ks[19], (S5_GROUPS, S5_CH, S5_CH), S5_CH ** -0.5),
        "s5_glu_b": nrm(ks[20], (D_S5,), 0.01),
        "w_out": nrm(ks[21], (MIX_WIDTH, D_MODEL), MIX_WIDTH ** -0.5),
        "norm_xa_g": gain(ks[22], D_MODEL),
        "mem_norm_g": gain(ks[23], D_MODEL),
        "xa_wq": nrm(ks[24], (D_MODEL, D_MODEL), D_MODEL ** -0.5),
        "xa_wkv": nrm(ks[25], (D_MODEL, 2 * D_MODEL), D_MODEL ** -0.5),
        "xa_wo": nrm(ks[26], (D_MODEL, D_MODEL), D_MODEL ** -0.5),
        "norm_ffn_g": gain(ks[27], D_MODEL),
        "router_w": nrm(ks[28], (D_MODEL, N_EXPERTS), D_MODEL ** -0.5),
        "router_b": nrm(ks[29], (N_EXPERTS,), 0.01),
        "exp_w_gu": nrm(ks[30], (N_EXPERTS, D_MODEL, 2 * D_FF), D_MODEL ** -0.5),
        "exp_b_gu": nrm(ks[31], (N_EXPERTS, 2 * D_FF), 0.01),
        "exp_w_down": nrm(ks[32], (N_EXPERTS, D_FF, D_MODEL), D_FF ** -0.5),
        "exp_b_down": nrm(ks[33], (N_EXPERTS, D_MODEL), 0.01),
        "final_norm_g": gain(ks[34], D_MODEL),
    }


def reference(x, mem, norm_mix_g, w_in, conv_w, conv_b, rg_wa, rg_ba, rg_wx, rg_bx, rg_lambda,
              s5_lambda_re, s5_lambda_im, s5_b_re, s5_b_im, s5_c_re, s5_c_im, s5_d, s5_log_dt,
              s5_glu_w, s5_glu_b, w_out, norm_xa_g, mem_norm_g, xa_wq, xa_wkv, xa_wo,
              norm_ffn_g, router_w, router_b, exp_w_gu, exp_b_gu, exp_w_down, exp_b_down,
              final_norm_g):
    h = x
    for _ in range(DEPTH):
        xn = rmsnorm(h, norm_mix_g)
        proj = xn @ w_in
        gate_rg = proj[..., :D_RG]
        x_rg = proj[..., D_RG:2 * D_RG]
        u_s5 = proj[..., 2 * D_RG:]
        y_rg = rg_lru_mixer(x_rg, gate_rg, conv_w, conv_b, rg_wa, rg_ba, rg_wx, rg_bx, rg_lambda)
        y_s5 = s5_mixer(u_s5, s5_lambda_re, s5_lambda_im, s5_b_re, s5_b_im, s5_c_re, s5_c_im,
                        s5_d, s5_log_dt, s5_glu_w, s5_glu_b)
        h = h + jnp.concatenate([y_rg, y_s5], axis=-1) @ w_out
        h = h + memory_cross_attention(rmsnorm(h, norm_xa_g), mem, mem_norm_g, xa_wq, xa_wkv, xa_wo)
        h = h + moe_ffn(rmsnorm(h, norm_ffn_g), router_w, router_b, exp_w_gu, exp_b_gu,
                        exp_w_down, exp_b_down)
    return rmsnorm(h, final_norm_g)
```

```python
import functools
import math

import jax
import jax.numpy as jnp
from jax import lax
from jax.experimental import pallas as pl
from jax.experimental.pallas import tpu as pltpu

F32 = jnp.float32
BF16 = jnp.bfloat16

D_MODEL = 1024
D_RG = 512
D_S5 = 512
RG_HEADS = 8
CONV_W = 4
RG_C = 8.0
S5_CH = 16
S5_GROUPS = 32
S5_STATE = 64
N_STATE = S5_GROUPS * S5_STATE
XA_HEADS = 4
XA_HD = 256
N_EXPERTS = 32
TOP_K = 4
D_FF = 1024
SWIGLU_LIMIT = 7.0
SWIGLU_ALPHA = 1.702
NORM_EPS = 1e-6

SUBLANES = 8
T_MIX = 256
T_ATT = 512
T_ROUTE = 512
T_ROW = 256
TM_MOE = 256
VMEM_LIMIT = 56 * 1024 * 1024


def _rms(x, g):
    return x * lax.rsqrt(jnp.mean(x * x, axis=-1, keepdims=True) + NORM_EPS) * g


def _gelu_tanh(x):
    return 0.5 * x * (1.0 + jnp.tanh(math.sqrt(2.0 / math.pi) * (x + 0.044715 * (x * x * x))))


def _sigmoid(x):
    return 1.0 / (1.0 + jnp.exp(-x))


def _block_diag(blocks):
    n, r, c = blocks.shape
    eye = jnp.eye(n, dtype=blocks.dtype)
    return jnp.einsum('nrc,nm->nrmc', blocks, eye).reshape(n * r, n * c)


def _mixer_kernel(x_ref, g_ref, win_ref, convw_ref, convb_ref, wg_ref, bg_ref, sp_ref,
                  bmat_ref, tab_ref, cmat_ref, dskip_ref, glu_ref, glub_ref, wout_ref,
                  o_ref,
                  xbuf, a_s, b_s, rg_carry, sre, sim, c_re, c_im):
    t = x_ref.shape[0]
    ngrp = t // SUBLANES
    c = pl.program_id(1)

    @pl.when(c == 0)
    def _():
        xbuf[...] = jnp.zeros_like(xbuf)
        rg_carry[...] = jnp.zeros_like(rg_carry)
        c_re[...] = jnp.zeros_like(c_re)
        c_im[...] = jnp.zeros_like(c_im)

    x = x_ref[...]
    xn = _rms(x, g_ref[...]).astype(BF16)
    proj = jnp.dot(xn, win_ref[...], preferred_element_type=F32)
    gate_rg = proj[:, :D_RG]
    x_rg = proj[:, D_RG:2 * D_RG]
    u = proj[:, 2 * D_RG:]

    xbuf[0:SUBLANES, :] = xbuf[t:t + SUBLANES, :]
    xbuf[SUBLANES:, :] = x_rg
    xfull = xbuf[...]
    xc = convb_ref[...] + convw_ref[CONV_W - 1:CONV_W, :] * x_rg
    for k in range(CONV_W - 1):
        sh = pltpu.roll(xfull, CONV_W - 1 - k, axis=0)[SUBLANES:, :]
        xc = xc + convw_ref[k:k + 1, :] * sh
    gates = jnp.dot(xc.astype(BF16), wg_ref[...], preferred_element_type=F32) + bg_ref[...]
    r = _sigmoid(gates[:, :D_RG])
    ig = _sigmoid(gates[:, D_RG:])
    log_a = (-RG_C) * r * sp_ref[...]
    a = jnp.exp(log_a)
    mult = jnp.sqrt(1.0 - a * a)
    row = lax.broadcasted_iota(jnp.int32, (t, D_RG), 0)
    mult = jnp.where((row == 0) & (c == 0), 1.0, mult)
    a_s[...] = a
    b_s[...] = mult * (ig * xc)

    sub = lax.broadcasted_iota(jnp.int32, (SUBLANES, D_RG), 0)

    def rg_body(j, carry):
        r0 = pl.multiple_of(j * SUBLANES, SUBLANES)
        aa = a_s[pl.ds(r0, SUBLANES), :]
        bb = b_s[pl.ds(r0, SUBLANES), :]
        for d in (1, 2, 4):
            a_sh = jnp.where(sub >= d, pltpu.roll(aa, d, axis=0), 1.0)
            b_sh = jnp.where(sub >= d, pltpu.roll(bb, d, axis=0), 0.0)
            bb = aa * b_sh + bb
            aa = aa * a_sh
        h = bb + aa * carry
        b_s[pl.ds(r0, SUBLANES), :] = h
        return h[SUBLANES - 1:SUBLANES, :]

    rg_carry[...] = lax.fori_loop(0, ngrp, rg_body, rg_carry[...], unroll=2)
    y_rg = b_s[...] * _gelu_tanh(gate_rg)

    ub = u.astype(BF16)
    half = N_STATE // 2
    for hf in range(2):
        bu = jnp.dot(ub[:, hf * 256:(hf + 1) * 256], bmat_ref[hf], preferred_element_type=F32)
        sre[:, hf * half:(hf + 1) * half] = bu[:, :half]
        sim[:, hf * half:(hf + 1) * half] = bu[:, half:]

    lb = 1024
    for blk in range(N_STATE // lb):
        cs = slice(blk * lb, (blk + 1) * lb)
        md = [(tab_ref[2 * i, :, cs], tab_ref[2 * i + 1, :, cs]) for i in range(3)]
        pr = tab_ref[6, :, cs]
        pi_ = tab_ref[7, :, cs]

        def s5_body(j, carry, cs=cs, md=md, pr=pr, pi_=pi_):
            cr, ci = carry
            r0 = pl.multiple_of(j * SUBLANES, SUBLANES)
            re = sre[pl.ds(r0, SUBLANES), cs]
            im = sim[pl.ds(r0, SUBLANES), cs]
            for i, d in enumerate((1, 2, 4)):
                mr, mi = md[i]
                rr = pltpu.roll(re, d, axis=0)
                ri = pltpu.roll(im, d, axis=0)
                re, im = re + (mr * rr - mi * ri), im + (mr * ri + mi * rr)
            hr = re + (pr * cr - pi_ * ci)
            hi = im + (pr * ci + pi_ * cr)
            sre[pl.ds(r0, SUBLANES), cs] = hr
            sim[pl.ds(r0, SUBLANES), cs] = hi
            return hr[SUBLANES - 1:SUBLANES, :], hi[SUBLANES - 1:SUBLANES, :]

        cr, ci = lax.fori_loop(0, ngrp, s5_body, (c_re[:, cs], c_im[:, cs]), unroll=2)
        c_re[:, cs] = cr
        c_im[:, cs] = ci

    ys = []
    for hf in range(2):
        hcat = jnp.concatenate([sre[:, hf * half:(hf + 1) * half], sim[:, hf * half:(hf + 1) * half]], axis=1)
        ys.append(jnp.dot(hcat.astype(BF16), cmat_ref[hf], preferred_element_type=F32))
    y = jnp.concatenate(ys, axis=1) + dskip_ref[...] * u
    z = _gelu_tanh(y)
    zg = _sigmoid(jnp.dot(z.astype(BF16), glu_ref[...], preferred_element_type=F32) + glub_ref[...])
    y_s5 = z * zg

    ycat = jnp.concatenate([y_rg, y_s5], axis=1).astype(BF16)
    o_ref[...] = x + jnp.dot(ycat, wout_ref[...], preferred_element_type=F32)


def _mixer(x2, bsz, seq, p):
    t = min(T_MIX, seq)
    nc = seq // t
    const = lambda shape: pl.BlockSpec(shape, lambda b, c: (0,) * len(shape))
    return pl.pallas_call(
        _mixer_kernel,
        out_shape=jax.ShapeDtypeStruct(x2.shape, F32),
        grid=(bsz, nc),
        in_specs=[
            pl.BlockSpec((t, D_MODEL), lambda b, c: (b * nc + c, 0)),
            const((1, D_MODEL)), const((D_MODEL, 3 * D_RG)), const((CONV_W, D_RG)), const((1, D_RG)),
            const((D_RG, 2 * D_RG)), const((1, 2 * D_RG)), const((1, D_RG)),
            const((2, 256, N_STATE)), const((8, SUBLANES, N_STATE)), const((2, N_STATE, 256)),
            const((1, D_S5)), const((D_S5, D_S5)), const((1, D_S5)), const((D_MODEL, D_MODEL)),
        ],
        out_specs=pl.BlockSpec((t, D_MODEL), lambda b, c: (b * nc + c, 0)),
        scratch_shapes=[
            pltpu.VMEM((t + SUBLANES, D_RG), F32), pltpu.VMEM((t, D_RG), F32), pltpu.VMEM((t, D_RG), F32),
            pltpu.VMEM((1, D_RG), F32),
            pltpu.VMEM((t, N_STATE), F32), pltpu.VMEM((t, N_STATE), F32),
            pltpu.VMEM((1, N_STATE), F32), pltpu.VMEM((1, N_STATE), F32),
        ],
        compiler_params=pltpu.CompilerParams(
            dimension_semantics=("arbitrary", "arbitrary"), vmem_limit_bytes=VMEM_LIMIT),
        name="mixer",
    )(x2, p["g_mix"], p["w_in"], p["conv_w"], p["conv_b"], p["w_gates"], p["b_gates"], p["sp"],
      p["bmat"], p["s5tab"], p["cmat"], p["d_skip"], p["glu_w"], p["glu_b"], p["w_out"])


def _kv_kernel(m_ref, g_ref, w_ref, k_ref, v_ref):
    mn = _rms(m_ref[...], g_ref[...]).astype(BF16)
    kv = jnp.dot(mn, w_ref[...], preferred_element_type=F32)
    k_ref[...] = kv[:, :D_MODEL].astype(BF16)
    v_ref[...] = kv[:, D_MODEL:].astype(BF16)


def _kv(mem2, bsz, mlen, g, wkv):
    return pl.pallas_call(
        _kv_kernel,
        out_shape=(jax.ShapeDtypeStruct(mem2.shape, BF16), jax.ShapeDtypeStruct(mem2.shape, BF16)),
        grid=(bsz,),
        in_specs=[pl.BlockSpec((mlen, D_MODEL), lambda b: (b, 0)),
                  pl.BlockSpec((1, D_MODEL), lambda b: (0, 0)),
                  pl.BlockSpec((D_MODEL, 2 * D_MODEL), lambda b: (0, 0))],
        out_specs=(pl.BlockSpec((mlen, D_MODEL), lambda b: (b, 0)),
                   pl.BlockSpec((mlen, D_MODEL), lambda b: (b, 0))),
        compiler_params=pltpu.CompilerParams(dimension_semantics=("arbitrary",), vmem_limit_bytes=VMEM_LIMIT),
        name="kv",
    )(mem2, g, wkv)


def _attn_kernel(h_ref, g_ref, wq_ref, k_ref, v_ref, wo_ref, o_ref):
    h = h_ref[...]
    xn = _rms(h, g_ref[...]).astype(BF16)
    q = jnp.dot(xn, wq_ref[...], preferred_element_type=F32).astype(BF16)
    outs = []
    for hd in range(XA_HEADS):
        cs = slice(hd * XA_HD, (hd + 1) * XA_HD)
        s = lax.dot_general(q[:, cs], k_ref[:, cs], (((1,), (1,)), ((), ())),
                            preferred_element_type=F32) * (XA_HD ** -0.5)
        s = s - jnp.max(s, axis=-1, keepdims=True)
        e = jnp.exp(s)
        pr = e / jnp.sum(e, axis=-1, keepdims=True)
        outs.append(jnp.dot(pr.astype(BF16), v_ref[:, cs], preferred_element_type=F32))
    o = jnp.concatenate(outs, axis=1).astype(BF16)
    o_ref[...] = h + jnp.dot(o, wo_ref[...], preferred_element_type=F32)


def _attn(h2, bsz, seq, mlen, g, wq, k, v, wo):
    t = min(T_ATT, seq)
    nc = seq // t
    return pl.pallas_call(
        _attn_kernel,
        out_shape=jax.ShapeDtypeStruct(h2.shape, F32),
        grid=(bsz, nc),
        in_specs=[pl.BlockSpec((t, D_MODEL), lambda b, c: (b * nc + c, 0)),
                  pl.BlockSpec((1, D_MODEL), lambda b, c: (0, 0)),
                  pl.BlockSpec((D_MODEL, D_MODEL), lambda b, c: (0, 0)),
                  pl.BlockSpec((mlen, D_MODEL), lambda b, c: (b, 0)),
                  pl.BlockSpec((mlen, D_MODEL), lambda b, c: (b, 0)),
                  pl.BlockSpec((D_MODEL, D_MODEL), lambda b, c: (0, 0))],
        out_specs=pl.BlockSpec((t, D_MODEL), lambda b, c: (b * nc + c, 0)),
        compiler_params=pltpu.CompilerParams(
            dimension_semantics=("arbitrary", "arbitrary"), vmem_limit_bytes=VMEM_LIMIT),
        name="attn",
    )(h2, g, wq, k, v, wo)


def _route_kernel(h_ref, g_ref, wr_ref, br_ref, xn_ref, idx_ref, gate_ref, rank_ref, cnt_ref, run):
    t = h_ref.shape[0]

    @pl.when(pl.program_id(0) == 0)
    def _():
        run[...] = jnp.zeros_like(run)

    xn = _rms(h_ref[...], g_ref[...])
    xn_ref[...] = xn
    lg = lax.dot_general(wr_ref[...], xn, (((1,), (1,)), ((), ())),
                         precision=lax.Precision.HIGHEST, preferred_element_type=F32) + br_ref[...]
    erow = lax.broadcasted_iota(jnp.int32, (N_EXPERTS, t), 0)
    vals, idxs, hots = [], [], []
    for _ in range(TOP_K):
        m = jnp.max(lg, axis=0, keepdims=True)
        ix = jnp.min(jnp.where(lg == m, erow, N_EXPERTS), axis=0, keepdims=True)
        hot = erow == ix
        lg = jnp.where(hot, -jnp.inf, lg)
        vals.append(m)
        idxs.append(ix)
        hots.append(hot)
    es = [jnp.exp(v - vals[0]) for v in vals]
    den = es[0] + es[1] + es[2] + es[3]
    sel = jnp.zeros((N_EXPERTS, t), F32)
    for hot in hots:
        sel = sel + jnp.where(hot, 1.0, 0.0)
    tri = jnp.where(lax.broadcasted_iota(jnp.int32, (t, t), 0) < lax.broadcasted_iota(jnp.int32, (t, t), 1),
                    1.0, 0.0).astype(BF16)
    cnt = jnp.dot(sel.astype(BF16), tri, preferred_element_type=F32) + run[...]
    for k in range(TOP_K):
        idx_ref[k:k + 1, :] = idxs[k]
        gate_ref[k:k + 1, :] = es[k] / den
        rank_ref[k:k + 1, :] = jnp.sum(jnp.where(hots[k], cnt, 0.0), axis=0, keepdims=True).astype(jnp.int32)
    run[...] = run[...] + jnp.sum(sel, axis=1, keepdims=True)
    cnt_ref[...] = jnp.broadcast_to(run[...], cnt_ref.shape).astype(jnp.int32)


def _route(h2, g, wr_t, br):
    n = h2.shape[0]
    t = min(T_ROUTE, n)
    return pl.pallas_call(
        _route_kernel,
        out_shape=(jax.ShapeDtypeStruct((n, D_MODEL), F32),
                   jax.ShapeDtypeStruct((TOP_K, n), jnp.int32),
                   jax.ShapeDtypeStruct((TOP_K, n), F32),
                   jax.ShapeDtypeStruct((TOP_K, n), jnp.int32),
                   jax.ShapeDtypeStruct((N_EXPERTS, 128), jnp.int32)),
        grid=(n // t,),
        in_specs=[pl.BlockSpec((t, D_MODEL), lambda i: (i, 0)),
                  pl.BlockSpec((1, D_MODEL), lambda i: (0, 0)),
                  pl.BlockSpec((N_EXPERTS, D_MODEL), lambda i: (0, 0)),
                  pl.BlockSpec((N_EXPERTS, 1), lambda i: (0, 0))],
        out_specs=(pl.BlockSpec((t, D_MODEL), lambda i: (i, 0)),
                   pl.BlockSpec((TOP_K, t), lambda i: (0, i)),
                   pl.BlockSpec((TOP_K, t), lambda i: (0, i)),
                   pl.BlockSpec((TOP_K, t), lambda i: (0, i)),
                   pl.BlockSpec((N_EXPERTS, 128), lambda i: (0, 0))),
        scratch_shapes=[pltpu.VMEM((N_EXPERTS, 1), F32)],
        compiler_params=pltpu.CompilerParams(dimension_semantics=("arbitrary",), vmem_limit_bytes=VMEM_LIMIT),
        name="route",
    )(h2, g, wr_t, br)


def _dispatch_kernel(dest_hbm, xn_ref, buf_in, out_hbm, dest_s, sem_i, sem_r):
    del buf_in
    t = xn_ref.shape[0]
    i = pl.program_id(0)
    n_idx = TOP_K * t
    cp = pltpu.make_async_copy(dest_hbm.at[pl.ds(pl.multiple_of(i * n_idx, n_idx), n_idx)], dest_s, sem_i)
    cp.start()
    cp.wait()

    def row_copy(tok, slot):
        return pltpu.make_async_copy(xn_ref.at[pl.ds(tok, 1), :], out_hbm.at[pl.ds(slot, 1), :], sem_r)

    for tok in range(t):
        for k in range(TOP_K):
            row_copy(tok, dest_s[TOP_K * tok + k]).start()
    for tok in range(t):
        for k in range(TOP_K):
            row_copy(tok, 0).wait()


def _dispatch(dest_flat, xn, n_slots):
    n = xn.shape[0]
    t = min(T_ROW, n)
    zeros = jnp.zeros((n_slots, D_MODEL), F32)
    return pl.pallas_call(
        _dispatch_kernel,
        out_shape=jax.ShapeDtypeStruct((n_slots, D_MODEL), F32),
        grid=(n // t,),
        in_specs=[pl.BlockSpec(memory_space=pl.ANY),
                  pl.BlockSpec((t, D_MODEL), lambda i: (i, 0)),
                  pl.BlockSpec(memory_space=pl.ANY)],
        out_specs=pl.BlockSpec(memory_space=pl.ANY),
        scratch_shapes=[pltpu.SMEM((TOP_K * t,), jnp.int32), pltpu.SemaphoreType.DMA, pltpu.SemaphoreType.DMA],
        input_output_aliases={2: 0},
        compiler_params=pltpu.CompilerParams(dimension_semantics=("arbitrary",), vmem_limit_bytes=VMEM_LIMIT),
        name="dispatch",
    )(dest_flat, xn, zeros)


def _moe_kernel(bexp_ref, nused_ref, x_ref, wgu_ref, bgu_ref, wd_ref, bd_ref, o_ref, wgu_s, wd_s):
    b = pl.program_id(0)
    prev = bexp_ref[jnp.maximum(b - 1, 0)]
    fresh = (b == 0) | (bexp_ref[b] != prev)

    @pl.when(fresh)
    def _():
        rows = 128
        for r0 in range(0, D_MODEL, rows):
            wgu_s[r0:r0 + rows, :] = wgu_ref[0, r0:r0 + rows, :].astype(BF16)
        for r0 in range(0, D_FF, rows):
            wd_s[r0:r0 + rows, :] = wd_ref[0, r0:r0 + rows, :].astype(BF16)

    @pl.when(b < nused_ref[0])
    def _():
        xb = x_ref[...].astype(BF16)
        gu = jnp.dot(xb, wgu_s[...], preferred_element_type=F32) + bgu_ref[0]
        g = jnp.minimum(gu[:, :D_FF], SWIGLU_LIMIT)
        u = jnp.clip(gu[:, D_FF:], -SWIGLU_LIMIT, SWIGLU_LIMIT)
        h = (u + 1.0) * (g * _sigmoid(SWIGLU_ALPHA * g))
        o_ref[...] = jnp.dot(h.astype(BF16), wd_s[...], preferred_element_type=F32) + bd_ref[0]

    @pl.when(b >= nused_ref[0])
    def _():
        o_ref[...] = jnp.zeros_like(o_ref)


def _moe(block_exp, n_used, xs, w_gu, b_gu, w_down, b_down):
    n_slots = xs.shape[0]
    nb = n_slots // TM_MOE
    grid_spec = pltpu.PrefetchScalarGridSpec(
        num_scalar_prefetch=2,
        grid=(nb,),
        in_specs=[
            pl.BlockSpec((TM_MOE, D_MODEL), lambda b, be, nu: (jnp.minimum(b, nu[0] - 1), 0)),
            pl.BlockSpec((1, D_MODEL, 2 * D_FF), lambda b, be, nu: (be[b], 0, 0)),
            pl.BlockSpec((1, 1, 2 * D_FF), lambda b, be, nu: (be[b], 0, 0)),
            pl.BlockSpec((1, D_FF, D_MODEL), lambda b, be, nu: (be[b], 0, 0)),
            pl.BlockSpec((1, 1, D_MODEL), lambda b, be, nu: (be[b], 0, 0)),
        ],
        out_specs=pl.BlockSpec((TM_MOE, D_MODEL), lambda b, be, nu: (b, 0)),
        scratch_shapes=[pltpu.VMEM((D_MODEL, 2 * D_FF), BF16), pltpu.VMEM((D_FF, D_MODEL), BF16)],
    )
    return pl.pallas_call(
        _moe_kernel,
        out_shape=jax.ShapeDtypeStruct((n_slots, D_MODEL), F32),
        grid_spec=grid_spec,
        compiler_params=pltpu.CompilerParams(dimension_semantics=("arbitrary",), vmem_limit_bytes=VMEM_LIMIT),
        name="moe",
    )(block_exp, n_used, xs, w_gu, b_gu, w_down, b_down)


def _combine_kernel(dest_hbm, ys_hbm, h_ref, gate_ref, g_ref, o_ref, dest_s, buf, sem_i, sem_r):
    t = h_ref.shape[0]
    i = pl.program_id(0)
    n_idx = TOP_K * t
    cp = pltpu.make_async_copy(dest_hbm.at[pl.ds(pl.multiple_of(i * n_idx, n_idx), n_idx)], dest_s, sem_i)
    cp.start()
    cp.wait()

    def row_copy(tok, k, slot):
        return pltpu.make_async_copy(ys_hbm.at[pl.ds(slot, 1), :], buf.at[k, pl.ds(tok, 1), :], sem_r)

    for tok in range(t):
        for k in range(TOP_K):
            row_copy(tok, k, dest_s[TOP_K * tok + k]).start()
    for tok in range(t):
        for k in range(TOP_K):
            row_copy(tok, k, 0).wait()

    acc = h_ref[...]
    gates = gate_ref[...]
    for k in range(TOP_K):
        acc = acc + gates[:, k:k + 1] * buf[k]
    o_ref[...] = _rms(acc, g_ref[...])


def _combine(dest_flat, ys, h2, gates_tok, g):
    n = h2.shape[0]
    t = min(T_ROW, n)
    return pl.pallas_call(
        _combine_kernel,
        out_shape=jax.ShapeDtypeStruct((n, D_MODEL), F32),
        grid=(n // t,),
        in_specs=[pl.BlockSpec(memory_space=pl.ANY),
                  pl.BlockSpec(memory_space=pl.ANY),
                  pl.BlockSpec((t, D_MODEL), lambda i: (i, 0)),
                  pl.BlockSpec((t, TOP_K), lambda i: (i, 0)),
                  pl.BlockSpec((1, D_MODEL), lambda i: (0, 0))],
        out_specs=pl.BlockSpec((t, D_MODEL), lambda i: (i, 0)),
        scratch_shapes=[pltpu.SMEM((TOP_K * t,), jnp.int32), pltpu.VMEM((TOP_K, t, D_MODEL), F32),
                        pltpu.SemaphoreType.DMA, pltpu.SemaphoreType.DMA],
        compiler_params=pltpu.CompilerParams(dimension_semantics=("arbitrary",), vmem_limit_bytes=VMEM_LIMIT),
        name="combine",
    )(dest_flat, ys, h2, gates_tok, g)


def _prep_mixer_params(norm_mix_g, w_in, conv_w, conv_b, rg_wa, rg_ba, rg_wx, rg_bx, rg_lambda,
                       s5_lambda_re, s5_lambda_im, s5_b_re, s5_b_im, s5_c_re, s5_c_im, s5_d, s5_log_dt,
                       s5_glu_w, s5_glu_b, w_out):
    row = lambda v: v.reshape(1, -1).astype(F32)
    dt = jnp.exp(s5_log_dt)[:, None]
    mag = jnp.exp(dt * s5_lambda_re)
    abar_re = mag * jnp.cos(dt * s5_lambda_im)
    abar_im = mag * jnp.sin(dt * s5_lambda_im)
    den = s5_lambda_re * s5_lambda_re + s5_lambda_im * s5_lambda_im
    num_re = abar_re - 1.0
    coef_re = (num_re * s5_lambda_re + abar_im * s5_lambda_im) / den
    coef_im = (abar_im * s5_lambda_re - num_re * s5_lambda_im) / den
    bbar_re = coef_re[..., None] * s5_b_re - coef_im[..., None] * s5_b_im
    bbar_im = coef_re[..., None] * s5_b_im + coef_im[..., None] * s5_b_re
    ar, ai = abar_re.reshape(-1), abar_im.reshape(-1)
    pw_re, pw_im = [ar], [ai]
    for _ in range(SUBLANES - 1):
        pr, pi_ = pw_re[-1], pw_im[-1]
        pw_re.append(pr * ar - pi_ * ai)
        pw_im.append(pr * ai + pi_ * ar)
    sub = jnp.arange(SUBLANES)[:, None]
    tabs = []
    for d in (1, 2, 4):
        keep = (sub >= d).astype(F32)
        tabs += [keep * pw_re[d - 1][None, :], keep * pw_im[d - 1][None, :]]
    tabs += [jnp.stack(pw_re), jnp.stack(pw_im)]
    s5tab = jnp.stack(tabs).astype(F32)
    gh = S5_GROUPS // 2
    bmat, cmat = [], []
    for hf in range(2):
        gs = slice(hf * gh, (hf + 1) * gh)
        b_re = _block_diag(jnp.transpose(bbar_re[gs], (0, 2, 1)))
        b_im = _block_diag(jnp.transpose(bbar_im[gs], (0, 2, 1)))
        bmat.append(jnp.concatenate([b_re, b_im], axis=1))
        cr = _block_diag(jnp.transpose(s5_c_re[gs], (0, 2, 1)))
        ci = _block_diag(jnp.transpose(s5_c_im[gs], (0, 2, 1)))
        cmat.append(jnp.concatenate([cr, -ci], axis=0))
    return {
        "g_mix": row(norm_mix_g), "w_in": w_in.astype(BF16), "conv_w": conv_w.astype(F32), "conv_b": row(conv_b),
        "w_gates": jnp.concatenate([_block_diag(rg_wa), _block_diag(rg_wx)], axis=1).astype(BF16),
        "b_gates": jnp.concatenate([rg_ba, rg_bx]).reshape(1, -1).astype(F32),
        "sp": row(jax.nn.softplus(-rg_lambda)),
        "bmat": jnp.stack(bmat).astype(BF16), "s5tab": s5tab, "cmat": jnp.stack(cmat).astype(BF16),
        "d_skip": row(s5_d), "glu_w": _block_diag(s5_glu_w).astype(BF16), "glu_b": row(s5_glu_b),
        "w_out": w_out.astype(BF16),
    }


def kernel(x, mem, norm_mix_g, w_in, conv_w, conv_b, rg_wa, rg_ba, rg_wx, rg_bx, rg_lambda, s5_lambda_re,
           s5_lambda_im, s5_b_re, s5_b_im, s5_c_re, s5_c_im, s5_d, s5_log_dt, s5_glu_w, s5_glu_b, w_out,
           norm_xa_g, mem_norm_g, xa_wq, xa_wkv, xa_wo, norm_ffn_g, router_w, router_b, exp_w_gu, exp_b_gu,
           exp_w_down, exp_b_down, final_norm_g):
    bsz, seq, d = x.shape
    mlen = mem.shape[1]
    n = bsz * seq
    row = lambda v: v.reshape(1, -1).astype(F32)

    p = _prep_mixer_params(norm_mix_g, w_in, conv_w, conv_b, rg_wa, rg_ba, rg_wx, rg_bx, rg_lambda,
                           s5_lambda_re, s5_lambda_im, s5_b_re, s5_b_im, s5_c_re, s5_c_im, s5_d, s5_log_dt,
                           s5_glu_w, s5_glu_b, w_out)
    h1 = _mixer(x.reshape(n, d), bsz, seq, p)

    k, v = _kv(mem.reshape(bsz * mlen, d), bsz, mlen, row(mem_norm_g), xa_wkv.astype(BF16))
    h2 = _attn(h1, bsz, seq, mlen, row(norm_xa_g), xa_wq.astype(BF16), k, v, xa_wo.astype(BF16))

    xn, idx_t, gate_t, rank_t, cnt = _route(h2, row(norm_ffn_g), router_w.T.astype(F32),
                                            router_b.reshape(-1, 1).astype(F32))
    counts = cnt[:, 0]
    padded = ((counts + TM_MOE - 1) // TM_MOE) * TM_MOE
    pend = jnp.cumsum(padded)
    pstart = pend - padded
    onehot = idx_t[:, :, None] == jnp.arange(N_EXPERTS, dtype=jnp.int32)
    dest_t = rank_t + jnp.sum(jnp.where(onehot, pstart, 0), axis=-1)
    dest_flat = dest_t.T.reshape(-1).astype(jnp.int32)
    n_slots = n * TOP_K + N_EXPERTS * TM_MOE
    nb = n_slots // TM_MOE
    n_used = (pend[-1] // TM_MOE).astype(jnp.int32)
    bstart = jnp.arange(nb, dtype=jnp.int32) * TM_MOE
    block_exp = jnp.minimum(jnp.sum(bstart[:, None] >= pend[None, :], axis=1), N_EXPERTS - 1)
    block_exp = jnp.where(jnp.arange(nb) < n_used, block_exp, block_exp[jnp.maximum(n_used - 1, 0)])

    xs = _dispatch(dest_flat, xn, n_slots)
    ys = _moe(block_exp.astype(jnp.int32), n_used.reshape(1), xs, exp_w_gu, exp_b_gu.reshape(N_EXPERTS, 1, -1),
              exp_w_down, exp_b_down.reshape(N_EXPERTS, 1, -1))
    out = _combine(dest_flat, ys, h2, gate_t.T, row(final_norm_g))
    return out.reshape(bsz, seq, d)
```

```python
import math

import jax
import jax.numpy as jnp
from jax import lax
from jax.experimental import pallas as pl
from jax.experimental.pallas import tpu as pltpu

F32 = jnp.float32
BF16 = jnp.bfloat16

D_MODEL = 1024
D_RG = 512
D_S5 = 512
CONV_W = 4
RG_C = 8.0
S5_GROUPS = 32
S5_STATE = 64
N_STATE = S5_GROUPS * S5_STATE
XA_HEADS = 4
XA_HD = 256
N_EXPERTS = 32
TOP_K = 4
D_FF = 1024
SWIGLU_LIMIT = 7.0
SWIGLU_ALPHA = 1.702
NORM_EPS = 1e-6

SUBLANES = 8
T_MIX = 512
T_ATT = 512
T_ROUTE = 512
T_COMB = 256
TM_MOE = 256
META_W = 1024
VMEM_LIMIT = 56 * 1024 * 1024


def _rms(x, g):
    return x * lax.rsqrt(jnp.mean(x * x, axis=-1, keepdims=True) + NORM_EPS) * g


def _gelu_tanh(x):
    return 0.5 * x * (1.0 + jnp.tanh(math.sqrt(2.0 / math.pi) * (x + 0.044715 * (x * x * x))))


def _sigmoid(x):
    return 1.0 / (1.0 + jnp.exp(-x))


def _block_diag(blocks):
    n, r, c = blocks.shape
    eye = jnp.eye(n, dtype=blocks.dtype)
    return jnp.einsum('nrc,nm->nrmc', blocks, eye).reshape(n * r, n * c)


def _sublane_chain(al, e, carry, sub):
    for d in (1, 2, 4):
        al_sh = jnp.where(sub >= d, pltpu.roll(al, d, axis=0), 1.0)
        e_sh = jnp.where(sub >= d, pltpu.roll(e, d, axis=0), 0.0)
        e = al * e_sh + e
        al = al * al_sh
    s_end = e + al * carry
    start = jnp.where(sub == 0, carry, pltpu.roll(s_end, 1, axis=0))
    return start, s_end[SUBLANES - 1:SUBLANES, :]


def _mixer_kernel(x_ref, g_ref, win_ref, convw_ref, convb_ref, wg_ref, bg_ref, sp_ref,
                  bmat_ref, tab_ref, cmat_ref, dskip_ref, glu_ref, glub_ref, wout_ref,
                  o_ref,
                  xin, xout, sem_i, sem_o, xbuf, hist, a_s, b_s, rg_carry, sre, sim, c_re, c_im):
    ngrp = xin.shape[1]
    t = ngrp * SUBLANES
    nh = (CONV_W - 1) * SUBLANES
    c = pl.program_id(1)
    step = pl.program_id(0) * pl.num_programs(1) + c
    nsteps = pl.num_programs(0) * pl.num_programs(1)
    slot = lax.rem(step, 2)

    def in_copy(s, sl, i):
        return pltpu.make_async_copy(x_ref.at[pl.ds(s * t + i * ngrp, ngrp), :], xin.at[sl, :, i, :], sem_i.at[sl])

    def out_copy(s, sl, i):
        return pltpu.make_async_copy(xout.at[sl, :, i, :], o_ref.at[pl.ds(s * t + i * ngrp, ngrp), :], sem_o.at[sl])

    @pl.when(step == 0)
    def _():
        for i in range(SUBLANES):
            in_copy(0, 0, i).start()

    @pl.when(c == 0)
    def _():
        hist[...] = jnp.zeros_like(hist)
        rg_carry[...] = jnp.zeros_like(rg_carry)
        c_re[...] = jnp.zeros_like(c_re)
        c_im[...] = jnp.zeros_like(c_im)

    for i in range(SUBLANES):
        in_copy(step, slot, i).wait()

    @pl.when(step + 1 < nsteps)
    def _():
        for i in range(SUBLANES):
            in_copy(step + 1, 1 - slot, i).start()

    x = xin[slot].reshape(t, D_MODEL)
    xn = _rms(x, g_ref[...]).astype(BF16)
    proj = jnp.dot(xn, win_ref[...], preferred_element_type=F32)
    gate_rg = proj[:, :D_RG]
    x_rg = proj[:, D_RG:2 * D_RG]
    u = proj[:, 2 * D_RG:]

    sub = lax.broadcasted_iota(jnp.int32, (SUBLANES, D_RG), 0)
    cur_tail = x_rg[t - nh:, :]
    for gidx in range(CONV_W - 1):
        rs = slice(gidx * SUBLANES, (gidx + 1) * SUBLANES)
        xbuf[rs, :] = jnp.where(sub == 0, pltpu.roll(hist[rs, :], 1, axis=0), pltpu.roll(cur_tail[rs, :], 1, axis=0))
    hist[...] = cur_tail
    xbuf[nh:, :] = x_rg
    xc = convb_ref[...] + convw_ref[CONV_W - 1:CONV_W, :] * x_rg
    for k in range(CONV_W - 1):
        xc = xc + convw_ref[k:k + 1, :] * xbuf[k * SUBLANES:k * SUBLANES + t, :]
    gates = jnp.dot(xc.astype(BF16), wg_ref[...], preferred_element_type=F32) + bg_ref[...]
    r = _sigmoid(gates[:, :D_RG])
    ig = _sigmoid(gates[:, D_RG:])
    log_a = (-RG_C) * r * sp_ref[...]
    a = jnp.exp(log_a)
    mult = jnp.sqrt(1.0 - a * a)
    row = lax.broadcasted_iota(jnp.int32, (t, D_RG), 0)
    mult = jnp.where((row == 0) & (c == 0), 1.0, mult)
    a_s[...] = a
    b_s[...] = mult * (ig * xc)

    def rg_pass(store):
        def body(q, carry):
            h, ac = carry
            r0 = pl.multiple_of(q * SUBLANES, SUBLANES)
            aa = a_s[pl.ds(r0, SUBLANES), :]
            h = aa * h + b_s[pl.ds(r0, SUBLANES), :]
            if store:
                b_s[pl.ds(r0, SUBLANES), :] = h
                return h, ac
            return h, aa * ac
        return body

    zeros8 = jnp.zeros((SUBLANES, D_RG), F32)
    ones8 = jnp.ones((SUBLANES, D_RG), F32)
    e_loc, a_loc = lax.fori_loop(0, ngrp, rg_pass(False), (zeros8, ones8), unroll=4)
    start, last = _sublane_chain(a_loc, e_loc, rg_carry[...], sub)
    rg_carry[...] = last
    lax.fori_loop(0, ngrp, rg_pass(True), (start, ones8), unroll=4)
    y_rg = b_s[...] * _gelu_tanh(gate_rg)

    ub = u.astype(BF16)
    half = N_STATE // 2
    for hf in range(2):
        bu = jnp.dot(ub[:, hf * 256:(hf + 1) * 256], bmat_ref[hf], preferred_element_type=F32)
        sre[:, hf * half:(hf + 1) * half] = bu[:, :half]
        sim[:, hf * half:(hf + 1) * half] = bu[:, half:]

    lb = 1024
    subl = lax.broadcasted_iota(jnp.int32, (SUBLANES, lb), 0)
    for blk in range(N_STATE // lb):
        cs = slice(blk * lb, (blk + 1) * lb)
        ar = tab_ref[8, :, cs]
        ai = tab_ref[9, :, cs]

        def s5_pass(store, cs=cs, ar=ar, ai=ai):
            def body(q, carry):
                hr, hi = carry
                r0 = pl.multiple_of(q * SUBLANES, SUBLANES)
                nr = ar * hr - ai * hi + sre[pl.ds(r0, SUBLANES), cs]
                ni = ar * hi + ai * hr + sim[pl.ds(r0, SUBLANES), cs]
                if store:
                    sre[pl.ds(r0, SUBLANES), cs] = nr
                    sim[pl.ds(r0, SUBLANES), cs] = ni
                return nr, ni
            return body

        z8 = jnp.zeros((SUBLANES, lb), F32)
        er, ei = lax.fori_loop(0, ngrp, s5_pass(False), (z8, z8), unroll=4)
        for i, d in enumerate((1, 2, 4)):
            mr = tab_ref[2 * i, :, cs]
            mi = tab_ref[2 * i + 1, :, cs]
            rr = pltpu.roll(er, d, axis=0)
            ri = pltpu.roll(ei, d, axis=0)
            er, ei = er + (mr * rr - mi * ri), ei + (mr * ri + mi * rr)
        pr = tab_ref[6, :, cs]
        pi_ = tab_ref[7, :, cs]
        cr = c_re[:, cs]
        ci = c_im[:, cs]
        sr = er + (pr * cr - pi_ * ci)
        si = ei + (pr * ci + pi_ * cr)
        c_re[:, cs] = sr[SUBLANES - 1:SUBLANES, :]
        c_im[:, cs] = si[SUBLANES - 1:SUBLANES, :]
        st_r = jnp.where(subl == 0, cr, pltpu.roll(sr, 1, axis=0))
        st_i = jnp.where(subl == 0, ci, pltpu.roll(si, 1, axis=0))
        lax.fori_loop(0, ngrp, s5_pass(True), (st_r, st_i), unroll=4)

    ys = []
    for hf in range(2):
        hcat = jnp.concatenate([sre[:, hf * half:(hf + 1) * half], sim[:, hf * half:(hf + 1) * half]], axis=1)
        ys.append(jnp.dot(hcat.astype(BF16), cmat_ref[hf], preferred_element_type=F32))
    y = jnp.concatenate(ys, axis=1) + dskip_ref[...] * u
    z = _gelu_tanh(y)
    zg = _sigmoid(jnp.dot(z.astype(BF16), glu_ref[...], preferred_element_type=F32) + glub_ref[...])
    y_s5 = z * zg

    ycat = jnp.concatenate([y_rg, y_s5], axis=1).astype(BF16)
    res = x + jnp.dot(ycat, wout_ref[...], preferred_element_type=F32)

    @pl.when(step >= 2)
    def _():
        for i in range(SUBLANES):
            out_copy(step - 2, slot, i).wait()

    xout[slot] = res.reshape(ngrp, SUBLANES, D_MODEL)
    for i in range(SUBLANES):
        out_copy(step, slot, i).start()

    @pl.when(step == nsteps - 1)
    def _():
        for i in range(SUBLANES):
            out_copy(step, slot, i).wait()

        @pl.when(step >= 1)
        def _():
            for i in range(SUBLANES):
                out_copy(step - 1, 1 - slot, i).wait()


def _mixer(x2, bsz, seq, p):
    t = min(T_MIX, seq)
    nc = seq // t
    nh = (CONV_W - 1) * SUBLANES
    const = lambda shape: pl.BlockSpec(shape, lambda b, c: (0,) * len(shape))
    return pl.pallas_call(
        _mixer_kernel,
        out_shape=jax.ShapeDtypeStruct(x2.shape, F32),
        grid=(bsz, nc),
        in_specs=[
            pl.BlockSpec(memory_space=pl.ANY),
            const((1, D_MODEL)), const((D_MODEL, 3 * D_RG)), const((CONV_W, D_RG)), const((1, D_RG)),
            const((D_RG, 2 * D_RG)), const((1, 2 * D_RG)), const((1, D_RG)),
            const((2, 256, N_STATE)), const((10, SUBLANES, N_STATE)), const((2, N_STATE, 256)),
            const((1, D_S5)), const((D_S5, D_S5)), const((1, D_S5)), const((D_MODEL, D_MODEL)),
        ],
        out_specs=pl.BlockSpec(memory_space=pl.ANY),
        scratch_shapes=[
            pltpu.VMEM((2, t // SUBLANES, SUBLANES, D_MODEL), F32),
            pltpu.VMEM((2, t // SUBLANES, SUBLANES, D_MODEL), F32),
            pltpu.SemaphoreType.DMA((2,)), pltpu.SemaphoreType.DMA((2,)),
            pltpu.VMEM((t + nh, D_RG), F32), pltpu.VMEM((nh, D_RG), F32),
            pltpu.VMEM((t, D_RG), F32), pltpu.VMEM((t, D_RG), F32), pltpu.VMEM((1, D_RG), F32),
            pltpu.VMEM((t, N_STATE), F32), pltpu.VMEM((t, N_STATE), F32),
            pltpu.VMEM((1, N_STATE), F32), pltpu.VMEM((1, N_STATE), F32),
        ],
        compiler_params=pltpu.CompilerParams(
            dimension_semantics=("arbitrary", "arbitrary"), vmem_limit_bytes=VMEM_LIMIT),
        name="mixer",
    )(x2, p["g_mix"], p["w_in"], p["conv_w"], p["conv_b"], p["w_gates"], p["b_gates"], p["sp"],
      p["bmat"], p["s5tab"], p["cmat"], p["d_skip"], p["glu_w"], p["glu_b"], p["w_out"])


def _kv_kernel(m_ref, g_ref, w_ref, k_ref, v_ref):
    mn = _rms(m_ref[...], g_ref[...]).astype(BF16)
    kv = jnp.dot(mn, w_ref[...], preferred_element_type=F32)
    k_ref[...] = kv[:, :D_MODEL].astype(BF16)
    v_ref[...] = kv[:, D_MODEL:].astype(BF16)


def _kv(mem2, bsz, mlen, g, wkv):
    return pl.pallas_call(
        _kv_kernel,
        out_shape=(jax.ShapeDtypeStruct(mem2.shape, BF16), jax.ShapeDtypeStruct(mem2.shape, BF16)),
        grid=(bsz,),
        in_specs=[pl.BlockSpec((mlen, D_MODEL), lambda b: (b, 0)),
                  pl.BlockSpec((1, D_MODEL), lambda b: (0, 0)),
                  pl.BlockSpec((D_MODEL, 2 * D_MODEL), lambda b: (0, 0))],
        out_specs=(pl.BlockSpec((mlen, D_MODEL), lambda b: (b, 0)),
                   pl.BlockSpec((mlen, D_MODEL), lambda b: (b, 0))),
        compiler_params=pltpu.CompilerParams(dimension_semantics=("arbitrary",), vmem_limit_bytes=VMEM_LIMIT),
        name="kv",
    )(mem2, g, wkv)


def _attn_kernel(h_ref, g_ref, wq_ref, k_ref, v_ref, wo_ref, o_ref):
    h = h_ref[...]
    xn = _rms(h, g_ref[...]).astype(BF16)
    q = jnp.dot(xn, wq_ref[...], preferred_element_type=F32).astype(BF16)
    outs = []
    for hd in range(XA_HEADS):
        cs = slice(hd * XA_HD, (hd + 1) * XA_HD)
        s = lax.dot_general(q[:, cs], k_ref[:, cs], (((1,), (1,)), ((), ())),
                            preferred_element_type=F32) * (XA_HD ** -0.5)
        s = s - jnp.max(s, axis=-1, keepdims=True)
        e = jnp.exp(s)
        pr = e / jnp.sum(e, axis=-1, keepdims=True)
        outs.append(jnp.dot(pr.astype(BF16), v_ref[:, cs], preferred_element_type=F32))
    o = jnp.concatenate(outs, axis=1).astype(BF16)
    o_ref[...] = h + jnp.dot(o, wo_ref[...], preferred_element_type=F32)


def _attn(h2, bsz, seq, mlen, g, wq, k, v, wo):
    t = min(T_ATT, seq)
    nc = seq // t
    return pl.pallas_call(
        _attn_kernel,
        out_shape=jax.ShapeDtypeStruct(h2.shape, F32),
        grid=(bsz, nc),
        in_specs=[pl.BlockSpec((t, D_MODEL), lambda b, c: (b * nc + c, 0)),
                  pl.BlockSpec((1, D_MODEL), lambda b, c: (0, 0)),
                  pl.BlockSpec((D_MODEL, D_MODEL), lambda b, c: (0, 0)),
                  pl.BlockSpec((mlen, D_MODEL), lambda b, c: (b, 0)),
                  pl.BlockSpec((mlen, D_MODEL), lambda b, c: (b, 0)),
                  pl.BlockSpec((D_MODEL, D_MODEL), lambda b, c: (0, 0))],
        out_specs=pl.BlockSpec((t, D_MODEL), lambda b, c: (b * nc + c, 0)),
        compiler_params=pltpu.CompilerParams(
            dimension_semantics=("arbitrary", "arbitrary"), vmem_limit_bytes=VMEM_LIMIT),
        name="attn",
    )(h2, g, wq, k, v, wo)


def _route_kernel(h_ref, g_ref, wr_ref, br_ref, xn_ref, idx_ref, gate_ref, rank_ref, cnt_ref, run):
    t = h_ref.shape[0]

    @pl.when(pl.program_id(0) == 0)
    def _():
        run[...] = jnp.zeros_like(run)

    xn = _rms(h_ref[...], g_ref[...])
    for s in range(SUBLANES):
        xn_ref[pl.ds(s, t, stride=SUBLANES), :] = xn[:, s * 128:(s + 1) * 128]
    lg = lax.dot_general(wr_ref[...], xn, (((1,), (1,)), ((), ())),
                         precision=lax.Precision.HIGHEST, preferred_element_type=F32) + br_ref[...]
    erow = lax.broadcasted_iota(jnp.int32, (N_EXPERTS, t), 0)
    vals, idxs, hots = [], [], []
    for _ in range(TOP_K):
        m = jnp.max(lg, axis=0, keepdims=True)
        ix = jnp.min(jnp.where(lg == m, erow, N_EXPERTS), axis=0, keepdims=True)
        hot = erow == ix
        lg = jnp.where(hot, -jnp.inf, lg)
        vals.append(m)
        idxs.append(ix)
        hots.append(hot)
    es = [jnp.exp(v - vals[0]) for v in vals]
    den = es[0] + es[1] + es[2] + es[3]
    sel = jnp.zeros((N_EXPERTS, t), F32)
    for hot in hots:
        sel = sel + jnp.where(hot, 1.0, 0.0)
    tri = jnp.where(lax.broadcasted_iota(jnp.int32, (t, t), 0) < lax.broadcasted_iota(jnp.int32, (t, t), 1),
                    1.0, 0.0).astype(BF16)
    cnt = jnp.dot(sel.astype(BF16), tri, preferred_element_type=F32) + run[...]
    for k in range(TOP_K):
        idx_ref[k:k + 1, :] = idxs[k]
        gate_ref[k:k + 1, :] = es[k] / den
        rank_ref[k:k + 1, :] = jnp.sum(jnp.where(hots[k], cnt, 0.0), axis=0, keepdims=True).astype(jnp.int32)
    run[...] = run[...] + jnp.sum(sel, axis=1, keepdims=True)
    cnt_ref[...] = jnp.broadcast_to(run[...], cnt_ref.shape).astype(jnp.int32)


def _route(h2, g, wr_t, br):
    n = h2.shape[0]
    t = min(T_ROUTE, n)
    return pl.pallas_call(
        _route_kernel,
        out_shape=(jax.ShapeDtypeStruct((n * SUBLANES, 128), F32),
                   jax.ShapeDtypeStruct((TOP_K, n), jnp.int32),
                   jax.ShapeDtypeStruct((TOP_K, n), F32),
                   jax.ShapeDtypeStruct((TOP_K, n), jnp.int32),
                   jax.ShapeDtypeStruct((N_EXPERTS, 128), jnp.int32)),
        grid=(n // t,),
        in_specs=[pl.BlockSpec((t, D_MODEL), lambda i: (i, 0)),
                  pl.BlockSpec((1, D_MODEL), lambda i: (0, 0)),
                  pl.BlockSpec((N_EXPERTS, D_MODEL), lambda i: (0, 0)),
                  pl.BlockSpec((N_EXPERTS, 1), lambda i: (0, 0))],
        out_specs=(pl.BlockSpec((t * SUBLANES, 128), lambda i: (i, 0)),
                   pl.BlockSpec((TOP_K, t), lambda i: (0, i)),
                   pl.BlockSpec((TOP_K, t), lambda i: (0, i)),
                   pl.BlockSpec((TOP_K, t), lambda i: (0, i)),
                   pl.BlockSpec((N_EXPERTS, 128), lambda i: (0, 0))),
        scratch_shapes=[pltpu.VMEM((N_EXPERTS, 1), F32)],
        compiler_params=pltpu.CompilerParams(dimension_semantics=("arbitrary",), vmem_limit_bytes=VMEM_LIMIT),
        name="route",
    )(h2, g, wr_t, br)


N_CHUNK = 4
META_RING = 4


def _moe_kernel(bexp_ref, nused_ref, meta_hbm, xn_hbm, wgu_ref, bgu_ref, wd_ref, bd_ref, out_hbm,
                wgu_s, wd_s, xbuf, obuf, meta_s, sem_m, sem_g, sem_s):
    b = pl.program_id(0)
    nu = nused_ref[0]

    def meta_copy(blk):
        ring = lax.rem(blk, META_RING)
        return pltpu.make_async_copy(meta_hbm.at[pl.ds(pl.multiple_of(blk * META_W, META_W), META_W)],
                                     meta_s.at[ring], sem_m.at[ring])

    def gather_row(sl, r, tok8):
        return pltpu.make_async_copy(xn_hbm.at[pl.ds(pl.multiple_of(tok8, SUBLANES), SUBLANES), :],
                                     xbuf.at[sl, pl.ds(r * SUBLANES, SUBLANES), :], sem_g.at[sl])

    def scatter_row(sl, r, dst8):
        return pltpu.make_async_copy(obuf.at[sl, pl.ds(r * SUBLANES, SUBLANES), :],
                                     out_hbm.at[pl.ds(pl.multiple_of(dst8, SUBLANES), SUBLANES), :], sem_s.at[sl])

    prev = bexp_ref[jnp.maximum(b - 1, 0)]
    fresh = (b == 0) | (bexp_ref[b] != prev)

    @pl.when(fresh & (b < nu))
    def _():
        rows = 128
        for r0 in range(0, D_MODEL, rows):
            wgu_s[r0:r0 + rows, :] = wgu_ref[0, r0:r0 + rows, :].astype(BF16)
        for r0 in range(0, D_FF, rows):
            wd_s[r0:r0 + rows, :] = wd_ref[0, r0:r0 + rows, :].astype(BF16)

    @pl.when(b == 0)
    def _():
        cp = meta_copy(0)
        cp.start()
        cp.wait()
        for r in range(TM_MOE):
            gather_row(0, r, meta_s[0, r]).start()
        meta_copy(1).start()

    def step(slot, first):
        other = 1 - slot
        ring_next = lax.rem(b + 1, META_RING)
        ring_prev = lax.rem(b + META_RING - 1, META_RING)
        for r in range(TM_MOE):
            gather_row(slot, r, 0).wait()
        meta_copy(b + 1).wait()

        if not first:
            @pl.when(b >= 2)
            def _():
                for r in range(TM_MOE):
                    scatter_row(slot, r, 0).wait()

        xb = jnp.concatenate([xbuf[slot, pl.ds(s, TM_MOE, stride=SUBLANES), :] for s in range(SUBLANES)],
                             axis=1).astype(BF16)
        rows = TM_MOE // N_CHUNK
        cols = D_FF // N_CHUNK
        y = bd_ref[0]
        for j in range(N_CHUNK):
            for r in range(j * rows, (j + 1) * rows):
                gather_row(other, r, meta_s[ring_next, r]).start()
            if not first:
                for r in range(j * rows, (j + 1) * rows):
                    scatter_row(other, r, meta_s[ring_prev, TM_MOE + r]).start(priority=1)
            cg = slice(j * cols, (j + 1) * cols)
            cu = slice(D_FF + j * cols, D_FF + (j + 1) * cols)
            g = jnp.dot(xb, wgu_s[:, cg], preferred_element_type=F32) + bgu_ref[0, :, cg]
            u = jnp.dot(xb, wgu_s[:, cu], preferred_element_type=F32) + bgu_ref[0, :, cu]
            g = jnp.minimum(g, SWIGLU_LIMIT)
            u = jnp.clip(u, -SWIGLU_LIMIT, SWIGLU_LIMIT)
            h = (u + 1.0) * (g * _sigmoid(SWIGLU_ALPHA * g))
            y = y + jnp.dot(h.astype(BF16), wd_s[cg, :], preferred_element_type=F32)
        for s in range(SUBLANES):
            obuf[slot, pl.ds(s, TM_MOE, stride=SUBLANES), :] = y[:, s * 128:(s + 1) * 128]

        @pl.when(b + 2 <= nu)
        def _():
            meta_copy(b + 2).start()

        @pl.when(b == nu - 1)
        def _():
            ring_cur = lax.rem(b, META_RING)
            for r in range(TM_MOE):
                gather_row(other, r, 0).wait()
            for r in range(TM_MOE):
                scatter_row(slot, r, meta_s[ring_cur, TM_MOE + r]).start(priority=1)
            if not first:
                for r in range(TM_MOE):
                    scatter_row(other, r, 0).wait()
            for r in range(TM_MOE):
                scatter_row(slot, r, 0).wait()

    @pl.when(b >= nu)
    def _():
        cp = meta_copy(b)
        cp.start()
        cp.wait()
        obuf[0] = jnp.zeros(obuf.shape[1:], F32)
        dst8 = pl.multiple_of(meta_s[lax.rem(b, META_RING), TM_MOE], SUBLANES)
        zc = pltpu.make_async_copy(obuf.at[0], out_hbm.at[pl.ds(dst8, TM_MOE * SUBLANES), :], sem_s.at[0])
        zc.start()
        zc.wait()

    pl.when(b == 0)(lambda: step(0, True))
    for par in range(2):
        pl.when((b > 0) & (b < nu) & (lax.rem(b, 2) == par))(lambda par=par: step(par, False))


def _moe(block_exp, n_used, meta, xn3, w_gu, b_gu, w_down, b_down):
    nb = block_exp.shape[0]
    wspec = lambda shape: pl.BlockSpec(shape, lambda b, be, nu: (be[b], 0, 0))
    grid_spec = pltpu.PrefetchScalarGridSpec(
        num_scalar_prefetch=2,
        grid=(nb,),
        in_specs=[
            pl.BlockSpec(memory_space=pl.ANY),
            pl.BlockSpec(memory_space=pl.ANY),
            wspec((1, D_MODEL, 2 * D_FF)), wspec((1, 1, 2 * D_FF)),
            wspec((1, D_FF, D_MODEL)), wspec((1, 1, D_MODEL)),
        ],
        out_specs=pl.BlockSpec(memory_space=pl.ANY),
        scratch_shapes=[pltpu.VMEM((D_MODEL, 2 * D_FF), BF16), pltpu.VMEM((D_FF, D_MODEL), BF16),
                        pltpu.VMEM((2, TM_MOE * SUBLANES, 128), F32), pltpu.VMEM((2, TM_MOE * SUBLANES, 128), F32),
                        pltpu.SMEM((META_RING, META_W), jnp.int32),
                        pltpu.SemaphoreType.DMA((META_RING,)), pltpu.SemaphoreType.DMA((2,)),
                        pltpu.SemaphoreType.DMA((2,))],
    )
    return pl.pallas_call(
        _moe_kernel,
        out_shape=jax.ShapeDtypeStruct((nb * TM_MOE * SUBLANES, 128), F32),
        grid_spec=grid_spec,
        compiler_params=pltpu.CompilerParams(dimension_semantics=("arbitrary",), vmem_limit_bytes=VMEM_LIMIT),
        name="moe",
    )(block_exp, n_used, meta, xn3, w_gu, b_gu, w_down, b_down)


def _combine_kernel(y0_ref, y1_ref, y2_ref, y3_ref, h_ref, gate_ref, g_ref, o_ref):
    t = h_ref.shape[0]
    acc = h_ref[...]
    gates = gate_ref[...]
    for k, y_ref in enumerate((y0_ref, y1_ref, y2_ref, y3_ref)):
        yk = jnp.concatenate([y_ref[pl.ds(s, t, stride=SUBLANES), :] for s in range(SUBLANES)], axis=1)
        acc = acc + gates[:, k:k + 1] * yk
    o_ref[...] = _rms(acc, g_ref[...])


def _combine(y2d, h2, gates_tok, g):
    n = h2.shape[0]
    t = min(T_COMB, n)
    nt = n // t
    yspec = lambda k: pl.BlockSpec((t * SUBLANES, 128), lambda i: (k * nt + i, 0))
    return pl.pallas_call(
        _combine_kernel,
        out_shape=jax.ShapeDtypeStruct((n, D_MODEL), F32),
        grid=(nt,),
        in_specs=[yspec(0), yspec(1), yspec(2), yspec(3),
                  pl.BlockSpec((t, D_MODEL), lambda i: (i, 0)),
                  pl.BlockSpec((t, TOP_K), lambda i: (i, 0)),
                  pl.BlockSpec((1, D_MODEL), lambda i: (0, 0))],
        out_specs=pl.BlockSpec((t, D_MODEL), lambda i: (i, 0)),
        compiler_params=pltpu.CompilerParams(dimension_semantics=("arbitrary",), vmem_limit_bytes=VMEM_LIMIT),
        name="combine",
    )(y2d, y2d, y2d, y2d, h2, gates_tok, g)


def _cmul(a, b):
    return a[0] * b[0] - a[1] * b[1], a[0] * b[1] + a[1] * b[0]


def _prep_mixer_params(sub_len, norm_mix_g, w_in, conv_w, conv_b, rg_wa, rg_ba, rg_wx, rg_bx, rg_lambda,
                       s5_lambda_re, s5_lambda_im, s5_b_re, s5_b_im, s5_c_re, s5_c_im, s5_d, s5_log_dt,
                       s5_glu_w, s5_glu_b, w_out):
    row = lambda v: v.reshape(1, -1).astype(F32)
    dt = jnp.exp(s5_log_dt)[:, None]
    mag = jnp.exp(dt * s5_lambda_re)
    abar_re = mag * jnp.cos(dt * s5_lambda_im)
    abar_im = mag * jnp.sin(dt * s5_lambda_im)
    den = s5_lambda_re * s5_lambda_re + s5_lambda_im * s5_lambda_im
    num_re = abar_re - 1.0
    coef_re = (num_re * s5_lambda_re + abar_im * s5_lambda_im) / den
    coef_im = (abar_im * s5_lambda_re - num_re * s5_lambda_im) / den
    bbar_re = coef_re[..., None] * s5_b_re - coef_im[..., None] * s5_b_im
    bbar_im = coef_re[..., None] * s5_b_im + coef_im[..., None] * s5_b_re
    a1 = (abar_re.reshape(-1), abar_im.reshape(-1))
    aq = a1
    for _ in range(int(math.log2(sub_len))):
        aq = _cmul(aq, aq)
    pw = [aq]
    for _ in range(SUBLANES - 1):
        pw.append(_cmul(pw[-1], aq))
    sub = jnp.arange(SUBLANES)[:, None]
    tabs = []
    for d in (1, 2, 4):
        keep = (sub >= d).astype(F32)
        tabs += [keep * pw[d - 1][0][None, :], keep * pw[d - 1][1][None, :]]
    tabs += [jnp.stack([p_[0] for p_ in pw]), jnp.stack([p_[1] for p_ in pw])]
    tabs += [jnp.broadcast_to(a1[0][None, :], (SUBLANES, N_STATE)),
             jnp.broadcast_to(a1[1][None, :], (SUBLANES, N_STATE))]
    s5tab = jnp.stack(tabs).astype(F32)
    gh = S5_GROUPS // 2
    bmat, cmat = [], []
    for hf in range(2):
        gs = slice(hf * gh, (hf + 1) * gh)
        b_re = _block_diag(jnp.transpose(bbar_re[gs], (0, 2, 1)))
        b_im = _block_diag(jnp.transpose(bbar_im[gs], (0, 2, 1)))
        bmat.append(jnp.concatenate([b_re, b_im], axis=1))
        cr = _block_diag(jnp.transpose(s5_c_re[gs], (0, 2, 1)))
        ci = _block_diag(jnp.transpose(s5_c_im[gs], (0, 2, 1)))
        cmat.append(jnp.concatenate([cr, -ci], axis=0))
    return {
        "g_mix": row(norm_mix_g), "w_in": w_in.astype(BF16), "conv_w": conv_w.astype(F32), "conv_b": row(conv_b),
        "w_gates": jnp.concatenate([_block_diag(rg_wa), _block_diag(rg_wx)], axis=1).astype(BF16),
        "b_gates": jnp.concatenate([rg_ba, rg_bx]).reshape(1, -1).astype(F32),
        "sp": row(jax.nn.softplus(-rg_lambda)),
        "bmat": jnp.stack(bmat).astype(BF16), "s5tab": s5tab, "cmat": jnp.stack(cmat).astype(BF16),
        "d_skip": row(s5_d), "glu_w": _block_diag(s5_glu_w).astype(BF16), "glu_b": row(s5_glu_b),
        "w_out": w_out.astype(BF16),
    }


def _slot_tables(idx_t, rank_t, counts, n):
    padded = ((counts + TM_MOE - 1) // TM_MOE) * TM_MOE
    pend = jnp.cumsum(padded)
    pstart = pend - padded
    onehot = idx_t[:, :, None] == jnp.arange(N_EXPERTS, dtype=jnp.int32)
    dest_t = rank_t + jnp.sum(jnp.where(onehot, pstart, 0), axis=-1)
    n_slots = n * TOP_K + N_EXPERTS * TM_MOE
    nb = n_slots // TM_MOE
    n_used = (pend[-1] // TM_MOE).astype(jnp.int32)
    bstart = jnp.arange(nb, dtype=jnp.int32) * TM_MOE
    block_exp = jnp.minimum(jnp.sum(bstart[:, None] >= pend[None, :], axis=1), N_EXPERTS - 1).astype(jnp.int32)
    used = jnp.arange(nb) < n_used
    block_exp = jnp.where(used, block_exp, block_exp[jnp.maximum(n_used - 1, 0)])
    tok = jnp.broadcast_to(jnp.arange(n, dtype=jnp.int32)[None, :], (TOP_K, n))
    out_row = tok + jnp.arange(TOP_K, dtype=jnp.int32)[:, None] * n
    vals = jnp.stack([tok.reshape(-1), out_row.reshape(-1), jnp.ones((TOP_K * n,), jnp.int32)], axis=1)
    inv = jnp.zeros((n_slots, 3), jnp.int32).at[dest_t.reshape(-1)].set(vals, unique_indices=True)
    is_pad = 1 - inv[:, 2]
    pad_row = TOP_K * n + jnp.cumsum(is_pad) - 1
    out_rows = jnp.where(is_pad == 1, pad_row, inv[:, 1]).astype(jnp.int32)
    meta = jnp.concatenate([(inv[:, 0] * SUBLANES).reshape(nb, TM_MOE), (out_rows * SUBLANES).reshape(nb, TM_MOE),
                            jnp.zeros((nb, META_W - 2 * TM_MOE), jnp.int32)], axis=1).reshape(-1)
    return block_exp, n_used.reshape(1), meta


def kernel(x, mem, norm_mix_g, w_in, conv_w, conv_b, rg_wa, rg_ba, rg_wx, rg_bx, rg_lambda, s5_lambda_re,
           s5_lambda_im, s5_b_re, s5_b_im, s5_c_re, s5_c_im, s5_d, s5_log_dt, s5_glu_w, s5_glu_b, w_out,
           norm_xa_g, mem_norm_g, xa_wq, xa_wkv, xa_wo, norm_ffn_g, router_w, router_b, exp_w_gu, exp_b_gu,
           exp_w_down, exp_b_down, final_norm_g):
    bsz, seq, d = x.shape
    mlen = mem.shape[1]
    n = bsz * seq
    row = lambda v: v.reshape(1, -1).astype(F32)

    p = _prep_mixer_params(min(T_MIX, seq) // SUBLANES, norm_mix_g, w_in, conv_w, conv_b, rg_wa, rg_ba, rg_wx,
                           rg_bx, rg_lambda, s5_lambda_re, s5_lambda_im, s5_b_re, s5_b_im, s5_c_re, s5_c_im,
                           s5_d, s5_log_dt, s5_glu_w, s5_glu_b, w_out)
    h1 = _mixer(x.reshape(n, d), bsz, seq, p)

    k, v = _kv(mem.reshape(bsz * mlen, d), bsz, mlen, row(mem_norm_g), xa_wkv.astype(BF16))
    h2 = _attn(h1, bsz, seq, mlen, row(norm_xa_g), xa_wq.astype(BF16), k, v, xa_wo.astype(BF16))

    xn, idx_t, gate_t, rank_t, cnt = _route(h2, row(norm_ffn_g), router_w.T.astype(F32),
                                            router_b.reshape(-1, 1).astype(F32))
    block_exp, n_used, meta = _slot_tables(idx_t, rank_t, cnt[:, 0], n)
    y2d = _moe(block_exp, n_used, meta, xn, exp_w_gu, exp_b_gu.reshape(N_EXPERTS, 1, -1),
               exp_w_down, exp_b_down.reshape(N_EXPERTS, 1, -1))
    out = _combine(y2d, h2, gate_t.T, row(final_norm_g))
    return out.reshape(bsz, seq, d)
```

```python
import math

import jax
import jax.numpy as jnp
from jax import lax
from jax.experimental import pallas as pl
from jax.experimental.pallas import tpu as pltpu

F32 = jnp.float32
BF16 = jnp.bfloat16

D_MODEL = 1024
D_RG = 512
D_S5 = 512
CONV_W = 4
RG_C = 8.0
S5_GROUPS = 32
S5_STATE = 64
N_STATE = S5_GROUPS * S5_STATE
XA_HEADS = 4
XA_HD = 256
N_EXPERTS = 32
TOP_K = 4
D_FF = 1024
SWIGLU_LIMIT = 7.0
SWIGLU_ALPHA = 1.702
NORM_EPS = 1e-6

SUBLANES = 8
T_MIX = 512
T_ATT = 512
T_TILE = 512
TM_MOE = 256
VMEM_LIMIT = 56 * 1024 * 1024


def _rms(x, g):
    return x * lax.rsqrt(jnp.mean(x * x, axis=-1, keepdims=True) + NORM_EPS) * g


def _gelu_tanh(x):
    return 0.5 * x * (1.0 + jnp.tanh(math.sqrt(2.0 / math.pi) * (x + 0.044715 * (x * x * x))))


def _sigmoid(x):
    return 1.0 / (1.0 + jnp.exp(-x))


def _block_diag(blocks):
    n, r, c = blocks.shape
    eye = jnp.eye(n, dtype=blocks.dtype)
    return jnp.einsum('nrc,nm->nrmc', blocks, eye).reshape(n * r, n * c)


def _sublane_chain(al, e, carry, sub):
    for d in (1, 2, 4):
        al_sh = jnp.where(sub >= d, pltpu.roll(al, d, axis=0), 1.0)
        e_sh = jnp.where(sub >= d, pltpu.roll(e, d, axis=0), 0.0)
        e = al * e_sh + e
        al = al * al_sh
    s_end = e + al * carry
    start = jnp.where(sub == 0, carry, pltpu.roll(s_end, 1, axis=0))
    return start, s_end[SUBLANES - 1:SUBLANES, :]


def _mixer_kernel(x_ref, g_ref, win_ref, convw_ref, convb_ref, wg_ref, bg_ref, sp_ref,
                  bmat_ref, tab_ref, cmat_ref, dskip_ref, glu_ref, glub_ref, wout_ref,
                  o_ref,
                  xin, xout, sem_i, sem_o, xbuf, hist, a_s, b_s, rg_carry, sre, sim, c_re, c_im):
    ngrp = xin.shape[1]
    t = ngrp * SUBLANES
    nh = (CONV_W - 1) * SUBLANES
    c = pl.program_id(1)
    step = pl.program_id(0) * pl.num_programs(1) + c
    nsteps = pl.num_programs(0) * pl.num_programs(1)
    slot = lax.rem(step, 2)

    def in_copy(s, sl, i):
        return pltpu.make_async_copy(x_ref.at[pl.ds(s * t + i * ngrp, ngrp), :], xin.at[sl, :, i, :], sem_i.at[sl])

    def out_copy(s, sl, i):
        return pltpu.make_async_copy(xout.at[sl, :, i, :], o_ref.at[pl.ds(s * t + i * ngrp, ngrp), :], sem_o.at[sl])

    @pl.when(step == 0)
    def _():
        for i in range(SUBLANES):
            in_copy(0, 0, i).start()

    @pl.when(c == 0)
    def _():
        hist[...] = jnp.zeros_like(hist)
        rg_carry[...] = jnp.zeros_like(rg_carry)
        c_re[...] = jnp.zeros_like(c_re)
        c_im[...] = jnp.zeros_like(c_im)

    for i in range(SUBLANES):
        in_copy(step, slot, i).wait()

    @pl.when(step + 1 < nsteps)
    def _():
        for i in range(SUBLANES):
            in_copy(step + 1, 1 - slot, i).start()

    x = xin[slot].reshape(t, D_MODEL)
    xn = _rms(x, g_ref[...]).astype(BF16)
    proj = jnp.dot(xn, win_ref[...], preferred_element_type=F32)
    gate_rg = proj[:, :D_RG]
    x_rg = proj[:, D_RG:2 * D_RG]
    u = proj[:, 2 * D_RG:]

    sub = lax.broadcasted_iota(jnp.int32, (SUBLANES, D_RG), 0)
    cur_tail = x_rg[t - nh:, :]
    for gidx in range(CONV_W - 1):
        rs = slice(gidx * SUBLANES, (gidx + 1) * SUBLANES)
        xbuf[rs, :] = jnp.where(sub == 0, pltpu.roll(hist[rs, :], 1, axis=0), pltpu.roll(cur_tail[rs, :], 1, axis=0))
    hist[...] = cur_tail
    xbuf[nh:, :] = x_rg
    xc = convb_ref[...] + convw_ref[CONV_W - 1:CONV_W, :] * x_rg
    for k in range(CONV_W - 1):
        xc = xc + convw_ref[k:k + 1, :] * xbuf[k * SUBLANES:k * SUBLANES + t, :]
    gates = jnp.dot(xc.astype(BF16), wg_ref[...], preferred_element_type=F32) + bg_ref[...]
    r = _sigmoid(gates[:, :D_RG])
    ig = _sigmoid(gates[:, D_RG:])
    log_a = (-RG_C) * r * sp_ref[...]
    a = jnp.exp(log_a)
    mult = jnp.sqrt(1.0 - a * a)
    row = lax.broadcasted_iota(jnp.int32, (t, D_RG), 0)
    mult = jnp.where((row == 0) & (c == 0), 1.0, mult)
    a_s[...] = a
    b_s[...] = mult * (ig * xc)

    def rg_pass(store):
        def body(q, carry):
            h, ac = carry
            r0 = pl.multiple_of(q * SUBLANES, SUBLANES)
            aa = a_s[pl.ds(r0, SUBLANES), :]
            h = aa * h + b_s[pl.ds(r0, SUBLANES), :]
            if store:
                b_s[pl.ds(r0, SUBLANES), :] = h
                return h, ac
            return h, aa * ac
        return body

    zeros8 = jnp.zeros((SUBLANES, D_RG), F32)
    ones8 = jnp.ones((SUBLANES, D_RG), F32)
    e_loc, a_loc = lax.fori_loop(0, ngrp, rg_pass(False), (zeros8, ones8), unroll=4)
    start, last = _sublane_chain(a_loc, e_loc, rg_carry[...], sub)
    rg_carry[...] = last
    lax.fori_loop(0, ngrp, rg_pass(True), (start, ones8), unroll=4)
    y_rg = b_s[...] * _gelu_tanh(gate_rg)

    ub = u.astype(BF16)
    half = N_STATE // 2
    for hf in range(2):
        bu = jnp.dot(ub[:, hf * 256:(hf + 1) * 256], bmat_ref[hf], preferred_element_type=F32)
        sre[:, hf * half:(hf + 1) * half] = bu[:, :half]
        sim[:, hf * half:(hf + 1) * half] = bu[:, half:]

    lb = 1024
    subl = lax.broadcasted_iota(jnp.int32, (SUBLANES, lb), 0)
    for blk in range(N_STATE // lb):
        cs = slice(blk * lb, (blk + 1) * lb)
        ar = tab_ref[8, :, cs]
        ai = tab_ref[9, :, cs]

        def s5_pass(store, cs=cs, ar=ar, ai=ai):
            def body(q, carry):
                hr, hi = carry
                r0 = pl.multiple_of(q * SUBLANES, SUBLANES)
                nr = ar * hr - ai * hi + sre[pl.ds(r0, SUBLANES), cs]
                ni = ar * hi + ai * hr + sim[pl.ds(r0, SUBLANES), cs]
                if store:
                    sre[pl.ds(r0, SUBLANES), cs] = nr
                    sim[pl.ds(r0, SUBLANES), cs] = ni
                return nr, ni
            return body

        z8 = jnp.zeros((SUBLANES, lb), F32)
        er, ei = lax.fori_loop(0, ngrp, s5_pass(False), (z8, z8), unroll=4)
        for i, d in enumerate((1, 2, 4)):
            mr = tab_ref[2 * i, :, cs]
            mi = tab_ref[2 * i + 1, :, cs]
            rr = pltpu.roll(er, d, axis=0)
            ri = pltpu.roll(ei, d, axis=0)
            er, ei = er + (mr * rr - mi * ri), ei + (mr * ri + mi * rr)
        pr = tab_ref[6, :, cs]
        pi_ = tab_ref[7, :, cs]
        cr = c_re[:, cs]
        ci = c_im[:, cs]
        sr = er + (pr * cr - pi_ * ci)
        si = ei + (pr * ci + pi_ * cr)
        c_re[:, cs] = sr[SUBLANES - 1:SUBLANES, :]
        c_im[:, cs] = si[SUBLANES - 1:SUBLANES, :]
        st_r = jnp.where(subl == 0, cr, pltpu.roll(sr, 1, axis=0))
        st_i = jnp.where(subl == 0, ci, pltpu.roll(si, 1, axis=0))
        lax.fori_loop(0, ngrp, s5_pass(True), (st_r, st_i), unroll=4)

    ys = []
    for hf in range(2):
        hcat = jnp.concatenate([sre[:, hf * half:(hf + 1) * half], sim[:, hf * half:(hf + 1) * half]], axis=1)
        ys.append(jnp.dot(hcat.astype(BF16), cmat_ref[hf], preferred_element_type=F32))
    y = jnp.concatenate(ys, axis=1) + dskip_ref[...] * u
    z = _gelu_tanh(y)
    zg = _sigmoid(jnp.dot(z.astype(BF16), glu_ref[...], preferred_element_type=F32) + glub_ref[...])
    y_s5 = z * zg

    ycat = jnp.concatenate([y_rg, y_s5], axis=1).astype(BF16)
    res = x + jnp.dot(ycat, wout_ref[...], preferred_element_type=F32)

    @pl.when(step >= 2)
    def _():
        for i in range(SUBLANES):
            out_copy(step - 2, slot, i).wait()

    xout[slot] = res.reshape(ngrp, SUBLANES, D_MODEL)
    for i in range(SUBLANES):
        out_copy(step, slot, i).start()

    @pl.when(step == nsteps - 1)
    def _():
        for i in range(SUBLANES):
            out_copy(step, slot, i).wait()

        @pl.when(step >= 1)
        def _():
            for i in range(SUBLANES):
                out_copy(step - 1, 1 - slot, i).wait()


def _mixer(x2, bsz, seq, p):
    t = min(T_MIX, seq)
    nc = seq // t
    nh = (CONV_W - 1) * SUBLANES
    const = lambda shape: pl.BlockSpec(shape, lambda b, c: (0,) * len(shape))
    return pl.pallas_call(
        _mixer_kernel,
        out_shape=jax.ShapeDtypeStruct(x2.shape, F32),
        grid=(bsz, nc),
        in_specs=[
            pl.BlockSpec(memory_space=pl.ANY),
            const((1, D_MODEL)), const((D_MODEL, 3 * D_RG)), const((CONV_W, D_RG)), const((1, D_RG)),
            const((D_RG, 2 * D_RG)), const((1, 2 * D_RG)), const((1, D_RG)),
            const((2, 256, N_STATE)), const((10, SUBLANES, N_STATE)), const((2, N_STATE, 256)),
            const((1, D_S5)), const((D_S5, D_S5)), const((1, D_S5)), const((D_MODEL, D_MODEL)),
        ],
        out_specs=pl.BlockSpec(memory_space=pl.ANY),
        scratch_shapes=[
            pltpu.VMEM((2, t // SUBLANES, SUBLANES, D_MODEL), F32),
            pltpu.VMEM((2, t // SUBLANES, SUBLANES, D_MODEL), F32),
            pltpu.SemaphoreType.DMA((2,)), pltpu.SemaphoreType.DMA((2,)),
            pltpu.VMEM((t + nh, D_RG), F32), pltpu.VMEM((nh, D_RG), F32),
            pltpu.VMEM((t, D_RG), F32), pltpu.VMEM((t, D_RG), F32), pltpu.VMEM((1, D_RG), F32),
            pltpu.VMEM((t, N_STATE), F32), pltpu.VMEM((t, N_STATE), F32),
            pltpu.VMEM((1, N_STATE), F32), pltpu.VMEM((1, N_STATE), F32),
        ],
        compiler_params=pltpu.CompilerParams(
            dimension_semantics=("arbitrary", "arbitrary"), vmem_limit_bytes=VMEM_LIMIT),
        name="mixer",
    )(x2, p["g_mix"], p["w_in"], p["conv_w"], p["conv_b"], p["w_gates"], p["b_gates"], p["sp"],
      p["bmat"], p["s5tab"], p["cmat"], p["d_skip"], p["glu_w"], p["glu_b"], p["w_out"])


def _kv_kernel(m_ref, g_ref, w_ref, k_ref, v_ref):
    mn = _rms(m_ref[...], g_ref[...]).astype(BF16)
    kv = jnp.dot(mn, w_ref[...], preferred_element_type=F32)
    k_ref[...] = kv[:, :D_MODEL].astype(BF16)
    v_ref[...] = kv[:, D_MODEL:].astype(BF16)


def _kv(mem2, bsz, mlen, g, wkv):
    return pl.pallas_call(
        _kv_kernel,
        out_shape=(jax.ShapeDtypeStruct(mem2.shape, BF16), jax.ShapeDtypeStruct(mem2.shape, BF16)),
        grid=(bsz,),
        in_specs=[pl.BlockSpec((mlen, D_MODEL), lambda b: (b, 0)),
                  pl.BlockSpec((1, D_MODEL), lambda b: (0, 0)),
                  pl.BlockSpec((D_MODEL, 2 * D_MODEL), lambda b: (0, 0))],
        out_specs=(pl.BlockSpec((mlen, D_MODEL), lambda b: (b, 0)),
                   pl.BlockSpec((mlen, D_MODEL), lambda b: (b, 0))),
        compiler_params=pltpu.CompilerParams(dimension_semantics=("arbitrary",), vmem_limit_bytes=VMEM_LIMIT),
        name="kv",
    )(mem2, g, wkv)


def _attn_kernel(h_ref, g_ref, wq_ref, k_ref, v_ref, wo_ref, o_ref):
    h = h_ref[...]
    xn = _rms(h, g_ref[...]).astype(BF16)
    q = jnp.dot(xn, wq_ref[...], preferred_element_type=F32).astype(BF16)
    outs = []
    for hd in range(XA_HEADS):
        cs = slice(hd * XA_HD, (hd + 1) * XA_HD)
        s = lax.dot_general(q[:, cs], k_ref[:, cs], (((1,), (1,)), ((), ())),
                            preferred_element_type=F32) * (XA_HD ** -0.5)
        s = s - jnp.max(s, axis=-1, keepdims=True)
        e = jnp.exp(s)
        pr = e / jnp.sum(e, axis=-1, keepdims=True)
        outs.append(jnp.dot(pr.astype(BF16), v_ref[:, cs], preferred_element_type=F32))
    o = jnp.concatenate(outs, axis=1).astype(BF16)
    o_ref[...] = h + jnp.dot(o, wo_ref[...], preferred_element_type=F32)


def _attn(h2, bsz, seq, mlen, g, wq, k, v, wo):
    t = min(T_ATT, seq)
    nc = seq // t
    return pl.pallas_call(
        _attn_kernel,
        out_shape=jax.ShapeDtypeStruct(h2.shape, F32),
        grid=(bsz, nc),
        in_specs=[pl.BlockSpec((t, D_MODEL), lambda b, c: (b * nc + c, 0)),
                  pl.BlockSpec((1, D_MODEL), lambda b, c: (0, 0)),
                  pl.BlockSpec((D_MODEL, D_MODEL), lambda b, c: (0, 0)),
                  pl.BlockSpec((mlen, D_MODEL), lambda b, c: (b, 0)),
                  pl.BlockSpec((mlen, D_MODEL), lambda b, c: (b, 0)),
                  pl.BlockSpec((D_MODEL, D_MODEL), lambda b, c: (0, 0))],
        out_specs=pl.BlockSpec((t, D_MODEL), lambda b, c: (b * nc + c, 0)),
        compiler_params=pltpu.CompilerParams(
            dimension_semantics=("arbitrary", "arbitrary"), vmem_limit_bytes=VMEM_LIMIT),
        name="attn",
    )(h2, g, wq, k, v, wo)


def _route_kernel(h_ref, g_ref, wr_ref, br_ref, xn_ref, lpos_ref, gate_ref, tcnt_ref):
    t = h_ref.shape[0]
    xn = _rms(h_ref[...], g_ref[...])
    xn_ref[...] = xn.astype(BF16)
    lg = lax.dot_general(wr_ref[...], xn, (((1,), (1,)), ((), ())),
                         precision=lax.Precision.HIGHEST, preferred_element_type=F32) + br_ref[...]
    erow = lax.broadcasted_iota(jnp.int32, (N_EXPERTS, t), 0)
    vals, hots = [], []
    for _ in range(TOP_K):
        m = jnp.max(lg, axis=0, keepdims=True)
        ix = jnp.min(jnp.where(lg == m, erow, N_EXPERTS), axis=0, keepdims=True)
        hot = erow == ix
        lg = jnp.where(hot, -jnp.inf, lg)
        vals.append(m)
        hots.append(hot)
    es = [jnp.exp(v - vals[0]) for v in vals]
    den = es[0] + es[1] + es[2] + es[3]
    sel = jnp.zeros((N_EXPERTS, t), F32)
    for hot in hots:
        sel = sel + jnp.where(hot, 1.0, 0.0)
    tri = jnp.where(lax.broadcasted_iota(jnp.int32, (t, t), 0) < lax.broadcasted_iota(jnp.int32, (t, t), 1),
                    1.0, 0.0).astype(BF16)
    rank = jnp.dot(sel.astype(BF16), tri, preferred_element_type=F32)
    cnt = jnp.sum(sel, axis=1, keepdims=True)
    run8 = jnp.floor((cnt + (SUBLANES - 1.0)) * (1.0 / SUBLANES))
    below = jnp.where(lax.broadcasted_iota(jnp.int32, (N_EXPERTS, N_EXPERTS), 1)
                      < lax.broadcasted_iota(jnp.int32, (N_EXPERTS, N_EXPERTS), 0), 1.0, 0.0).astype(BF16)
    start8 = jnp.dot(below, jnp.broadcast_to(run8, (N_EXPERTS, 128)).astype(BF16), preferred_element_type=F32)
    pos = rank + SUBLANES * start8[:, 0:1]
    for k in range(TOP_K):
        gate_ref[k:k + 1, :] = es[k] / den
        lpos_ref[k:k + 1, :] = jnp.sum(jnp.where(hots[k], pos, 0.0), axis=0, keepdims=True).astype(jnp.int32)
    tcnt_ref[0] = jnp.broadcast_to(cnt, (N_EXPERTS, 128)).astype(jnp.int32)


def _route(h2, g, wr_t, br):
    n = h2.shape[0]
    t = min(T_TILE, n)
    return pl.pallas_call(
        _route_kernel,
        out_shape=(jax.ShapeDtypeStruct((n, D_MODEL), BF16),
                   jax.ShapeDtypeStruct((TOP_K, n), jnp.int32),
                   jax.ShapeDtypeStruct((TOP_K, n), F32),
                   jax.ShapeDtypeStruct((n // t, N_EXPERTS, 128), jnp.int32)),
        grid=(n // t,),
        in_specs=[pl.BlockSpec((t, D_MODEL), lambda i: (i, 0)),
                  pl.BlockSpec((1, D_MODEL), lambda i: (0, 0)),
                  pl.BlockSpec((N_EXPERTS, D_MODEL), lambda i: (0, 0)),
                  pl.BlockSpec((N_EXPERTS, 1), lambda i: (0, 0))],
        out_specs=(pl.BlockSpec((t, D_MODEL), lambda i: (i, 0)),
                   pl.BlockSpec((TOP_K, t), lambda i: (0, i)),
                   pl.BlockSpec((TOP_K, t), lambda i: (0, i)),
                   pl.BlockSpec((1, N_EXPERTS, 128), lambda i: (i, 0, 0))),
        compiler_params=pltpu.CompilerParams(dimension_semantics=("arbitrary",), vmem_limit_bytes=VMEM_LIMIT),
        name="route",
    )(h2, g, wr_t, br)


def _run_copies(i, lo_ref, go_ref, n8_ref, make_copy):
    def expert(e, started):
        idx = i * N_EXPERTS + e
        lo = lo_ref[idx]
        go = go_ref[idx]

        def chunk(j, carry):
            make_copy(pl.multiple_of(lo + j * SUBLANES, SUBLANES), pl.multiple_of(go + j * SUBLANES, SUBLANES)).start()
            return carry

        lax.fori_loop(0, n8_ref[idx], chunk, 0)
        return started + n8_ref[idx]

    return lax.fori_loop(0, N_EXPERTS, expert, 0)


def _dispatch_kernel(lo_ref, go_ref, n8_ref, zs_ref, zn_ref, xn_ref, lpos_ref, xs_hbm, loc, zbuf, sem, sem_z):
    i = pl.program_id(0)
    nloc, t = loc.shape[0], xn_ref.shape[0]
    liota = lax.broadcasted_iota(jnp.int32, (nloc, t), 0)
    pick = jnp.zeros((nloc, t), F32)
    for k in range(TOP_K):
        pick = pick + jnp.where(liota == lpos_ref[k:k + 1, :], 1.0, 0.0)
    loc[...] = jnp.dot(pick.astype(BF16), xn_ref[...], preferred_element_type=F32)

    def row_copy(lo, go):
        return pltpu.make_async_copy(loc.at[pl.ds(lo, SUBLANES), :], xs_hbm.at[pl.ds(go, SUBLANES), :], sem)

    started = _run_copies(i, lo_ref, go_ref, n8_ref, row_copy)

    def wait_one(j, carry):
        row_copy(0, 0).wait()
        return carry

    lax.fori_loop(0, started, wait_one, 0)

    @pl.when(i == pl.num_programs(0) - 1)
    def _():
        zbuf[...] = jnp.zeros_like(zbuf)

        def zero_copy(go):
            return pltpu.make_async_copy(zbuf.at[pl.ds(0, SUBLANES), :], xs_hbm.at[pl.ds(go, SUBLANES), :], sem_z)

        def expert(e, total):
            def chunk(j, carry):
                zero_copy(pl.multiple_of(zs_ref[e] + j * SUBLANES, SUBLANES)).start()
                return carry
            lax.fori_loop(0, zn_ref[e], chunk, 0)
            return total + zn_ref[e]

        total = lax.fori_loop(0, N_EXPERTS, expert, 0)

        def wait_zero(j, carry):
            zero_copy(0).wait()
            return carry

        lax.fori_loop(0, total, wait_zero, 0)

        def block_copy(go):
            return pltpu.make_async_copy(zbuf, xs_hbm.at[pl.ds(go, TM_MOE), :], sem_z)

        def start_block(j, carry):
            block_copy(pl.multiple_of(zs_ref[N_EXPERTS] + j * TM_MOE, TM_MOE)).start()
            return carry

        def wait_block(j, carry):
            block_copy(0).wait()
            return carry

        lax.fori_loop(0, zn_ref[N_EXPERTS], start_block, 0)
        lax.fori_loop(0, zn_ref[N_EXPERTS], wait_block, 0)


def _dispatch(tabs, xn, lpos_t, n_slots, nloc):
    n = xn.shape[0]
    t = min(T_TILE, n)
    grid_spec = pltpu.PrefetchScalarGridSpec(
        num_scalar_prefetch=5,
        grid=(n // t,),
        in_specs=[pl.BlockSpec((t, D_MODEL), lambda i, *_: (i, 0)),
                  pl.BlockSpec((TOP_K, t), lambda i, *_: (0, i))],
        out_specs=pl.BlockSpec(memory_space=pl.ANY),
        scratch_shapes=[pltpu.VMEM((nloc, D_MODEL), F32), pltpu.VMEM((TM_MOE, D_MODEL), F32),
                        pltpu.SemaphoreType.DMA, pltpu.SemaphoreType.DMA],
    )
    return pl.pallas_call(
        _dispatch_kernel,
        out_shape=jax.ShapeDtypeStruct((n_slots, D_MODEL), F32),
        grid_spec=grid_spec,
        compiler_params=pltpu.CompilerParams(dimension_semantics=("arbitrary",), vmem_limit_bytes=VMEM_LIMIT),
        name="dispatch",
    )(tabs["lo"], tabs["go"], tabs["n8"], tabs["zs"], tabs["zn"], xn, lpos_t)


def _moe_kernel(bexp_ref, nused_ref, x_ref, wgu_ref, bgu_ref, wd_ref, bd_ref, o_ref, wgu_s, wd_s):
    b = pl.program_id(0)
    prev = bexp_ref[jnp.maximum(b - 1, 0)]
    fresh = (b == 0) | (bexp_ref[b] != prev)

    @pl.when(fresh & (b < nused_ref[0]))
    def _():
        rows = 128
        for r0 in range(0, D_MODEL, rows):
            wgu_s[r0:r0 + rows, :] = wgu_ref[0, r0:r0 + rows, :].astype(BF16)
        for r0 in range(0, D_FF, rows):
            wd_s[r0:r0 + rows, :] = wd_ref[0, r0:r0 + rows, :].astype(BF16)

    @pl.when(b < nused_ref[0])
    def _():
        xb = x_ref[...].astype(BF16)
        gu = jnp.dot(xb, wgu_s[...], preferred_element_type=F32) + bgu_ref[0]
        g = jnp.minimum(gu[:, :D_FF], SWIGLU_LIMIT)
        u = jnp.clip(gu[:, D_FF:], -SWIGLU_LIMIT, SWIGLU_LIMIT)
        h = (u + 1.0) * (g * _sigmoid(SWIGLU_ALPHA * g))
        o_ref[...] = jnp.dot(h.astype(BF16), wd_s[...], preferred_element_type=F32) + bd_ref[0]

    @pl.when(b >= nused_ref[0])
    def _():
        o_ref[...] = jnp.zeros_like(o_ref)


def _moe(block_exp, n_used, xs, w_gu, b_gu, w_down, b_down):
    n_slots = xs.shape[0]
    nb = n_slots // TM_MOE
    wspec = lambda shape: pl.BlockSpec(shape, lambda b, be, nu: (be[b], 0, 0))
    grid_spec = pltpu.PrefetchScalarGridSpec(
        num_scalar_prefetch=2,
        grid=(nb,),
        in_specs=[
            pl.BlockSpec((TM_MOE, D_MODEL), lambda b, be, nu: (jnp.minimum(b, nu[0] - 1), 0)),
            wspec((1, D_MODEL, 2 * D_FF)), wspec((1, 1, 2 * D_FF)),
            wspec((1, D_FF, D_MODEL)), wspec((1, 1, D_MODEL)),
        ],
        out_specs=pl.BlockSpec((TM_MOE, D_MODEL), lambda b, be, nu: (b, 0)),
        scratch_shapes=[pltpu.VMEM((D_MODEL, 2 * D_FF), BF16), pltpu.VMEM((D_FF, D_MODEL), BF16)],
    )
    return pl.pallas_call(
        _moe_kernel,
        out_shape=jax.ShapeDtypeStruct((n_slots, D_MODEL), F32),
        grid_spec=grid_spec,
        compiler_params=pltpu.CompilerParams(dimension_semantics=("arbitrary",), vmem_limit_bytes=VMEM_LIMIT),
        name="moe",
    )(block_exp, n_used, xs, w_gu, b_gu, w_down, b_down)


def _combine_kernel(lo_ref, go_ref, n8_ref, ltot_ref, ys_hbm, h_ref, lpos_ref, gate_ref, g_ref, o_ref, loc, sem):
    i = pl.program_id(0)
    nloc, t = loc.shape[0], h_ref.shape[0]

    def row_copy(lo, go):
        return pltpu.make_async_copy(ys_hbm.at[pl.ds(go, SUBLANES), :], loc.at[pl.ds(lo, SUBLANES), :], sem)

    started = _run_copies(i, lo_ref, go_ref, n8_ref, row_copy)

    def zero_rows(j, carry):
        r0 = pl.multiple_of(ltot_ref[i] + j * SUBLANES, SUBLANES)
        loc[pl.ds(r0, SUBLANES), :] = jnp.zeros((SUBLANES, D_MODEL), F32)
        return carry

    lax.fori_loop(0, (nloc - ltot_ref[i]) // SUBLANES, zero_rows, 0)

    liota = lax.broadcasted_iota(jnp.int32, (t, nloc), 1)
    lpos = lpos_ref[...]
    gates = gate_ref[...]
    wsel = jnp.zeros((t, nloc), F32)
    for k in range(TOP_K):
        wsel = wsel + jnp.where(liota == lpos[:, k:k + 1], gates[:, k:k + 1], 0.0)

    def wait_one(j, carry):
        row_copy(0, 0).wait()
        return carry

    lax.fori_loop(0, started, wait_one, 0)
    acc = h_ref[...] + jnp.dot(wsel.astype(BF16), loc[...].astype(BF16), preferred_element_type=F32)
    o_ref[...] = _rms(acc, g_ref[...])


def _combine(tabs, ys, h2, lpos_tok, gates_tok, g, nloc):
    n = h2.shape[0]
    t = min(T_TILE, n)
    grid_spec = pltpu.PrefetchScalarGridSpec(
        num_scalar_prefetch=4,
        grid=(n // t,),
        in_specs=[pl.BlockSpec(memory_space=pl.ANY),
                  pl.BlockSpec((t, D_MODEL), lambda i, *_: (i, 0)),
                  pl.BlockSpec((t, TOP_K), lambda i, *_: (i, 0)),
                  pl.BlockSpec((t, TOP_K), lambda i, *_: (i, 0)),
                  pl.BlockSpec((1, D_MODEL), lambda i, *_: (0, 0))],
        out_specs=pl.BlockSpec((t, D_MODEL), lambda i, *_: (i, 0)),
        scratch_shapes=[pltpu.VMEM((nloc, D_MODEL), F32), pltpu.SemaphoreType.DMA],
    )
    return pl.pallas_call(
        _combine_kernel,
        out_shape=jax.ShapeDtypeStruct((n, D_MODEL), F32),
        grid_spec=grid_spec,
        compiler_params=pltpu.CompilerParams(dimension_semantics=("arbitrary",), vmem_limit_bytes=VMEM_LIMIT),
        name="combine",
    )(tabs["lo"], tabs["go"], tabs["n8"], tabs["ltot"], ys, h2, lpos_tok, gates_tok, g)


def _cmul(a, b):
    return a[0] * b[0] - a[1] * b[1], a[0] * b[1] + a[1] * b[0]


def _prep_mixer_params(sub_len, norm_mix_g, w_in, conv_w, conv_b, rg_wa, rg_ba, rg_wx, rg_bx, rg_lambda,
                       s5_lambda_re, s5_lambda_im, s5_b_re, s5_b_im, s5_c_re, s5_c_im, s5_d, s5_log_dt,
                       s5_glu_w, s5_glu_b, w_out):
    row = lambda v: v.reshape(1, -1).astype(F32)
    dt = jnp.exp(s5_log_dt)[:, None]
    mag = jnp.exp(dt * s5_lambda_re)
    abar_re = mag * jnp.cos(dt * s5_lambda_im)
    abar_im = mag * jnp.sin(dt * s5_lambda_im)
    den = s5_lambda_re * s5_lambda_re + s5_lambda_im * s5_lambda_im
    num_re = abar_re - 1.0
    coef_re = (num_re * s5_lambda_re + abar_im * s5_lambda_im) / den
    coef_im = (abar_im * s5_lambda_re - num_re * s5_lambda_im) / den
    bbar_re = coef_re[..., None] * s5_b_re - coef_im[..., None] * s5_b_im
    bbar_im = coef_re[..., None] * s5_b_im + coef_im[..., None] * s5_b_re
    a1 = (abar_re.reshape(-1), abar_im.reshape(-1))
    aq = a1
    for _ in range(int(math.log2(sub_len))):
        aq = _cmul(aq, aq)
    pw = [aq]
    for _ in range(SUBLANES - 1):
        pw.append(_cmul(pw[-1], aq))
    sub = jnp.arange(SUBLANES)[:, None]
    tabs = []
    for d in (1, 2, 4):
        keep = (sub >= d).astype(F32)
        tabs += [keep * pw[d - 1][0][None, :], keep * pw[d - 1][1][None, :]]
    tabs += [jnp.stack([p_[0] for p_ in pw]), jnp.stack([p_[1] for p_ in pw])]
    tabs += [jnp.broadcast_to(a1[0][None, :], (SUBLANES, N_STATE)),
             jnp.broadcast_to(a1[1][None, :], (SUBLANES, N_STATE))]
    s5tab = jnp.stack(tabs).astype(F32)
    gh = S5_GROUPS // 2
    bmat, cmat = [], []
    for hf in range(2):
        gs = slice(hf * gh, (hf + 1) * gh)
        b_re = _block_diag(jnp.transpose(bbar_re[gs], (0, 2, 1)))
        b_im = _block_diag(jnp.transpose(bbar_im[gs], (0, 2, 1)))
        bmat.append(jnp.concatenate([b_re, b_im], axis=1))
        cr = _block_diag(jnp.transpose(s5_c_re[gs], (0, 2, 1)))
        ci = _block_diag(jnp.transpose(s5_c_im[gs], (0, 2, 1)))
        cmat.append(jnp.concatenate([cr, -ci], axis=0))
    return {
        "g_mix": row(norm_mix_g), "w_in": w_in.astype(BF16), "conv_w": conv_w.astype(F32), "conv_b": row(conv_b),
        "w_gates": jnp.concatenate([_block_diag(rg_wa), _block_diag(rg_wx)], axis=1).astype(BF16),
        "b_gates": jnp.concatenate([rg_ba, rg_bx]).reshape(1, -1).astype(F32),
        "sp": row(jax.nn.softplus(-rg_lambda)),
        "bmat": jnp.stack(bmat).astype(BF16), "s5tab": s5tab, "cmat": jnp.stack(cmat).astype(BF16),
        "d_skip": row(s5_d), "glu_w": _block_diag(s5_glu_w).astype(BF16), "glu_b": row(s5_glu_b),
        "w_out": w_out.astype(BF16),
    }


def _slot_tables(tcnt, n_slots):
    i32 = lambda v: v.astype(jnp.int32)
    run = ((tcnt + SUBLANES - 1) // SUBLANES) * SUBLANES
    rows = jnp.sum(run, axis=0)
    blocks = (rows + TM_MOE - 1) // TM_MOE
    bend = jnp.cumsum(blocks)
    estart = (bend - blocks) * TM_MOE
    nb = n_slots // TM_MOE
    n_used = i32(bend[-1])
    block_exp = jnp.minimum(jnp.sum(jnp.arange(nb)[:, None] >= bend[None, :], axis=1), N_EXPERTS - 1)
    block_exp = jnp.where(jnp.arange(nb) < n_used, block_exp, block_exp[jnp.maximum(n_used - 1, 0)])
    tabs = {
        "lo": i32(jnp.cumsum(run, axis=1) - run).reshape(-1),
        "go": i32(estart[None, :] + jnp.cumsum(run, axis=0) - run).reshape(-1),
        "n8": i32(run // SUBLANES).reshape(-1),
        "ltot": i32(jnp.sum(run, axis=1)),
        "zs": i32(jnp.concatenate([estart + rows, bend[-1:] * TM_MOE])),
        "zn": i32(jnp.concatenate([(blocks * TM_MOE - rows) // SUBLANES, nb - bend[-1:]])),
    }
    return i32(block_exp), n_used.reshape(1), tabs


def kernel(x, mem, norm_mix_g, w_in, conv_w, conv_b, rg_wa, rg_ba, rg_wx, rg_bx, rg_lambda, s5_lambda_re,
           s5_lambda_im, s5_b_re, s5_b_im, s5_c_re, s5_c_im, s5_d, s5_log_dt, s5_glu_w, s5_glu_b, w_out,
           norm_xa_g, mem_norm_g, xa_wq, xa_wkv, xa_wo, norm_ffn_g, router_w, router_b, exp_w_gu, exp_b_gu,
           exp_w_down, exp_b_down, final_norm_g):
    bsz, seq, d = x.shape
    mlen = mem.shape[1]
    n = bsz * seq
    row = lambda v: v.reshape(1, -1).astype(F32)

    p = _prep_mixer_params(min(T_MIX, seq) // SUBLANES, norm_mix_g, w_in, conv_w, conv_b, rg_wa, rg_ba, rg_wx,
                           rg_bx, rg_lambda, s5_lambda_re, s5_lambda_im, s5_b_re, s5_b_im, s5_c_re, s5_c_im,
                           s5_d, s5_log_dt, s5_glu_w, s5_glu_b, w_out)
    h1 = _mixer(x.reshape(n, d), bsz, seq, p)

    k, v = _kv(mem.reshape(bsz * mlen, d), bsz, mlen, row(mem_norm_g), xa_wkv.astype(BF16))
    h2 = _attn(h1, bsz, seq, mlen, row(norm_xa_g), xa_wq.astype(BF16), k, v, xa_wo.astype(BF16))

    xn, lpos_t, gate_t, tcnt = _route(h2, row(norm_ffn_g), router_w.T.astype(F32),
                                      router_b.reshape(-1, 1).astype(F32))
    tile = min(T_TILE, n)
    n_tiles = n // tile
    run_pad = N_EXPERTS * (SUBLANES - 1)
    nloc = -(-(TOP_K * tile + run_pad) // 128) * 128
    n_slots = -(-(TOP_K * n + n_tiles * run_pad + N_EXPERTS * (TM_MOE - SUBLANES)) // TM_MOE) * TM_MOE
    block_exp, n_used, tabs = _slot_tables(tcnt[:, :, 0], n_slots)
    xs = _dispatch(tabs, xn, lpos_t, n_slots, nloc)
    ys = _moe(block_exp, n_used, xs, exp_w_gu, exp_b_gu.reshape(N_EXPERTS, 1, -1),
              exp_w_down, exp_b_down.reshape(N_EXPERTS, 1, -1))
    out = _combine(tabs, ys, h2, lpos_t.T, gate_t.T, row(final_norm_g), nloc)
    return out.reshape(bsz, seq, d)
```

```python
import math

import jax
import jax.numpy as jnp
from jax import lax
from jax.experimental import pallas as pl
from jax.experimental.pallas import tpu as pltpu

F32 = jnp.float32
BF16 = jnp.bfloat16

D_MODEL = 1024
D_RG = 512
D_S5 = 512
CONV_W = 4
RG_C = 8.0
S5_GROUPS = 32
S5_STATE = 64
N_STATE = S5_GROUPS * S5_STATE
XA_HEADS = 4
XA_HD = 256
N_EXPERTS = 32
TOP_K = 4
D_FF = 1024
SWIGLU_LIMIT = 7.0
SWIGLU_ALPHA = 1.702
NORM_EPS = 1e-6

SUBLANES = 8
T_MIX = 512
T_ATT = 512
T_TILE = 512
TM_MOE = 256
VMEM_LIMIT = 56 * 1024 * 1024


def _rms(x, g):
    return x * lax.rsqrt(jnp.mean(x * x, axis=-1, keepdims=True) + NORM_EPS) * g


def _gelu_tanh(x):
    return 0.5 * x * (1.0 + jnp.tanh(math.sqrt(2.0 / math.pi) * (x + 0.044715 * (x * x * x))))


def _sigmoid(x):
    return 1.0 / (1.0 + jnp.exp(-x))


def _block_diag(blocks):
    n, r, c = blocks.shape
    eye = jnp.eye(n, dtype=blocks.dtype)
    return jnp.einsum('nrc,nm->nrmc', blocks, eye).reshape(n * r, n * c)


def _sublane_chain(al, e, carry, sub):
    for d in (1, 2, 4):
        al_sh = jnp.where(sub >= d, pltpu.roll(al, d, axis=0), 1.0)
        e_sh = jnp.where(sub >= d, pltpu.roll(e, d, axis=0), 0.0)
        e = al * e_sh + e
        al = al * al_sh
    s_end = e + al * carry
    start = jnp.where(sub == 0, carry, pltpu.roll(s_end, 1, axis=0))
    return start, s_end[SUBLANES - 1:SUBLANES, :]


def _mixer_kernel(x_ref, g_ref, win_ref, convw_ref, convb_ref, wg_ref, bg_ref, sp_ref,
                  bmat_ref, tab_ref, cmat_ref, dskip_ref, glu_ref, glub_ref, wout_ref,
                  o_ref,
                  xin, xout, sem_i, sem_o, xbuf, hist, a_s, b_s, rg_carry, sre, sim, c_re, c_im):
    ngrp = xin.shape[1]
    t = ngrp * SUBLANES
    nh = (CONV_W - 1) * SUBLANES
    c = pl.program_id(1)
    step = pl.program_id(0) * pl.num_programs(1) + c
    nsteps = pl.num_programs(0) * pl.num_programs(1)
    slot = lax.rem(step, 2)

    def in_copy(s, sl, i):
        return pltpu.make_async_copy(x_ref.at[pl.ds(s * t + i * ngrp, ngrp), :], xin.at[sl, :, i, :], sem_i.at[sl])

    def out_copy(s, sl, i):
        return pltpu.make_async_copy(xout.at[sl, :, i, :], o_ref.at[pl.ds(s * t + i * ngrp, ngrp), :], sem_o.at[sl])

    @pl.when(step == 0)
    def _():
        for i in range(SUBLANES):
            in_copy(0, 0, i).start()

    @pl.when(c == 0)
    def _():
        hist[...] = jnp.zeros_like(hist)
        rg_carry[...] = jnp.zeros_like(rg_carry)
        c_re[...] = jnp.zeros_like(c_re)
        c_im[...] = jnp.zeros_like(c_im)

    for i in range(SUBLANES):
        in_copy(step, slot, i).wait()

    @pl.when(step + 1 < nsteps)
    def _():
        for i in range(SUBLANES):
            in_copy(step + 1, 1 - slot, i).start()

    x = xin[slot].reshape(t, D_MODEL)
    xn = _rms(x, g_ref[...]).astype(BF16)
    proj = jnp.dot(xn, win_ref[...], preferred_element_type=F32)
    gate_rg = proj[:, :D_RG]
    x_rg = proj[:, D_RG:2 * D_RG]
    u = proj[:, 2 * D_RG:]

    sub = lax.broadcasted_iota(jnp.int32, (SUBLANES, D_RG), 0)
    cur_tail = x_rg[t - nh:, :]
    for gidx in range(CONV_W - 1):
        rs = slice(gidx * SUBLANES, (gidx + 1) * SUBLANES)
        xbuf[rs, :] = jnp.where(sub == 0, pltpu.roll(hist[rs, :], 1, axis=0), pltpu.roll(cur_tail[rs, :], 1, axis=0))
    hist[...] = cur_tail
    xbuf[nh:, :] = x_rg
    xc = convb_ref[...] + convw_ref[CONV_W - 1:CONV_W, :] * x_rg
    for k in range(CONV_W - 1):
        xc = xc + convw_ref[k:k + 1, :] * xbuf[k * SUBLANES:k * SUBLANES + t, :]
    gates = jnp.dot(xc.astype(BF16), wg_ref[...], preferred_element_type=F32) + bg_ref[...]
    r = _sigmoid(gates[:, :D_RG])
    ig = _sigmoid(gates[:, D_RG:])
    log_a = (-RG_C) * r * sp_ref[...]
    a = jnp.exp(log_a)
    mult = jnp.sqrt(1.0 - a * a)
    row = lax.broadcasted_iota(jnp.int32, (t, D_RG), 0)
    mult = jnp.where((row == 0) & (c == 0), 1.0, mult)
    a_s[...] = a
    b_s[...] = mult * (ig * xc)

    def rg_pass(store):
        def body(q, carry):
            h, ac = carry
            r0 = pl.multiple_of(q * SUBLANES, SUBLANES)
            aa = a_s[pl.ds(r0, SUBLANES), :]
            h = aa * h + b_s[pl.ds(r0, SUBLANES), :]
            if store:
                b_s[pl.ds(r0, SUBLANES), :] = h
                return h, ac
            return h, aa * ac
        return body

    zeros8 = jnp.zeros((SUBLANES, D_RG), F32)
    ones8 = jnp.ones((SUBLANES, D_RG), F32)
    e_loc, a_loc = lax.fori_loop(0, ngrp, rg_pass(False), (zeros8, ones8), unroll=4)
    start, last = _sublane_chain(a_loc, e_loc, rg_carry[...], sub)
    rg_carry[...] = last
    lax.fori_loop(0, ngrp, rg_pass(True), (start, ones8), unroll=4)
    y_rg = b_s[...] * _gelu_tanh(gate_rg)

    ub = u.astype(BF16)
    half = N_STATE // 2
    for hf in range(2):
        bu = jnp.dot(ub[:, hf * 256:(hf + 1) * 256], bmat_ref[hf], preferred_element_type=F32)
        sre[:, hf * half:(hf + 1) * half] = bu[:, :half]
        sim[:, hf * half:(hf + 1) * half] = bu[:, half:]

    lb = 1024
    subl = lax.broadcasted_iota(jnp.int32, (SUBLANES, lb), 0)
    for blk in range(N_STATE // lb):
        cs = slice(blk * lb, (blk + 1) * lb)
        ar = tab_ref[8, :, cs]
        ai = tab_ref[9, :, cs]

        def s5_pass(store, cs=cs, ar=ar, ai=ai):
            def body(q, carry):
                hr, hi = carry
                r0 = pl.multiple_of(q * SUBLANES, SUBLANES)
                nr = ar * hr - ai * hi + sre[pl.ds(r0, SUBLANES), cs]
                ni = ar * hi + ai * hr + sim[pl.ds(r0, SUBLANES), cs]
                if store:
                    sre[pl.ds(r0, SUBLANES), cs] = nr
                    sim[pl.ds(r0, SUBLANES), cs] = ni
                return nr, ni
            return body

        z8 = jnp.zeros((SUBLANES, lb), F32)
        er, ei = lax.fori_loop(0, ngrp, s5_pass(False), (z8, z8), unroll=4)
        for i, d in enumerate((1, 2, 4)):
            mr = tab_ref[2 * i, :, cs]
            mi = tab_ref[2 * i + 1, :, cs]
            rr = pltpu.roll(er, d, axis=0)
            ri = pltpu.roll(ei, d, axis=0)
            er, ei = er + (mr * rr - mi * ri), ei + (mr * ri + mi * rr)
        pr = tab_ref[6, :, cs]
        pi_ = tab_ref[7, :, cs]
        cr = c_re[:, cs]
        ci = c_im[:, cs]
        sr = er + (pr * cr - pi_ * ci)
        si = ei + (pr * ci + pi_ * cr)
        c_re[:, cs] = sr[SUBLANES - 1:SUBLANES, :]
        c_im[:, cs] = si[SUBLANES - 1:SUBLANES, :]
        st_r = jnp.where(subl == 0, cr, pltpu.roll(sr, 1, axis=0))
        st_i = jnp.where(subl == 0, ci, pltpu.roll(si, 1, axis=0))
        lax.fori_loop(0, ngrp, s5_pass(True), (st_r, st_i), unroll=4)

    ys = []
    for hf in range(2):
        hcat = jnp.concatenate([sre[:, hf * half:(hf + 1) * half], sim[:, hf * half:(hf + 1) * half]], axis=1)
        ys.append(jnp.dot(hcat.astype(BF16), cmat_ref[hf], preferred_element_type=F32))
    y = jnp.concatenate(ys, axis=1) + dskip_ref[...] * u
    z = _gelu_tanh(y)
    zg = _sigmoid(jnp.dot(z.astype(BF16), glu_ref[...], preferred_element_type=F32) + glub_ref[...])
    y_s5 = z * zg

    ycat = jnp.concatenate([y_rg, y_s5], axis=1).astype(BF16)
    res = x + jnp.dot(ycat, wout_ref[...], preferred_element_type=F32)

    @pl.when(step >= 2)
    def _():
        for i in range(SUBLANES):
            out_copy(step - 2, slot, i).wait()

    xout[slot] = res.reshape(ngrp, SUBLANES, D_MODEL)
    for i in range(SUBLANES):
        out_copy(step, slot, i).start()

    @pl.when(step == nsteps - 1)
    def _():
        for i in range(SUBLANES):
            out_copy(step, slot, i).wait()

        @pl.when(step >= 1)
        def _():
            for i in range(SUBLANES):
                out_copy(step - 1, 1 - slot, i).wait()


def _mixer(x2, bsz, seq, p):
    t = min(T_MIX, seq)
    nc = seq // t
    nh = (CONV_W - 1) * SUBLANES
    const = lambda shape: pl.BlockSpec(shape, lambda b, c: (0,) * len(shape))
    return pl.pallas_call(
        _mixer_kernel,
        out_shape=jax.ShapeDtypeStruct(x2.shape, F32),
        grid=(bsz, nc),
        in_specs=[
            pl.BlockSpec(memory_space=pl.ANY),
            const((1, D_MODEL)), const((D_MODEL, 3 * D_RG)), const((CONV_W, D_RG)), const((1, D_RG)),
            const((D_RG, 2 * D_RG)), const((1, 2 * D_RG)), const((1, D_RG)),
            const((2, 256, N_STATE)), const((10, SUBLANES, N_STATE)), const((2, N_STATE, 256)),
            const((1, D_S5)), const((D_S5, D_S5)), const((1, D_S5)), const((D_MODEL, D_MODEL)),
        ],
        out_specs=pl.BlockSpec(memory_space=pl.ANY),
        scratch_shapes=[
            pltpu.VMEM((2, t // SUBLANES, SUBLANES, D_MODEL), F32),
            pltpu.VMEM((2, t // SUBLANES, SUBLANES, D_MODEL), F32),
            pltpu.SemaphoreType.DMA((2,)), pltpu.SemaphoreType.DMA((2,)),
            pltpu.VMEM((t + nh, D_RG), F32), pltpu.VMEM((nh, D_RG), F32),
            pltpu.VMEM((t, D_RG), F32), pltpu.VMEM((t, D_RG), F32), pltpu.VMEM((1, D_RG), F32),
            pltpu.VMEM((t, N_STATE), F32), pltpu.VMEM((t, N_STATE), F32),
            pltpu.VMEM((1, N_STATE), F32), pltpu.VMEM((1, N_STATE), F32),
        ],
        compiler_params=pltpu.CompilerParams(
            dimension_semantics=("arbitrary", "arbitrary"), vmem_limit_bytes=VMEM_LIMIT),
        name="mixer",
    )(x2, p["g_mix"], p["w_in"], p["conv_w"], p["conv_b"], p["w_gates"], p["b_gates"], p["sp"],
      p["bmat"], p["s5tab"], p["cmat"], p["d_skip"], p["glu_w"], p["glu_b"], p["w_out"])


def _kv_kernel(m_ref, g_ref, w_ref, k_ref, v_ref):
    mn = _rms(m_ref[...], g_ref[...]).astype(BF16)
    kv = jnp.dot(mn, w_ref[...], preferred_element_type=F32)
    k_ref[...] = kv[:, :D_MODEL].astype(BF16)
    v_ref[...] = kv[:, D_MODEL:].astype(BF16)


def _kv(mem2, bsz, mlen, g, wkv):
    return pl.pallas_call(
        _kv_kernel,
        out_shape=(jax.ShapeDtypeStruct(mem2.shape, BF16), jax.ShapeDtypeStruct(mem2.shape, BF16)),
        grid=(bsz,),
        in_specs=[pl.BlockSpec((mlen, D_MODEL), lambda b: (b, 0)),
                  pl.BlockSpec((1, D_MODEL), lambda b: (0, 0)),
                  pl.BlockSpec((D_MODEL, 2 * D_MODEL), lambda b: (0, 0))],
        out_specs=(pl.BlockSpec((mlen, D_MODEL), lambda b: (b, 0)),
                   pl.BlockSpec((mlen, D_MODEL), lambda b: (b, 0))),
        compiler_params=pltpu.CompilerParams(dimension_semantics=("arbitrary",), vmem_limit_bytes=VMEM_LIMIT),
        name="kv",
    )(mem2, g, wkv)


def _attn_kernel(h_ref, g_ref, wq_ref, k_ref, v_ref, wo_ref, o_ref):
    h = h_ref[...]
    xn = _rms(h, g_ref[...]).astype(BF16)
    q = jnp.dot(xn, wq_ref[...], preferred_element_type=F32).astype(BF16)
    outs = []
    for hd in range(XA_HEADS):
        cs = slice(hd * XA_HD, (hd + 1) * XA_HD)
        s = lax.dot_general(q[:, cs], k_ref[:, cs], (((1,), (1,)), ((), ())),
                            preferred_element_type=F32) * (XA_HD ** -0.5)
        s = s - jnp.max(s, axis=-1, keepdims=True)
        e = jnp.exp(s)
        pr = e / jnp.sum(e, axis=-1, keepdims=True)
        outs.append(jnp.dot(pr.astype(BF16), v_ref[:, cs], preferred_element_type=F32))
    o = jnp.concatenate(outs, axis=1).astype(BF16)
    o_ref[...] = h + jnp.dot(o, wo_ref[...], preferred_element_type=F32)


def _attn(h2, bsz, seq, mlen, g, wq, k, v, wo):
    t = min(T_ATT, seq)
    nc = seq // t
    return pl.pallas_call(
        _attn_kernel,
        out_shape=jax.ShapeDtypeStruct(h2.shape, F32),
        grid=(bsz, nc),
        in_specs=[pl.BlockSpec((t, D_MODEL), lambda b, c: (b * nc + c, 0)),
                  pl.BlockSpec((1, D_MODEL), lambda b, c: (0, 0)),
                  pl.BlockSpec((D_MODEL, D_MODEL), lambda b, c: (0, 0)),
                  pl.BlockSpec((mlen, D_MODEL), lambda b, c: (b, 0)),
                  pl.BlockSpec((mlen, D_MODEL), lambda b, c: (b, 0)),
                  pl.BlockSpec((D_MODEL, D_MODEL), lambda b, c: (0, 0))],
        out_specs=pl.BlockSpec((t, D_MODEL), lambda b, c: (b * nc + c, 0)),
        compiler_params=pltpu.CompilerParams(
            dimension_semantics=("arbitrary", "arbitrary"), vmem_limit_bytes=VMEM_LIMIT),
        name="attn",
    )(h2, g, wq, k, v, wo)


def _route_kernel(h_ref, g_ref, wr_ref, br_ref, xn_ref, lpos_ref, gate_ref, tcnt_ref):
    t = h_ref.shape[0]
    xn = _rms(h_ref[...], g_ref[...])
    xn_ref[...] = xn.astype(BF16)
    lg = lax.dot_general(wr_ref[...], xn, (((1,), (1,)), ((), ())),
                         precision=lax.Precision.HIGHEST, preferred_element_type=F32) + br_ref[...]
    erow = lax.broadcasted_iota(jnp.int32, (N_EXPERTS, t), 0)
    vals, hots = [], []
    for _ in range(TOP_K):
        m = jnp.max(lg, axis=0, keepdims=True)
        ix = jnp.min(jnp.where(lg == m, erow, N_EXPERTS), axis=0, keepdims=True)
        hot = erow == ix
        lg = jnp.where(hot, -jnp.inf, lg)
        vals.append(m)
        hots.append(hot)
    es = [jnp.exp(v - vals[0]) for v in vals]
    den = es[0] + es[1] + es[2] + es[3]
    sel = jnp.zeros((N_EXPERTS, t), F32)
    for hot in hots:
        sel = sel + jnp.where(hot, 1.0, 0.0)
    tri = jnp.where(lax.broadcasted_iota(jnp.int32, (t, t), 0) < lax.broadcasted_iota(jnp.int32, (t, t), 1),
                    1.0, 0.0).astype(BF16)
    rank = jnp.dot(sel.astype(BF16), tri, preferred_element_type=F32)
    cnt = jnp.sum(sel, axis=1, keepdims=True)
    run8 = jnp.floor((cnt + (SUBLANES - 1.0)) * (1.0 / SUBLANES))
    below = jnp.where(lax.broadcasted_iota(jnp.int32, (N_EXPERTS, N_EXPERTS), 1)
                      < lax.broadcasted_iota(jnp.int32, (N_EXPERTS, N_EXPERTS), 0), 1.0, 0.0).astype(BF16)
    start8 = jnp.dot(below, jnp.broadcast_to(run8, (N_EXPERTS, 128)).astype(BF16), preferred_element_type=F32)
    pos = rank + SUBLANES * start8[:, 0:1]
    for k in range(TOP_K):
        gate_ref[k:k + 1, :] = es[k] / den
        lpos_ref[k:k + 1, :] = jnp.sum(jnp.where(hots[k], pos, 0.0), axis=0, keepdims=True).astype(jnp.int32)
    tcnt_ref[0] = jnp.broadcast_to(cnt, (N_EXPERTS, 128)).astype(jnp.int32)


def _route(h2, g, wr_t, br):
    n = h2.shape[0]
    t = min(T_TILE, n)
    return pl.pallas_call(
        _route_kernel,
        out_shape=(jax.ShapeDtypeStruct((n, D_MODEL), BF16),
                   jax.ShapeDtypeStruct((TOP_K, n), jnp.int32),
                   jax.ShapeDtypeStruct((TOP_K, n), F32),
                   jax.ShapeDtypeStruct((n // t, N_EXPERTS, 128), jnp.int32)),
        grid=(n // t,),
        in_specs=[pl.BlockSpec((t, D_MODEL), lambda i: (i, 0)),
                  pl.BlockSpec((1, D_MODEL), lambda i: (0, 0)),
                  pl.BlockSpec((N_EXPERTS, D_MODEL), lambda i: (0, 0)),
                  pl.BlockSpec((N_EXPERTS, 1), lambda i: (0, 0))],
        out_specs=(pl.BlockSpec((t, D_MODEL), lambda i: (i, 0)),
                   pl.BlockSpec((TOP_K, t), lambda i: (0, i)),
                   pl.BlockSpec((TOP_K, t), lambda i: (0, i)),
                   pl.BlockSpec((1, N_EXPERTS, 128), lambda i: (i, 0, 0))),
        compiler_params=pltpu.CompilerParams(dimension_semantics=("arbitrary",), vmem_limit_bytes=VMEM_LIMIT),
        name="route",
    )(h2, g, wr_t, br)


def _run_copies(i, lo_ref, go_ref, n8_ref, make_copy):
    def expert(e, carry):
        idx = i * N_EXPERTS + e
        lo = lo_ref[idx]
        go = go_ref[idx]

        def chunk(j, c):
            make_copy(pl.multiple_of(lo + j * SUBLANES, SUBLANES), pl.multiple_of(go + j * SUBLANES, SUBLANES)).start()
            return c

        lax.fori_loop(0, n8_ref[idx], chunk, 0)
        return carry

    lax.fori_loop(0, N_EXPERTS, expert, 0)


def _dispatch_kernel(lo_ref, go_ref, n8_ref, nt_ref, zs_ref, zn_ref, xn_ref, lpos_ref, xs_hbm, loc, zbuf, sem,
                     sem_z):
    i = pl.program_id(0)
    last = pl.num_programs(0) - 1
    slot = lax.rem(i, 2)
    nloc, t = loc.shape[1], xn_ref.shape[0]

    def row_copy(sl, lo, go):
        return pltpu.make_async_copy(loc.at[sl, pl.ds(lo, SUBLANES), :], xs_hbm.at[pl.ds(go, SUBLANES), :],
                                     sem.at[sl])

    def wait_copies(sl, count):
        def wait_one(j, carry):
            row_copy(sl, 0, 0).wait()
            return carry
        lax.fori_loop(0, count, wait_one, 0)

    @pl.when(i >= 2)
    def _():
        wait_copies(slot, nt_ref[jnp.maximum(i - 2, 0)])

    liota = lax.broadcasted_iota(jnp.int32, (nloc, t), 0)
    pick = jnp.zeros((nloc, t), F32)
    for k in range(TOP_K):
        pick = jnp.where(liota == lpos_ref[k:k + 1, :], 1.0, pick)
    loc[slot] = jnp.dot(pick.astype(BF16), xn_ref[...], preferred_element_type=F32)
    _run_copies(i, lo_ref, go_ref, n8_ref, lambda lo, go: row_copy(slot, lo, go))

    @pl.when(i == last)
    def _():
        wait_copies(slot, nt_ref[i])

        @pl.when(i >= 1)
        def _():
            wait_copies(1 - slot, nt_ref[jnp.maximum(i - 1, 0)])

        zbuf[...] = jnp.zeros_like(zbuf)

        def zero_copy(go):
            return pltpu.make_async_copy(zbuf.at[pl.ds(0, SUBLANES), :], xs_hbm.at[pl.ds(go, SUBLANES), :], sem_z)

        def expert(e, total):
            def chunk(j, carry):
                zero_copy(pl.multiple_of(zs_ref[e] + j * SUBLANES, SUBLANES)).start()
                return carry
            lax.fori_loop(0, zn_ref[e], chunk, 0)
            return total + zn_ref[e]

        total = lax.fori_loop(0, N_EXPERTS, expert, 0)

        def wait_zero(j, carry):
            zero_copy(0).wait()
            return carry

        lax.fori_loop(0, total, wait_zero, 0)

        def block_copy(go):
            return pltpu.make_async_copy(zbuf, xs_hbm.at[pl.ds(go, TM_MOE), :], sem_z)

        def start_block(j, carry):
            block_copy(pl.multiple_of(zs_ref[N_EXPERTS] + j * TM_MOE, TM_MOE)).start()
            return carry

        def wait_block(j, carry):
            block_copy(0).wait()
            return carry

        lax.fori_loop(0, zn_ref[N_EXPERTS], start_block, 0)
        lax.fori_loop(0, zn_ref[N_EXPERTS], wait_block, 0)


def _dispatch(tabs, xn, lpos_t, n_slots, nloc):
    n = xn.shape[0]
    t = min(T_TILE, n)
    grid_spec = pltpu.PrefetchScalarGridSpec(
        num_scalar_prefetch=6,
        grid=(n // t,),
        in_specs=[pl.BlockSpec((t, D_MODEL), lambda i, *_: (i, 0)),
                  pl.BlockSpec((TOP_K, t), lambda i, *_: (0, i))],
        out_specs=pl.BlockSpec(memory_space=pl.ANY),
        scratch_shapes=[pltpu.VMEM((2, nloc, D_MODEL), F32), pltpu.VMEM((TM_MOE, D_MODEL), F32),
                        pltpu.SemaphoreType.DMA((2,)), pltpu.SemaphoreType.DMA],
    )
    return pl.pallas_call(
        _dispatch_kernel,
        out_shape=jax.ShapeDtypeStruct((n_slots, D_MODEL), F32),
        grid_spec=grid_spec,
        compiler_params=pltpu.CompilerParams(dimension_semantics=("arbitrary",), vmem_limit_bytes=VMEM_LIMIT),
        name="dispatch",
    )(tabs["lo"], tabs["go"], tabs["n8"], tabs["nt"], tabs["zs"], tabs["zn"], xn, lpos_t)


def _moe_kernel(bexp_ref, nused_ref, nexp_ref, x_ref, wgu_hbm, bgu_ref, wd_hbm, bd_ref, o_ref,
                wgu_f, wd_f, wgu_s, wd_s, sem_w):
    b = pl.program_id(0)
    prev = bexp_ref[jnp.maximum(b - 1, 0)]
    fresh = (b == 0) | (bexp_ref[b] != prev)

    def fetch(ex):
        return (pltpu.make_async_copy(wgu_hbm.at[ex], wgu_f, sem_w.at[0]),
                pltpu.make_async_copy(wd_hbm.at[ex], wd_f, sem_w.at[1]))

    @pl.when(b == 0)
    def _():
        for cp in fetch(bexp_ref[0]):
            cp.start()

    @pl.when(fresh & (b < nused_ref[0]))
    def _():
        for cp in fetch(bexp_ref[b]):
            cp.wait()
        rows = 128
        for r0 in range(0, D_MODEL, rows):
            wgu_s[r0:r0 + rows, :] = wgu_f[r0:r0 + rows, :].astype(BF16)
        for r0 in range(0, D_FF, rows):
            wd_s[r0:r0 + rows, :] = wd_f[r0:r0 + rows, :].astype(BF16)

        @pl.when(nexp_ref[b] >= 0)
        def _():
            for cp in fetch(nexp_ref[b]):
                cp.start()

    @pl.when(b < nused_ref[0])
    def _():
        xb = x_ref[...].astype(BF16)
        gu = jnp.dot(xb, wgu_s[...], preferred_element_type=F32) + bgu_ref[0]
        g = jnp.minimum(gu[:, :D_FF], SWIGLU_LIMIT)
        u = jnp.clip(gu[:, D_FF:], -SWIGLU_LIMIT, SWIGLU_LIMIT)
        h = (u + 1.0) * (g * _sigmoid(SWIGLU_ALPHA * g))
        o_ref[...] = jnp.dot(h.astype(BF16), wd_s[...], preferred_element_type=F32) + bd_ref[0]

    @pl.when(b >= nused_ref[0])
    def _():
        o_ref[...] = jnp.zeros_like(o_ref)


def _moe(block_exp, n_used, next_exp, xs, w_gu, b_gu, w_down, b_down):
    n_slots = xs.shape[0]
    nb = n_slots // TM_MOE
    bspec = lambda shape: pl.BlockSpec(shape, lambda b, be, nu, ne: (be[b], 0, 0))
    grid_spec = pltpu.PrefetchScalarGridSpec(
        num_scalar_prefetch=3,
        grid=(nb,),
        in_specs=[
            pl.BlockSpec((TM_MOE, D_MODEL), lambda b, be, nu, ne: (jnp.minimum(b, nu[0] - 1), 0)),
            pl.BlockSpec(memory_space=pl.ANY), bspec((1, 1, 2 * D_FF)),
            pl.BlockSpec(memory_space=pl.ANY), bspec((1, 1, D_MODEL)),
        ],
        out_specs=pl.BlockSpec((TM_MOE, D_MODEL), lambda b, be, nu, ne: (b, 0)),
        scratch_shapes=[pltpu.VMEM((D_MODEL, 2 * D_FF), F32), pltpu.VMEM((D_FF, D_MODEL), F32),
                        pltpu.VMEM((D_MODEL, 2 * D_FF), BF16), pltpu.VMEM((D_FF, D_MODEL), BF16),
                        pltpu.SemaphoreType.DMA((2,))],
    )
    return pl.pallas_call(
        _moe_kernel,
        out_shape=jax.ShapeDtypeStruct((n_slots, D_MODEL), F32),
        grid_spec=grid_spec,
        compiler_params=pltpu.CompilerParams(dimension_semantics=("arbitrary",), vmem_limit_bytes=VMEM_LIMIT),
        name="moe",
    )(block_exp, n_used, next_exp, xs, w_gu, b_gu, w_down, b_down)


def _combine_kernel(lo_ref, go_ref, n8_ref, nt_ref, ys_hbm, h_ref, lpos_ref, gate_ref, g_ref, o_ref, loc, sem):
    i = pl.program_id(0)
    slot = lax.rem(i, 2)
    nloc, t = loc.shape[1], h_ref.shape[0]

    def row_copy(sl, lo, go):
        return pltpu.make_async_copy(ys_hbm.at[pl.ds(go, SUBLANES), :], loc.at[sl, pl.ds(lo, SUBLANES), :],
                                     sem.at[sl])

    def fetch(tile, sl):
        _run_copies(tile, lo_ref, go_ref, n8_ref, lambda lo, go: row_copy(sl, lo, go))

        def zero_rows(j, carry):
            r0 = pl.multiple_of((nt_ref[tile] + j) * SUBLANES, SUBLANES)
            loc[sl, pl.ds(r0, SUBLANES), :] = jnp.zeros((SUBLANES, D_MODEL), F32)
            return carry

        lax.fori_loop(0, nloc // SUBLANES - nt_ref[tile], zero_rows, 0)

    @pl.when(i == 0)
    def _():
        fetch(0, 0)

    @pl.when(i + 1 < pl.num_programs(0))
    def _():
        fetch(i + 1, 1 - slot)

    liota = lax.broadcasted_iota(jnp.int32, (t, nloc), 1)
    lpos = lpos_ref[...]
    gates = gate_ref[...]
    wsel = jnp.zeros((t, nloc), F32)
    for k in range(TOP_K):
        wsel = jnp.where(liota == lpos[:, k:k + 1], gates[:, k:k + 1], wsel)

    def wait_one(j, carry):
        row_copy(slot, 0, 0).wait()
        return carry

    lax.fori_loop(0, nt_ref[i], wait_one, 0)
    acc = h_ref[...] + jnp.dot(wsel.astype(BF16), loc[slot].astype(BF16), preferred_element_type=F32)
    o_ref[...] = _rms(acc, g_ref[...])


def _combine(tabs, ys, h2, lpos_tok, gates_tok, g, nloc):
    n = h2.shape[0]
    t = min(T_TILE, n)
    grid_spec = pltpu.PrefetchScalarGridSpec(
        num_scalar_prefetch=4,
        grid=(n // t,),
        in_specs=[pl.BlockSpec(memory_space=pl.ANY),
                  pl.BlockSpec((t, D_MODEL), lambda i, *_: (i, 0)),
                  pl.BlockSpec((t, TOP_K), lambda i, *_: (i, 0)),
                  pl.BlockSpec((t, TOP_K), lambda i, *_: (i, 0)),
                  pl.BlockSpec((1, D_MODEL), lambda i, *_: (0, 0))],
        out_specs=pl.BlockSpec((t, D_MODEL), lambda i, *_: (i, 0)),
        scratch_shapes=[pltpu.VMEM((2, nloc, D_MODEL), F32), pltpu.SemaphoreType.DMA((2,))],
    )
    return pl.pallas_call(
        _combine_kernel,
        out_shape=jax.ShapeDtypeStruct((n, D_MODEL), F32),
        grid_spec=grid_spec,
        compiler_params=pltpu.CompilerParams(dimension_semantics=("arbitrary",), vmem_limit_bytes=VMEM_LIMIT),
        name="combine",
    )(tabs["lo"], tabs["go"], tabs["n8"], tabs["nt"], ys, h2, lpos_tok, gates_tok, g)


def _cmul(a, b):
    return a[0] * b[0] - a[1] * b[1], a[0] * b[1] + a[1] * b[0]


def _prep_mixer_params(sub_len, norm_mix_g, w_in, conv_w, conv_b, rg_wa, rg_ba, rg_wx, rg_bx, rg_lambda,
                       s5_lambda_re, s5_lambda_im, s5_b_re, s5_b_im, s5_c_re, s5_c_im, s5_d, s5_log_dt,
                       s5_glu_w, s5_glu_b, w_out):
    row = lambda v: v.reshape(1, -1).astype(F32)
    dt = jnp.exp(s5_log_dt)[:, None]
    mag = jnp.exp(dt * s5_lambda_re)
    abar_re = mag * jnp.cos(dt * s5_lambda_im)
    abar_im = mag * jnp.sin(dt * s5_lambda_im)
    den = s5_lambda_re * s5_lambda_re + s5_lambda_im * s5_lambda_im
    num_re = abar_re - 1.0
    coef_re = (num_re * s5_lambda_re + abar_im * s5_lambda_im) / den
    coef_im = (abar_im * s5_lambda_re - num_re * s5_lambda_im) / den
    bbar_re = coef_re[..., None] * s5_b_re - coef_im[..., None] * s5_b_im
    bbar_im = coef_re[..., None] * s5_b_im + coef_im[..., None] * s5_b_re
    a1 = (abar_re.reshape(-1), abar_im.reshape(-1))
    aq = a1
    for _ in range(int(math.log2(sub_len))):
        aq = _cmul(aq, aq)
    pw = [aq]
    for _ in range(SUBLANES - 1):
        pw.append(_cmul(pw[-1], aq))
    sub = jnp.arange(SUBLANES)[:, None]
    tabs = []
    for d in (1, 2, 4):
        keep = (sub >= d).astype(F32)
        tabs += [keep * pw[d - 1][0][None, :], keep * pw[d - 1][1][None, :]]
    tabs += [jnp.stack([p_[0] for p_ in pw]), jnp.stack([p_[1] for p_ in pw])]
    tabs += [jnp.broadcast_to(a1[0][None, :], (SUBLANES, N_STATE)),
             jnp.broadcast_to(a1[1][None, :], (SUBLANES, N_STATE))]
    s5tab = jnp.stack(tabs).astype(F32)
    gh = S5_GROUPS // 2
    bmat, cmat = [], []
    for hf in range(2):
        gs = slice(hf * gh, (hf + 1) * gh)
        b_re = _block_diag(jnp.transpose(bbar_re[gs], (0, 2, 1)))
        b_im = _block_diag(jnp.transpose(bbar_im[gs], (0, 2, 1)))
        bmat.append(jnp.concatenate([b_re, b_im], axis=1))
        cr = _block_diag(jnp.transpose(s5_c_re[gs], (0, 2, 1)))
        ci = _block_diag(jnp.transpose(s5_c_im[gs], (0, 2, 1)))
        cmat.append(jnp.concatenate([cr, -ci], axis=0))
    return {
        "g_mix": row(norm_mix_g), "w_in": w_in.astype(BF16), "conv_w": conv_w.astype(F32), "conv_b": row(conv_b),
        "w_gates": jnp.concatenate([_block_diag(rg_wa), _block_diag(rg_wx)], axis=1).astype(BF16),
        "b_gates": jnp.concatenate([rg_ba, rg_bx]).reshape(1, -1).astype(F32),
        "sp": row(jax.nn.softplus(-rg_lambda)),
        "bmat": jnp.stack(bmat).astype(BF16), "s5tab": s5tab, "cmat": jnp.stack(cmat).astype(BF16),
        "d_skip": row(s5_d), "glu_w": _block_diag(s5_glu_w).astype(BF16), "glu_b": row(s5_glu_b),
        "w_out": w_out.astype(BF16),
    }


def _slot_tables(tcnt, n_slots):
    i32 = lambda v: v.astype(jnp.int32)
    run = ((tcnt + SUBLANES - 1) // SUBLANES) * SUBLANES
    rows = jnp.sum(run, axis=0)
    blocks = (rows + TM_MOE - 1) // TM_MOE
    bend = jnp.cumsum(blocks)
    estart = (bend - blocks) * TM_MOE
    nb = n_slots // TM_MOE
    n_used = i32(bend[-1])
    block_exp = jnp.minimum(jnp.sum(jnp.arange(nb)[:, None] >= bend[None, :], axis=1), N_EXPERTS - 1)
    block_exp = jnp.where(jnp.arange(nb) < n_used, block_exp, block_exp[jnp.maximum(n_used - 1, 0)])
    eidx = jnp.arange(N_EXPERTS)
    later = (eidx[None, :] > eidx[:, None]) & (blocks[None, :] > 0)
    next_of = jnp.min(jnp.where(later, eidx[None, :], N_EXPERTS), axis=1)
    next_exp = jnp.where(next_of < N_EXPERTS, next_of, -1)[block_exp]
    tabs = {
        "lo": i32(jnp.cumsum(run, axis=1) - run).reshape(-1),
        "go": i32(estart[None, :] + jnp.cumsum(run, axis=0) - run).reshape(-1),
        "n8": i32(run // SUBLANES).reshape(-1),
        "nt": i32(jnp.sum(run, axis=1) // SUBLANES),
        "zs": i32(jnp.concatenate([estart + rows, bend[-1:] * TM_MOE])),
        "zn": i32(jnp.concatenate([(blocks * TM_MOE - rows) // SUBLANES, nb - bend[-1:]])),
    }
    return i32(block_exp), n_used.reshape(1), i32(next_exp), tabs


def kernel(x, mem, norm_mix_g, w_in, conv_w, conv_b, rg_wa, rg_ba, rg_wx, rg_bx, rg_lambda, s5_lambda_re,
           s5_lambda_im, s5_b_re, s5_b_im, s5_c_re, s5_c_im, s5_d, s5_log_dt, s5_glu_w, s5_glu_b, w_out,
           norm_xa_g, mem_norm_g, xa_wq, xa_wkv, xa_wo, norm_ffn_g, router_w, router_b, exp_w_gu, exp_b_gu,
           exp_w_down, exp_b_down, final_norm_g):
    bsz, seq, d = x.shape
    mlen = mem.shape[1]
    n = bsz * seq
    row = lambda v: v.reshape(1, -1).astype(F32)

    p = _prep_mixer_params(min(T_MIX, seq) // SUBLANES, norm_mix_g, w_in, conv_w, conv_b, rg_wa, rg_ba, rg_wx,
                           rg_bx, rg_lambda, s5_lambda_re, s5_lambda_im, s5_b_re, s5_b_im, s5_c_re, s5_c_im,
                           s5_d, s5_log_dt, s5_glu_w, s5_glu_b, w_out)
    h1 = _mixer(x.reshape(n, d), bsz, seq, p)

    k, v = _kv(mem.reshape(bsz * mlen, d), bsz, mlen, row(mem_norm_g), xa_wkv.astype(BF16))
    h2 = _attn(h1, bsz, seq, mlen, row(norm_xa_g), xa_wq.astype(BF16), k, v, xa_wo.astype(BF16))

    xn, lpos_t, gate_t, tcnt = _route(h2, row(norm_ffn_g), router_w.T.astype(F32),
                                      router_b.reshape(-1, 1).astype(F32))
    tile = min(T_TILE, n)
    n_tiles = n // tile
    run_pad = N_EXPERTS * (SUBLANES - 1)
    nloc = -(-(TOP_K * tile + run_pad) // 128) * 128
    n_slots = -(-(TOP_K * n + n_tiles * run_pad + N_EXPERTS * (TM_MOE - SUBLANES)) // TM_MOE) * TM_MOE
    block_exp, n_used, next_exp, tabs = _slot_tables(tcnt[:, :, 0], n_slots)
    xs = _dispatch(tabs, xn, lpos_t, n_slots, nloc)
    ys = _moe(block_exp, n_used, next_exp, xs, exp_w_gu, exp_b_gu.reshape(N_EXPERTS, 1, -1),
              exp_w_down, exp_b_down.reshape(N_EXPERTS, 1, -1))
    out = _combine(tabs, ys, h2, lpos_t.T, gate_t.T, row(final_norm_g), nloc)
    return out.reshape(bsz, seq, d)
```

```python
import math

import jax
import jax.numpy as jnp
from jax import lax
from jax.experimental import pallas as pl
from jax.experimental.pallas import tpu as pltpu

F32 = jnp.float32
BF16 = jnp.bfloat16

D_MODEL = 1024
D_RG = 512
D_S5 = 512
CONV_W = 4
RG_C = 8.0
S5_GROUPS = 32
S5_STATE = 64
N_STATE = S5_GROUPS * S5_STATE
XA_HEADS = 4
XA_HD = 256
N_EXPERTS = 32
TOP_K = 4
D_FF = 1024
SWIGLU_LIMIT = 7.0
SWIGLU_ALPHA = 1.702
NORM_EPS = 1e-6

SUBLANES = 8
T_MIX = 512
T_ATT = 512
T_TILE = 512
TM_MOE = 256
VMEM_LIMIT = 56 * 1024 * 1024


def _rms(x, g):
    return x * lax.rsqrt(jnp.mean(x * x, axis=-1, keepdims=True) + NORM_EPS) * g


def _gelu_tanh(x):
    return 0.5 * x * (1.0 + jnp.tanh(math.sqrt(2.0 / math.pi) * (x + 0.044715 * (x * x * x))))


def _sigmoid(x):
    return 1.0 / (1.0 + jnp.exp(-x))


def _block_diag(blocks):
    n, r, c = blocks.shape
    eye = jnp.eye(n, dtype=blocks.dtype)
    return jnp.einsum('nrc,nm->nrmc', blocks, eye).reshape(n * r, n * c)


def _sublane_chain(al, e, carry, sub):
    for d in (1, 2, 4):
        al_sh = jnp.where(sub >= d, pltpu.roll(al, d, axis=0), 1.0)
        e_sh = jnp.where(sub >= d, pltpu.roll(e, d, axis=0), 0.0)
        e = al * e_sh + e
        al = al * al_sh
    s_end = e + al * carry
    start = jnp.where(sub == 0, carry, pltpu.roll(s_end, 1, axis=0))
    return start, s_end[SUBLANES - 1:SUBLANES, :]


def _mixer_kernel(x_ref, g_ref, win_ref, convw_ref, convb_ref, wg_ref, bg_ref, sp_ref,
                  bmat_ref, tab_ref, cmat_ref, dskip_ref, glu_ref, glub_ref, wout_ref,
                  o_ref,
                  xin, xout, sem_i, sem_o, xbuf, hist, a_s, b_s, rg_carry, sre, sim, c_re, c_im):
    ngrp = xin.shape[1]
    t = ngrp * SUBLANES
    nh = (CONV_W - 1) * SUBLANES
    c = pl.program_id(1)
    step = pl.program_id(0) * pl.num_programs(1) + c
    nsteps = pl.num_programs(0) * pl.num_programs(1)
    slot = lax.rem(step, 2)

    def in_copy(s, sl, i):
        return pltpu.make_async_copy(x_ref.at[pl.ds(s * t + i * ngrp, ngrp), :], xin.at[sl, :, i, :], sem_i.at[sl])

    def out_copy(s, sl, i):
        return pltpu.make_async_copy(xout.at[sl, :, i, :], o_ref.at[pl.ds(s * t + i * ngrp, ngrp), :], sem_o.at[sl])

    @pl.when(step == 0)
    def _():
        for i in range(SUBLANES):
            in_copy(0, 0, i).start()

    @pl.when(c == 0)
    def _():
        hist[...] = jnp.zeros_like(hist)
        rg_carry[...] = jnp.zeros_like(rg_carry)
        c_re[...] = jnp.zeros_like(c_re)
        c_im[...] = jnp.zeros_like(c_im)

    for i in range(SUBLANES):
        in_copy(step, slot, i).wait()

    @pl.when(step + 1 < nsteps)
    def _():
        for i in range(SUBLANES):
            in_copy(step + 1, 1 - slot, i).start()

    x = xin[slot].reshape(t, D_MODEL)
    xn = _rms(x, g_ref[...]).astype(BF16)
    proj = jnp.dot(xn, win_ref[...], preferred_element_type=F32)
    gate_rg = proj[:, :D_RG]
    x_rg = proj[:, D_RG:2 * D_RG]
    u = proj[:, 2 * D_RG:]

    sub = lax.broadcasted_iota(jnp.int32, (SUBLANES, D_RG), 0)
    cur_tail = x_rg[t - nh:, :]
    for gidx in range(CONV_W - 1):
        rs = slice(gidx * SUBLANES, (gidx + 1) * SUBLANES)
        xbuf[rs, :] = jnp.where(sub == 0, pltpu.roll(hist[rs, :], 1, axis=0), pltpu.roll(cur_tail[rs, :], 1, axis=0))
    hist[...] = cur_tail
    xbuf[nh:, :] = x_rg
    xc = convb_ref[...] + convw_ref[CONV_W - 1:CONV_W, :] * x_rg
    for k in range(CONV_W - 1):
        xc = xc + convw_ref[k:k + 1, :] * xbuf[k * SUBLANES:k * SUBLANES + t, :]
    gates = jnp.dot(xc.astype(BF16), wg_ref[...], preferred_element_type=F32) + bg_ref[...]
    r = _sigmoid(gates[:, :D_RG])
    ig = _sigmoid(gates[:, D_RG:])
    log_a = (-RG_C) * r * sp_ref[...]
    a = jnp.exp(log_a)
    mult = jnp.sqrt(1.0 - a * a)
    row = lax.broadcasted_iota(jnp.int32, (t, D_RG), 0)
    mult = jnp.where((row == 0) & (c == 0), 1.0, mult)
    a_s[...] = a
    b_s[...] = mult * (ig * xc)

    def rg_pass(store):
        def body(q, carry):
            h, ac = carry
            r0 = pl.multiple_of(q * SUBLANES, SUBLANES)
            aa = a_s[pl.ds(r0, SUBLANES), :]
            h = aa * h + b_s[pl.ds(r0, SUBLANES), :]
            if store:
                b_s[pl.ds(r0, SUBLANES), :] = h
                return h, ac
            return h, aa * ac
        return body

    zeros8 = jnp.zeros((SUBLANES, D_RG), F32)
    ones8 = jnp.ones((SUBLANES, D_RG), F32)
    e_loc, a_loc = lax.fori_loop(0, ngrp, rg_pass(False), (zeros8, ones8), unroll=4)
    start, last = _sublane_chain(a_loc, e_loc, rg_carry[...], sub)
    rg_carry[...] = last
    lax.fori_loop(0, ngrp, rg_pass(True), (start, ones8), unroll=4)
    y_rg = b_s[...] * _gelu_tanh(gate_rg)

    ub = u.astype(BF16)
    half = N_STATE // 2
    for hf in range(2):
        bu = jnp.dot(ub[:, hf * 256:(hf + 1) * 256], bmat_ref[hf], preferred_element_type=F32)
        sre[:, hf * half:(hf + 1) * half] = bu[:, :half]
        sim[:, hf * half:(hf + 1) * half] = bu[:, half:]

    lb = 1024
    subl = lax.broadcasted_iota(jnp.int32, (SUBLANES, lb), 0)
    for blk in range(N_STATE // lb):
        cs = slice(blk * lb, (blk + 1) * lb)
        ar = tab_ref[8, :, cs]
        ai = tab_ref[9, :, cs]

        def s5_pass(store, cs=cs, ar=ar, ai=ai):
            def body(q, carry):
                hr, hi = carry
                r0 = pl.multiple_of(q * SUBLANES, SUBLANES)
                nr = ar * hr - ai * hi + sre[pl.ds(r0, SUBLANES), cs]
                ni = ar * hi + ai * hr + sim[pl.ds(r0, SUBLANES), cs]
                if store:
                    sre[pl.ds(r0, SUBLANES), cs] = nr
                    sim[pl.ds(r0, SUBLANES), cs] = ni
                return nr, ni
            return body

        z8 = jnp.zeros((SUBLANES, lb), F32)
        er, ei = lax.fori_loop(0, ngrp, s5_pass(False), (z8, z8), unroll=4)
        for i, d in enumerate((1, 2, 4)):
            mr = tab_ref[2 * i, :, cs]
            mi = tab_ref[2 * i + 1, :, cs]
            rr = pltpu.roll(er, d, axis=0)
            ri = pltpu.roll(ei, d, axis=0)
            er, ei = er + (mr * rr - mi * ri), ei + (mr * ri + mi * rr)
        pr = tab_ref[6, :, cs]
        pi_ = tab_ref[7, :, cs]
        cr = c_re[:, cs]
        ci = c_im[:, cs]
        sr = er + (pr * cr - pi_ * ci)
        si = ei + (pr * ci + pi_ * cr)
        c_re[:, cs] = sr[SUBLANES - 1:SUBLANES, :]
        c_im[:, cs] = si[SUBLANES - 1:SUBLANES, :]
        st_r = jnp.where(subl == 0, cr, pltpu.roll(sr, 1, axis=0))
        st_i = jnp.where(subl == 0, ci, pltpu.roll(si, 1, axis=0))
        lax.fori_loop(0, ngrp, s5_pass(True), (st_r, st_i), unroll=4)

    ys = []
    for hf in range(2):
        hcat = jnp.concatenate([sre[:, hf * half:(hf + 1) * half], sim[:, hf * half:(hf + 1) * half]], axis=1)
        ys.append(jnp.dot(hcat.astype(BF16), cmat_ref[hf], preferred_element_type=F32))
    y = jnp.concatenate(ys, axis=1) + dskip_ref[...] * u
    z = _gelu_tanh(y)
    zg = _sigmoid(jnp.dot(z.astype(BF16), glu_ref[...], preferred_element_type=F32) + glub_ref[...])
    y_s5 = z * zg

    ycat = jnp.concatenate([y_rg, y_s5], axis=1).astype(BF16)
    res = x + jnp.dot(ycat, wout_ref[...], preferred_element_type=F32)

    @pl.when(step >= 2)
    def _():
        for i in range(SUBLANES):
            out_copy(step - 2, slot, i).wait()

    xout[slot] = res.reshape(ngrp, SUBLANES, D_MODEL)
    for i in range(SUBLANES):
        out_copy(step, slot, i).start()

    @pl.when(step == nsteps - 1)
    def _():
        for i in range(SUBLANES):
            out_copy(step, slot, i).wait()

        @pl.when(step >= 1)
        def _():
            for i in range(SUBLANES):
                out_copy(step - 1, 1 - slot, i).wait()


def _mixer(x2, bsz, seq, p):
    t = min(T_MIX, seq)
    nc = seq // t
    nh = (CONV_W - 1) * SUBLANES
    const = lambda shape: pl.BlockSpec(shape, lambda b, c: (0,) * len(shape))
    return pl.pallas_call(
        _mixer_kernel,
        out_shape=jax.ShapeDtypeStruct(x2.shape, F32),
        grid=(bsz, nc),
        in_specs=[
            pl.BlockSpec(memory_space=pl.ANY),
            const((1, D_MODEL)), const((D_MODEL, 3 * D_RG)), const((CONV_W, D_RG)), const((1, D_RG)),
            const((D_RG, 2 * D_RG)), const((1, 2 * D_RG)), const((1, D_RG)),
            const((2, 256, N_STATE)), const((10, SUBLANES, N_STATE)), const((2, N_STATE, 256)),
            const((1, D_S5)), const((D_S5, D_S5)), const((1, D_S5)), const((D_MODEL, D_MODEL)),
        ],
        out_specs=pl.BlockSpec(memory_space=pl.ANY),
        scratch_shapes=[
            pltpu.VMEM((2, t // SUBLANES, SUBLANES, D_MODEL), F32),
            pltpu.VMEM((2, t // SUBLANES, SUBLANES, D_MODEL), F32),
            pltpu.SemaphoreType.DMA((2,)), pltpu.SemaphoreType.DMA((2,)),
            pltpu.VMEM((t + nh, D_RG), F32), pltpu.VMEM((nh, D_RG), F32),
            pltpu.VMEM((t, D_RG), F32), pltpu.VMEM((t, D_RG), F32), pltpu.VMEM((1, D_RG), F32),
            pltpu.VMEM((t, N_STATE), F32), pltpu.VMEM((t, N_STATE), F32),
            pltpu.VMEM((1, N_STATE), F32), pltpu.VMEM((1, N_STATE), F32),
        ],
        compiler_params=pltpu.CompilerParams(
            dimension_semantics=("arbitrary", "arbitrary"), vmem_limit_bytes=VMEM_LIMIT),
        name="mixer",
    )(x2, p["g_mix"], p["w_in"], p["conv_w"], p["conv_b"], p["w_gates"], p["b_gates"], p["sp"],
      p["bmat"], p["s5tab"], p["cmat"], p["d_skip"], p["glu_w"], p["glu_b"], p["w_out"])


def _kv_kernel(m_ref, g_ref, w_ref, k_ref, v_ref):
    mn = _rms(m_ref[...], g_ref[...]).astype(BF16)
    kv = jnp.dot(mn, w_ref[...], preferred_element_type=F32)
    k_ref[...] = kv[:, :D_MODEL].astype(BF16)
    v_ref[...] = kv[:, D_MODEL:].astype(BF16)


def _kv(mem2, bsz, mlen, g, wkv):
    return pl.pallas_call(
        _kv_kernel,
        out_shape=(jax.ShapeDtypeStruct(mem2.shape, BF16), jax.ShapeDtypeStruct(mem2.shape, BF16)),
        grid=(bsz,),
        in_specs=[pl.BlockSpec((mlen, D_MODEL), lambda b: (b, 0)),
                  pl.BlockSpec((1, D_MODEL), lambda b: (0, 0)),
                  pl.BlockSpec((D_MODEL, 2 * D_MODEL), lambda b: (0, 0))],
        out_specs=(pl.BlockSpec((mlen, D_MODEL), lambda b: (b, 0)),
                   pl.BlockSpec((mlen, D_MODEL), lambda b: (b, 0))),
        compiler_params=pltpu.CompilerParams(dimension_semantics=("arbitrary",), vmem_limit_bytes=VMEM_LIMIT),
        name="kv",
    )(mem2, g, wkv)


def _attn_kernel(h_ref, g_ref, wq_ref, k_ref, v_ref, wo_ref, o_ref):
    h = h_ref[...]
    xn = _rms(h, g_ref[...]).astype(BF16)
    q = jnp.dot(xn, wq_ref[...], preferred_element_type=F32).astype(BF16)
    outs = []
    for hd in range(XA_HEADS):
        cs = slice(hd * XA_HD, (hd + 1) * XA_HD)
        s = lax.dot_general(q[:, cs], k_ref[:, cs], (((1,), (1,)), ((), ())),
                            preferred_element_type=F32) * (XA_HD ** -0.5)
        s = s - jnp.max(s, axis=-1, keepdims=True)
        e = jnp.exp(s)
        pr = e / jnp.sum(e, axis=-1, keepdims=True)
        outs.append(jnp.dot(pr.astype(BF16), v_ref[:, cs], preferred_element_type=F32))
    o = jnp.concatenate(outs, axis=1).astype(BF16)
    o_ref[...] = h + jnp.dot(o, wo_ref[...], preferred_element_type=F32)


def _attn(h2, bsz, seq, mlen, g, wq, k, v, wo):
    t = min(T_ATT, seq)
    nc = seq // t
    return pl.pallas_call(
        _attn_kernel,
        out_shape=jax.ShapeDtypeStruct(h2.shape, F32),
        grid=(bsz, nc),
        in_specs=[pl.BlockSpec((t, D_MODEL), lambda b, c: (b * nc + c, 0)),
                  pl.BlockSpec((1, D_MODEL), lambda b, c: (0, 0)),
                  pl.BlockSpec((D_MODEL, D_MODEL), lambda b, c: (0, 0)),
                  pl.BlockSpec((mlen, D_MODEL), lambda b, c: (b, 0)),
                  pl.BlockSpec((mlen, D_MODEL), lambda b, c: (b, 0)),
                  pl.BlockSpec((D_MODEL, D_MODEL), lambda b, c: (0, 0))],
        out_specs=pl.BlockSpec((t, D_MODEL), lambda b, c: (b * nc + c, 0)),
        compiler_params=pltpu.CompilerParams(
            dimension_semantics=("arbitrary", "arbitrary"), vmem_limit_bytes=VMEM_LIMIT),
        name="attn",
    )(h2, g, wq, k, v, wo)


def _route_kernel(h_ref, g_ref, wr_ref, br_ref, xn_ref, lpos_ref, gate_ref, tcnt_ref):
    t = h_ref.shape[0]
    xn = _rms(h_ref[...], g_ref[...])
    xn_ref[...] = xn.astype(BF16)
    lg = lax.dot_general(wr_ref[...], xn, (((1,), (1,)), ((), ())),
                         precision=lax.Precision.HIGHEST, preferred_element_type=F32) + br_ref[...]
    erow = lax.broadcasted_iota(jnp.int32, (N_EXPERTS, t), 0)
    vals, hots = [], []
    for _ in range(TOP_K):
        m = jnp.max(lg, axis=0, keepdims=True)
        ix = jnp.min(jnp.where(lg == m, erow, N_EXPERTS), axis=0, keepdims=True)
        hot = erow == ix
        lg = jnp.where(hot, -jnp.inf, lg)
        vals.append(m)
        hots.append(hot)
    es = [jnp.exp(v - vals[0]) for v in vals]
    den = es[0] + es[1] + es[2] + es[3]
    sel = jnp.zeros((N_EXPERTS, t), F32)
    for hot in hots:
        sel = sel + jnp.where(hot, 1.0, 0.0)
    tri = jnp.where(lax.broadcasted_iota(jnp.int32, (t, t), 0) < lax.broadcasted_iota(jnp.int32, (t, t), 1),
                    1.0, 0.0).astype(BF16)
    rank = jnp.dot(sel.astype(BF16), tri, preferred_element_type=F32)
    cnt = jnp.sum(sel, axis=1, keepdims=True)
    run8 = jnp.floor((cnt + (SUBLANES - 1.0)) * (1.0 / SUBLANES))
    below = jnp.where(lax.broadcasted_iota(jnp.int32, (N_EXPERTS, N_EXPERTS), 1)
                      < lax.broadcasted_iota(jnp.int32, (N_EXPERTS, N_EXPERTS), 0), 1.0, 0.0).astype(BF16)
    start8 = jnp.dot(below, jnp.broadcast_to(run8, (N_EXPERTS, 128)).astype(BF16), preferred_element_type=F32)
    pos = rank + SUBLANES * start8[:, 0:1]
    for k in range(TOP_K):
        gate_ref[k:k + 1, :] = es[k] / den
        lpos_ref[k:k + 1, :] = jnp.sum(jnp.where(hots[k], pos, 0.0), axis=0, keepdims=True).astype(jnp.int32)
    tcnt_ref[0] = jnp.broadcast_to(cnt, (N_EXPERTS, 128)).astype(jnp.int32)


def _route(h2, g, wr_t, br):
    n = h2.shape[0]
    t = min(T_TILE, n)
    return pl.pallas_call(
        _route_kernel,
        out_shape=(jax.ShapeDtypeStruct((n, D_MODEL), BF16),
                   jax.ShapeDtypeStruct((TOP_K, n), jnp.int32),
                   jax.ShapeDtypeStruct((TOP_K, n), F32),
                   jax.ShapeDtypeStruct((n // t, N_EXPERTS, 128), jnp.int32)),
        grid=(n // t,),
        in_specs=[pl.BlockSpec((t, D_MODEL), lambda i: (i, 0)),
                  pl.BlockSpec((1, D_MODEL), lambda i: (0, 0)),
                  pl.BlockSpec((N_EXPERTS, D_MODEL), lambda i: (0, 0)),
                  pl.BlockSpec((N_EXPERTS, 1), lambda i: (0, 0))],
        out_specs=(pl.BlockSpec((t, D_MODEL), lambda i: (i, 0)),
                   pl.BlockSpec((TOP_K, t), lambda i: (0, i)),
                   pl.BlockSpec((TOP_K, t), lambda i: (0, i)),
                   pl.BlockSpec((1, N_EXPERTS, 128), lambda i: (i, 0, 0))),
        compiler_params=pltpu.CompilerParams(dimension_semantics=("arbitrary",), vmem_limit_bytes=VMEM_LIMIT),
        name="route",
    )(h2, g, wr_t, br)


RUN_CHUNK = 64
WAIT_CHUNK = 256


def _run_copies(i, lo_ref, go_ref, n8_ref, make_copy):
    def expert(e, carry):
        idx = i * N_EXPERTS + e
        lo = lo_ref[idx]
        go = go_ref[idx]
        n8 = n8_ref[idx]
        nbig = lax.shift_right_logical(n8, 3)

        def big(j, c):
            make_copy(pl.multiple_of(lo + j * RUN_CHUNK, SUBLANES), pl.multiple_of(go + j * RUN_CHUNK, SUBLANES),
                      RUN_CHUNK).start()
            return c

        lax.fori_loop(0, nbig, big, 0)
        off = nbig * RUN_CHUNK
        for rows in (32, 16, 8):
            has = (n8 & (rows // SUBLANES)) != 0

            @pl.when(has)
            def _(off=off, rows=rows):
                make_copy(pl.multiple_of(lo + off, SUBLANES), pl.multiple_of(go + off, SUBLANES), rows).start()

            off = off + jnp.where(has, rows, 0)
        return carry

    lax.fori_loop(0, N_EXPERTS, expert, 0)


def _wait_rows(n8, make_wait):
    def big(j, c):
        make_wait(WAIT_CHUNK).wait()
        return c

    lax.fori_loop(0, lax.shift_right_logical(n8, 5), big, 0)
    for rows in (128, 64, 32, 16, 8):
        @pl.when((n8 & (rows // SUBLANES)) != 0)
        def _(rows=rows):
            make_wait(rows).wait()


def _dispatch_kernel(lo_ref, go_ref, n8_ref, nt_ref, zs_ref, zn_ref, xn_ref, lpos_ref, xs_hbm, loc, zbuf, sem,
                     sem_z):
    i = pl.program_id(0)
    last = pl.num_programs(0) - 1
    slot = lax.rem(i, 2)
    nloc, t = loc.shape[1], xn_ref.shape[0]

    def row_copy(sl, lo, go, rows):
        return pltpu.make_async_copy(loc.at[sl, pl.ds(lo, rows), :], xs_hbm.at[pl.ds(go, rows), :], sem.at[sl])

    def wait_copies(sl, count):
        _wait_rows(count, lambda rows: row_copy(sl, 0, 0, rows))

    @pl.when(i >= 2)
    def _():
        wait_copies(slot, nt_ref[jnp.maximum(i - 2, 0)])

    liota = lax.broadcasted_iota(jnp.int32, (nloc, t), 0)
    pick = jnp.zeros((nloc, t), F32)
    for k in range(TOP_K):
        pick = jnp.where(liota == lpos_ref[k:k + 1, :], 1.0, pick)
    loc[slot] = jnp.dot(pick.astype(BF16), xn_ref[...], preferred_element_type=F32)
    _run_copies(i, lo_ref, go_ref, n8_ref, lambda lo, go, rows: row_copy(slot, lo, go, rows))

    @pl.when(i == last)
    def _():
        wait_copies(slot, nt_ref[i])

        @pl.when(i >= 1)
        def _():
            wait_copies(1 - slot, nt_ref[jnp.maximum(i - 1, 0)])

        zbuf[...] = jnp.zeros_like(zbuf)

        def zero_copy(go):
            return pltpu.make_async_copy(zbuf.at[pl.ds(0, SUBLANES), :], xs_hbm.at[pl.ds(go, SUBLANES), :], sem_z)

        def expert(e, total):
            def chunk(j, carry):
                zero_copy(pl.multiple_of(zs_ref[e] + j * SUBLANES, SUBLANES)).start()
                return carry
            lax.fori_loop(0, zn_ref[e], chunk, 0)
            return total + zn_ref[e]

        total = lax.fori_loop(0, N_EXPERTS, expert, 0)

        def wait_zero(j, carry):
            zero_copy(0).wait()
            return carry

        lax.fori_loop(0, total, wait_zero, 0)

        def block_copy(go):
            return pltpu.make_async_copy(zbuf, xs_hbm.at[pl.ds(go, TM_MOE), :], sem_z)

        def start_block(j, carry):
            block_copy(pl.multiple_of(zs_ref[N_EXPERTS] + j * TM_MOE, TM_MOE)).start()
            return carry

        def wait_block(j, carry):
            block_copy(0).wait()
            return carry

        lax.fori_loop(0, zn_ref[N_EXPERTS], start_block, 0)
        lax.fori_loop(0, zn_ref[N_EXPERTS], wait_block, 0)


def _dispatch(tabs, xn, lpos_t, n_slots, nloc):
    n = xn.shape[0]
    t = min(T_TILE, n)
    grid_spec = pltpu.PrefetchScalarGridSpec(
        num_scalar_prefetch=6,
        grid=(n // t,),
        in_specs=[pl.BlockSpec((t, D_MODEL), lambda i, *_: (i, 0)),
                  pl.BlockSpec((TOP_K, t), lambda i, *_: (0, i))],
        out_specs=pl.BlockSpec(memory_space=pl.ANY),
        scratch_shapes=[pltpu.VMEM((2, nloc, D_MODEL), F32), pltpu.VMEM((TM_MOE, D_MODEL), F32),
                        pltpu.SemaphoreType.DMA((2,)), pltpu.SemaphoreType.DMA],
    )
    return pl.pallas_call(
        _dispatch_kernel,
        out_shape=jax.ShapeDtypeStruct((n_slots, D_MODEL), F32),
        grid_spec=grid_spec,
        compiler_params=pltpu.CompilerParams(dimension_semantics=("arbitrary",), vmem_limit_bytes=VMEM_LIMIT),
        name="dispatch",
    )(tabs["lo"], tabs["go"], tabs["n8"], tabs["nt"], tabs["zs"], tabs["zn"], xn, lpos_t)


def _moe_kernel(bexp_ref, nused_ref, nexp_ref, x_ref, wgu_hbm, bgu_ref, wd_hbm, bd_ref, o_ref,
                wgu_f, wd_f, wgu_s, wd_s, sem_w):
    b = pl.program_id(0)
    prev = bexp_ref[jnp.maximum(b - 1, 0)]
    fresh = (b == 0) | (bexp_ref[b] != prev)

    def fetch(ex):
        return (pltpu.make_async_copy(wgu_hbm.at[ex], wgu_f, sem_w.at[0]),
                pltpu.make_async_copy(wd_hbm.at[ex], wd_f, sem_w.at[1]))

    @pl.when(b == 0)
    def _():
        for cp in fetch(bexp_ref[0]):
            cp.start()

    @pl.when(fresh & (b < nused_ref[0]))
    def _():
        for cp in fetch(bexp_ref[b]):
            cp.wait()
        rows = 128
        for r0 in range(0, D_MODEL, rows):
            wgu_s[r0:r0 + rows, :] = wgu_f[r0:r0 + rows, :].astype(BF16)
        for r0 in range(0, D_FF, rows):
            wd_s[r0:r0 + rows, :] = wd_f[r0:r0 + rows, :].astype(BF16)

        @pl.when(nexp_ref[b] >= 0)
        def _():
            for cp in fetch(nexp_ref[b]):
                cp.start()

    @pl.when(b < nused_ref[0])
    def _():
        xb = x_ref[...].astype(BF16)
        gu = jnp.dot(xb, wgu_s[...], preferred_element_type=F32) + bgu_ref[0]
        g = jnp.minimum(gu[:, :D_FF], SWIGLU_LIMIT)
        u = jnp.clip(gu[:, D_FF:], -SWIGLU_LIMIT, SWIGLU_LIMIT)
        h = (u + 1.0) * (g * _sigmoid(SWIGLU_ALPHA * g))
        o_ref[...] = jnp.dot(h.astype(BF16), wd_s[...], preferred_element_type=F32) + bd_ref[0]

    @pl.when(b >= nused_ref[0])
    def _():
        o_ref[...] = jnp.zeros_like(o_ref)


def _moe(block_exp, n_used, next_exp, xs, w_gu, b_gu, w_down, b_down):
    n_slots = xs.shape[0]
    nb = n_slots // TM_MOE
    bspec = lambda shape: pl.BlockSpec(shape, lambda b, be, nu, ne: (be[b], 0, 0))
    grid_spec = pltpu.PrefetchScalarGridSpec(
        num_scalar_prefetch=3,
        grid=(nb,),
        in_specs=[
            pl.BlockSpec((TM_MOE, D_MODEL), lambda b, be, nu, ne: (jnp.minimum(b, nu[0] - 1), 0)),
            pl.BlockSpec(memory_space=pl.ANY), bspec((1, 1, 2 * D_FF)),
            pl.BlockSpec(memory_space=pl.ANY), bspec((1, 1, D_MODEL)),
        ],
        out_specs=pl.BlockSpec((TM_MOE, D_MODEL), lambda b, be, nu, ne: (b, 0)),
        scratch_shapes=[pltpu.VMEM((D_MODEL, 2 * D_FF), F32), pltpu.VMEM((D_FF, D_MODEL), F32),
                        pltpu.VMEM((D_MODEL, 2 * D_FF), BF16), pltpu.VMEM((D_FF, D_MODEL), BF16),
                        pltpu.SemaphoreType.DMA((2,))],
    )
    return pl.pallas_call(
        _moe_kernel,
        out_shape=jax.ShapeDtypeStruct((n_slots, D_MODEL), F32),
        grid_spec=grid_spec,
        compiler_params=pltpu.CompilerParams(dimension_semantics=("arbitrary",), vmem_limit_bytes=VMEM_LIMIT),
        name="moe",
    )(block_exp, n_used, next_exp, xs, w_gu, b_gu, w_down, b_down)


def _combine_kernel(lo_ref, go_ref, n8_ref, nt_ref, ys_hbm, h_ref, lpos_ref, gate_ref, g_ref, o_ref, loc, sem):
    i = pl.program_id(0)
    slot = lax.rem(i, 2)
    nloc, t = loc.shape[1], h_ref.shape[0]

    def row_copy(sl, lo, go, rows):
        return pltpu.make_async_copy(ys_hbm.at[pl.ds(go, rows), :], loc.at[sl, pl.ds(lo, rows), :], sem.at[sl])

    def fetch(tile, sl):
        _run_copies(tile, lo_ref, go_ref, n8_ref, lambda lo, go, rows: row_copy(sl, lo, go, rows))

        def zero_rows(j, carry):
            r0 = pl.multiple_of((nt_ref[tile] + j) * SUBLANES, SUBLANES)
            loc[sl, pl.ds(r0, SUBLANES), :] = jnp.zeros((SUBLANES, D_MODEL), F32)
            return carry

        lax.fori_loop(0, nloc // SUBLANES - nt_ref[tile], zero_rows, 0)

    @pl.when(i == 0)
    def _():
        fetch(0, 0)

    @pl.when(i + 1 < pl.num_programs(0))
    def _():
        fetch(i + 1, 1 - slot)

    liota = lax.broadcasted_iota(jnp.int32, (t, nloc), 1)
    lpos = lpos_ref[...]
    gates = gate_ref[...]
    wsel = jnp.zeros((t, nloc), F32)
    for k in range(TOP_K):
        wsel = jnp.where(liota == lpos[:, k:k + 1], gates[:, k:k + 1], wsel)

    _wait_rows(nt_ref[i], lambda rows: row_copy(slot, 0, 0, rows))
    acc = h_ref[...] + jnp.dot(wsel.astype(BF16), loc[slot].astype(BF16), preferred_element_type=F32)
    o_ref[...] = _rms(acc, g_ref[...])


def _combine(tabs, ys, h2, lpos_tok, gates_tok, g, nloc):
    n = h2.shape[0]
    t = min(T_TILE, n)
    grid_spec = pltpu.PrefetchScalarGridSpec(
        num_scalar_prefetch=4,
        grid=(n // t,),
        in_specs=[pl.BlockSpec(memory_space=pl.ANY),
                  pl.BlockSpec((t, D_MODEL), lambda i, *_: (i, 0)),
                  pl.BlockSpec((t, TOP_K), lambda i, *_: (i, 0)),
                  pl.BlockSpec((t, TOP_K), lambda i, *_: (i, 0)),
                  pl.BlockSpec((1, D_MODEL), lambda i, *_: (0, 0))],
        out_specs=pl.BlockSpec((t, D_MODEL), lambda i, *_: (i, 0)),
        scratch_shapes=[pltpu.VMEM((2, nloc, D_MODEL), F32), pltpu.SemaphoreType.DMA((2,))],
    )
    return pl.pallas_call(
        _combine_kernel,
        out_shape=jax.ShapeDtypeStruct((n, D_MODEL), F32),
        grid_spec=grid_spec,
        compiler_params=pltpu.CompilerParams(dimension_semantics=("arbitrary",), vmem_limit_bytes=VMEM_LIMIT),
        name="combine",
    )(tabs["lo"], tabs["go"], tabs["n8"], tabs["nt"], ys, h2, lpos_tok, gates_tok, g)


def _cmul(a, b):
    return a[0] * b[0] - a[1] * b[1], a[0] * b[1] + a[1] * b[0]


def _prep_mixer_params(sub_len, norm_mix_g, w_in, conv_w, conv_b, rg_wa, rg_ba, rg_wx, rg_bx, rg_lambda,
                       s5_lambda_re, s5_lambda_im, s5_b_re, s5_b_im, s5_c_re, s5_c_im, s5_d, s5_log_dt,
                       s5_glu_w, s5_glu_b, w_out):
    row = lambda v: v.reshape(1, -1).astype(F32)
    dt = jnp.exp(s5_log_dt)[:, None]
    mag = jnp.exp(dt * s5_lambda_re)
    abar_re = mag * jnp.cos(dt * s5_lambda_im)
    abar_im = mag * jnp.sin(dt * s5_lambda_im)
    den = s5_lambda_re * s5_lambda_re + s5_lambda_im * s5_lambda_im
    num_re = abar_re - 1.0
    coef_re = (num_re * s5_lambda_re + abar_im * s5_lambda_im) / den
    coef_im = (abar_im * s5_lambda_re - num_re * s5_lambda_im) / den
    bbar_re = coef_re[..., None] * s5_b_re - coef_im[..., None] * s5_b_im
    bbar_im = coef_re[..., None] * s5_b_im + coef_im[..., None] * s5_b_re
    a1 = (abar_re.reshape(-1), abar_im.reshape(-1))
    aq = a1
    for _ in range(int(math.log2(sub_len))):
        aq = _cmul(aq, aq)
    pw = [aq]
    for _ in range(SUBLANES - 1):
        pw.append(_cmul(pw[-1], aq))
    sub = jnp.arange(SUBLANES)[:, None]
    tabs = []
    for d in (1, 2, 4):
        keep = (sub >= d).astype(F32)
        tabs += [keep * pw[d - 1][0][None, :], keep * pw[d - 1][1][None, :]]
    tabs += [jnp.stack([p_[0] for p_ in pw]), jnp.stack([p_[1] for p_ in pw])]
    tabs += [jnp.broadcast_to(a1[0][None, :], (SUBLANES, N_STATE)),
             jnp.broadcast_to(a1[1][None, :], (SUBLANES, N_STATE))]
    s5tab = jnp.stack(tabs).astype(F32)
    gh = S5_GROUPS // 2
    bmat, cmat = [], []
    for hf in range(2):
        gs = slice(hf * gh, (hf + 1) * gh)
        b_re = _block_diag(jnp.transpose(bbar_re[gs], (0, 2, 1)))
        b_im = _block_diag(jnp.transpose(bbar_im[gs], (0, 2, 1)))
        bmat.append(jnp.concatenate([b_re, b_im], axis=1))
        cr = _block_diag(jnp.transpose(s5_c_re[gs], (0, 2, 1)))
        ci = _block_diag(jnp.transpose(s5_c_im[gs], (0, 2, 1)))
        cmat.append(jnp.concatenate([cr, -ci], axis=0))
    return {
        "g_mix": row(norm_mix_g), "w_in": w_in.astype(BF16), "conv_w": conv_w.astype(F32), "conv_b": row(conv_b),
        "w_gates": jnp.concatenate([_block_diag(rg_wa), _block_diag(rg_wx)], axis=1).astype(BF16),
        "b_gates": jnp.concatenate([rg_ba, rg_bx]).reshape(1, -1).astype(F32),
        "sp": row(jax.nn.softplus(-rg_lambda)),
        "bmat": jnp.stack(bmat).astype(BF16), "s5tab": s5tab, "cmat": jnp.stack(cmat).astype(BF16),
        "d_skip": row(s5_d), "glu_w": _block_diag(s5_glu_w).astype(BF16), "glu_b": row(s5_glu_b),
        "w_out": w_out.astype(BF16),
    }


def _slot_tables(tcnt, n_slots):
    i32 = lambda v: v.astype(jnp.int32)
    run = ((tcnt + SUBLANES - 1) // SUBLANES) * SUBLANES
    rows = jnp.sum(run, axis=0)
    blocks = (rows + TM_MOE - 1) // TM_MOE
    bend = jnp.cumsum(blocks)
    estart = (bend - blocks) * TM_MOE
    nb = n_slots // TM_MOE
    n_used = i32(bend[-1])
    block_exp = jnp.minimum(jnp.sum(jnp.arange(nb)[:, None] >= bend[None, :], axis=1), N_EXPERTS - 1)
    block_exp = jnp.where(jnp.arange(nb) < n_used, block_exp, block_exp[jnp.maximum(n_used - 1, 0)])
    eidx = jnp.arange(N_EXPERTS)
    later = (eidx[None, :] > eidx[:, None]) & (blocks[None, :] > 0)
    next_of = jnp.min(jnp.where(later, eidx[None, :], N_EXPERTS), axis=1)
    next_exp = jnp.where(next_of < N_EXPERTS, next_of, -1)[block_exp]
    tabs = {
        "lo": i32(jnp.cumsum(run, axis=1) - run).reshape(-1),
        "go": i32(estart[None, :] + jnp.cumsum(run, axis=0) - run).reshape(-1),
        "n8": i32(run // SUBLANES).reshape(-1),
        "nt": i32(jnp.sum(run, axis=1) // SUBLANES),
        "zs": i32(jnp.concatenate([estart + rows, bend[-1:] * TM_MOE])),
        "zn": i32(jnp.concatenate([(blocks * TM_MOE - rows) // SUBLANES, nb - bend[-1:]])),
    }
    return i32(block_exp), n_used.reshape(1), i32(next_exp), tabs


def kernel(x, mem, norm_mix_g, w_in, conv_w, conv_b, rg_wa, rg_ba, rg_wx, rg_bx, rg_lambda, s5_lambda_re,
           s5_lambda_im, s5_b_re, s5_b_im, s5_c_re, s5_c_im, s5_d, s5_log_dt, s5_glu_w, s5_glu_b, w_out,
           norm_xa_g, mem_norm_g, xa_wq, xa_wkv, xa_wo, norm_ffn_g, router_w, router_b, exp_w_gu, exp_b_gu,
           exp_w_down, exp_b_down, final_norm_g):
    bsz, seq, d = x.shape
    mlen = mem.shape[1]
    n = bsz * seq
    row = lambda v: v.reshape(1, -1).astype(F32)

    p = _prep_mixer_params(min(T_MIX, seq) // SUBLANES, norm_mix_g, w_in, conv_w, conv_b, rg_wa, rg_ba, rg_wx,
                           rg_bx, rg_lambda, s5_lambda_re, s5_lambda_im, s5_b_re, s5_b_im, s5_c_re, s5_c_im,
                           s5_d, s5_log_dt, s5_glu_w, s5_glu_b, w_out)
    h1 = _mixer(x.reshape(n, d), bsz, seq, p)

    k, v = _kv(mem.reshape(bsz * mlen, d), bsz, mlen, row(mem_norm_g), xa_wkv.astype(BF16))
    h2 = _attn(h1, bsz, seq, mlen, row(norm_xa_g), xa_wq.astype(BF16), k, v, xa_wo.astype(BF16))

    xn, lpos_t, gate_t, tcnt = _route(h2, row(norm_ffn_g), router_w.T.astype(F32),
                                      router_b.reshape(-1, 1).astype(F32))
    tile = min(T_TILE, n)
    n_tiles = n // tile
    run_pad = N_EXPERTS * (SUBLANES - 1)
    nloc = -(-(TOP_K * tile + run_pad) // 128) * 128
    n_slots = -(-(TOP_K * n + n_tiles * run_pad + N_EXPERTS * (TM_MOE - SUBLANES)) // TM_MOE) * TM_MOE
    block_exp, n_used, next_exp, tabs = _slot_tables(tcnt[:, :, 0], n_slots)
    xs = _dispatch(tabs, xn, lpos_t, n_slots, nloc)
    ys = _moe(block_exp, n_used, next_exp, xs, exp_w_gu, exp_b_gu.reshape(N_EXPERTS, 1, -1),
              exp_w_down, exp_b_down.reshape(N_EXPERTS, 1, -1))
    out = _combine(tabs, ys, h2, lpos_t.T, gate_t.T, row(final_norm_g), nloc)
    return out.reshape(bsz, seq, d)
```

```python
import math

import jax
import jax.numpy as jnp
from jax import lax
from jax.experimental import pallas as pl
from jax.experimental.pallas import tpu as pltpu

F32 = jnp.float32
BF16 = jnp.bfloat16

D_MODEL = 1024
D_RG = 512
D_S5 = 512
CONV_W = 4
RG_C = 8.0
S5_GROUPS = 32
S5_STATE = 64
N_STATE = S5_GROUPS * S5_STATE
XA_HEADS = 4
XA_HD = 256
N_EXPERTS = 32
TOP_K = 4
D_FF = 1024
SWIGLU_LIMIT = 7.0
SWIGLU_ALPHA = 1.702
NORM_EPS = 1e-6

SUBLANES = 8
T_MIX = 512
T_ATT = 512
T_TILE = 512
TM_MOE = 256
VMEM_LIMIT = 56 * 1024 * 1024


def _rms(x, g):
    return x * lax.rsqrt(jnp.mean(x * x, axis=-1, keepdims=True) + NORM_EPS) * g


def _gelu_tanh(x):
    return 0.5 * x * (1.0 + jnp.tanh(math.sqrt(2.0 / math.pi) * (x + 0.044715 * (x * x * x))))


def _sigmoid(x):
    return 1.0 / (1.0 + jnp.exp(-x))


def _block_diag(blocks):
    n, r, c = blocks.shape
    eye = jnp.eye(n, dtype=blocks.dtype)
    return jnp.einsum('nrc,nm->nrmc', blocks, eye).reshape(n * r, n * c)


def _sublane_chain(al, e, carry, sub):
    for d in (1, 2, 4):
        al_sh = jnp.where(sub >= d, pltpu.roll(al, d, axis=0), 1.0)
        e_sh = jnp.where(sub >= d, pltpu.roll(e, d, axis=0), 0.0)
        e = al * e_sh + e
        al = al * al_sh
    s_end = e + al * carry
    start = jnp.where(sub == 0, carry, pltpu.roll(s_end, 1, axis=0))
    return start, s_end[SUBLANES - 1:SUBLANES, :]


def _mixer_kernel(x_ref, g_ref, win_ref, convw_ref, convb_ref, wg_ref, bg_ref, sp_ref,
                  bmat_ref, tab_ref, cmat_ref, dskip_ref, glu_ref, glub_ref, wout_ref,
                  o_ref,
                  xin, xout, sem_i, sem_o, xbuf, hist, a_s, b_s, rg_carry, sre, sim, c_re, c_im):
    ngrp = xin.shape[1]
    t = ngrp * SUBLANES
    nh = (CONV_W - 1) * SUBLANES
    c = pl.program_id(1)
    step = pl.program_id(0) * pl.num_programs(1) + c
    nsteps = pl.num_programs(0) * pl.num_programs(1)
    slot = lax.rem(step, 2)

    def in_copy(s, sl, i):
        return pltpu.make_async_copy(x_ref.at[pl.ds(s * t + i * ngrp, ngrp), :], xin.at[sl, :, i, :], sem_i.at[sl])

    def out_copy(s, sl, i):
        return pltpu.make_async_copy(xout.at[sl, :, i, :], o_ref.at[pl.ds(s * t + i * ngrp, ngrp), :], sem_o.at[sl])

    @pl.when(step == 0)
    def _():
        for i in range(SUBLANES):
            in_copy(0, 0, i).start()

    @pl.when(c == 0)
    def _():
        hist[...] = jnp.zeros_like(hist)
        rg_carry[...] = jnp.zeros_like(rg_carry)
        c_re[...] = jnp.zeros_like(c_re)
        c_im[...] = jnp.zeros_like(c_im)

    for i in range(SUBLANES):
        in_copy(step, slot, i).wait()

    @pl.when(step + 1 < nsteps)
    def _():
        for i in range(SUBLANES):
            in_copy(step + 1, 1 - slot, i).start()

    x = xin[slot].reshape(t, D_MODEL)
    xn = _rms(x, g_ref[...]).astype(BF16)
    proj = jnp.dot(xn, win_ref[...], preferred_element_type=F32)
    gate_rg = proj[:, :D_RG]
    x_rg = proj[:, D_RG:2 * D_RG]
    u = proj[:, 2 * D_RG:]

    sub = lax.broadcasted_iota(jnp.int32, (SUBLANES, D_RG), 0)
    cur_tail = x_rg[t - nh:, :]
    for gidx in range(CONV_W - 1):
        rs = slice(gidx * SUBLANES, (gidx + 1) * SUBLANES)
        xbuf[rs, :] = jnp.where(sub == 0, pltpu.roll(hist[rs, :], 1, axis=0), pltpu.roll(cur_tail[rs, :], 1, axis=0))
    hist[...] = cur_tail
    xbuf[nh:, :] = x_rg
    xc = convb_ref[...] + convw_ref[CONV_W - 1:CONV_W, :] * x_rg
    for k in range(CONV_W - 1):
        xc = xc + convw_ref[k:k + 1, :] * xbuf[k * SUBLANES:k * SUBLANES + t, :]
    gates = jnp.dot(xc.astype(BF16), wg_ref[...], preferred_element_type=F32) + bg_ref[...]
    r = _sigmoid(gates[:, :D_RG])
    ig = _sigmoid(gates[:, D_RG:])
    log_a = (-RG_C) * r * sp_ref[...]
    a = jnp.exp(log_a)
    mult = jnp.sqrt(1.0 - a * a)
    row = lax.broadcasted_iota(jnp.int32, (t, D_RG), 0)
    mult = jnp.where((row == 0) & (c == 0), 1.0, mult)
    a_s[...] = a
    b_s[...] = mult * (ig * xc)

    def rg_pass(store):
        def body(q, carry):
            h, ac = carry
            r0 = pl.multiple_of(q * SUBLANES, SUBLANES)
            aa = a_s[pl.ds(r0, SUBLANES), :]
            h = aa * h + b_s[pl.ds(r0, SUBLANES), :]
            if store:
                b_s[pl.ds(r0, SUBLANES), :] = h
                return h, ac
            return h, aa * ac
        return body

    zeros8 = jnp.zeros((SUBLANES, D_RG), F32)
    ones8 = jnp.ones((SUBLANES, D_RG), F32)
    e_loc, a_loc = lax.fori_loop(0, ngrp, rg_pass(False), (zeros8, ones8), unroll=4)
    start, last = _sublane_chain(a_loc, e_loc, rg_carry[...], sub)
    rg_carry[...] = last
    lax.fori_loop(0, ngrp, rg_pass(True), (start, ones8), unroll=4)
    y_rg = b_s[...] * _gelu_tanh(gate_rg)

    ub = u.astype(BF16)
    half = N_STATE // 2
    for hf in range(2):
        bu = jnp.dot(ub[:, hf * 256:(hf + 1) * 256], bmat_ref[hf], preferred_element_type=F32)
        sre[:, hf * half:(hf + 1) * half] = bu[:, :half]
        sim[:, hf * half:(hf + 1) * half] = bu[:, half:]

    lb = 1024
    subl = lax.broadcasted_iota(jnp.int32, (SUBLANES, lb), 0)
    for blk in range(N_STATE // lb):
        cs = slice(blk * lb, (blk + 1) * lb)
        ar = tab_ref[8, :, cs]
        ai = tab_ref[9, :, cs]

        def s5_pass(store, cs=cs, ar=ar, ai=ai):
            def body(q, carry):
                hr, hi = carry
                r0 = pl.multiple_of(q * SUBLANES, SUBLANES)
                nr = ar * hr - ai * hi + sre[pl.ds(r0, SUBLANES), cs]
                ni = ar * hi + ai * hr + sim[pl.ds(r0, SUBLANES), cs]
                if store:
                    sre[pl.ds(r0, SUBLANES), cs] = nr
                    sim[pl.ds(r0, SUBLANES), cs] = ni
                return nr, ni
            return body

        z8 = jnp.zeros((SUBLANES, lb), F32)
        er, ei = lax.fori_loop(0, ngrp, s5_pass(False), (z8, z8), unroll=4)
        for i, d in enumerate((1, 2, 4)):
            mr = tab_ref[2 * i, :, cs]
            mi = tab_ref[2 * i + 1, :, cs]
            rr = pltpu.roll(er, d, axis=0)
            ri = pltpu.roll(ei, d, axis=0)
            er, ei = er + (mr * rr - mi * ri), ei + (mr * ri + mi * rr)
        pr = tab_ref[6, :, cs]
        pi_ = tab_ref[7, :, cs]
        cr = c_re[:, cs]
        ci = c_im[:, cs]
        sr = er + (pr * cr - pi_ * ci)
        si = ei + (pr * ci + pi_ * cr)
        c_re[:, cs] = sr[SUBLANES - 1:SUBLANES, :]
        c_im[:, cs] = si[SUBLANES - 1:SUBLANES, :]
        st_r = jnp.where(subl == 0, cr, pltpu.roll(sr, 1, axis=0))
        st_i = jnp.where(subl == 0, ci, pltpu.roll(si, 1, axis=0))
        lax.fori_loop(0, ngrp, s5_pass(True), (st_r, st_i), unroll=4)

    ys = []
    for hf in range(2):
        hcat = jnp.concatenate([sre[:, hf * half:(hf + 1) * half], sim[:, hf * half:(hf + 1) * half]], axis=1)
        ys.append(jnp.dot(hcat.astype(BF16), cmat_ref[hf], preferred_element_type=F32))
    y = jnp.concatenate(ys, axis=1) + dskip_ref[...] * u
    z = _gelu_tanh(y)
    zg = _sigmoid(jnp.dot(z.astype(BF16), glu_ref[...], preferred_element_type=F32) + glub_ref[...])
    y_s5 = z * zg

    ycat = jnp.concatenate([y_rg, y_s5], axis=1).astype(BF16)
    res = x + jnp.dot(ycat, wout_ref[...], preferred_element_type=F32)

    @pl.when(step >= 2)
    def _():
        for i in range(SUBLANES):
            out_copy(step - 2, slot, i).wait()

    xout[slot] = res.reshape(ngrp, SUBLANES, D_MODEL)
    for i in range(SUBLANES):
        out_copy(step, slot, i).start()

    @pl.when(step == nsteps - 1)
    def _():
        for i in range(SUBLANES):
            out_copy(step, slot, i).wait()

        @pl.when(step >= 1)
        def _():
            for i in range(SUBLANES):
                out_copy(step - 1, 1 - slot, i).wait()


def _mixer(x2, bsz, seq, p):
    t = min(T_MIX, seq)
    nc = seq // t
    nh = (CONV_W - 1) * SUBLANES
    const = lambda shape: pl.BlockSpec(shape, lambda b, c: (0,) * len(shape))
    return pl.pallas_call(
        _mixer_kernel,
        out_shape=jax.ShapeDtypeStruct(x2.shape, F32),
        grid=(bsz, nc),
        in_specs=[
            pl.BlockSpec(memory_space=pl.ANY),
            const((1, D_MODEL)), const((D_MODEL, 3 * D_RG)), const((CONV_W, D_RG)), const((1, D_RG)),
            const((D_RG, 2 * D_RG)), const((1, 2 * D_RG)), const((1, D_RG)),
            const((2, 256, N_STATE)), const((10, SUBLANES, N_STATE)), const((2, N_STATE, 256)),
            const((1, D_S5)), const((D_S5, D_S5)), const((1, D_S5)), const((D_MODEL, D_MODEL)),
        ],
        out_specs=pl.BlockSpec(memory_space=pl.ANY),
        scratch_shapes=[
            pltpu.VMEM((2, t // SUBLANES, SUBLANES, D_MODEL), F32),
            pltpu.VMEM((2, t // SUBLANES, SUBLANES, D_MODEL), F32),
            pltpu.SemaphoreType.DMA((2,)), pltpu.SemaphoreType.DMA((2,)),
            pltpu.VMEM((t + nh, D_RG), F32), pltpu.VMEM((nh, D_RG), F32),
            pltpu.VMEM((t, D_RG), F32), pltpu.VMEM((t, D_RG), F32), pltpu.VMEM((1, D_RG), F32),
            pltpu.VMEM((t, N_STATE), F32), pltpu.VMEM((t, N_STATE), F32),
            pltpu.VMEM((1, N_STATE), F32), pltpu.VMEM((1, N_STATE), F32),
        ],
        compiler_params=pltpu.CompilerParams(
            dimension_semantics=("arbitrary", "arbitrary"), vmem_limit_bytes=VMEM_LIMIT),
        name="mixer",
    )(x2, p["g_mix"], p["w_in"], p["conv_w"], p["conv_b"], p["w_gates"], p["b_gates"], p["sp"],
      p["bmat"], p["s5tab"], p["cmat"], p["d_skip"], p["glu_w"], p["glu_b"], p["w_out"])


def _kv_kernel(m_ref, g_ref, w_ref, k_ref, v_ref):
    mn = _rms(m_ref[...], g_ref[...]).astype(BF16)
    kv = jnp.dot(mn, w_ref[...], preferred_element_type=F32)
    k_ref[...] = kv[:, :D_MODEL].astype(BF16)
    v_ref[...] = kv[:, D_MODEL:].astype(BF16)


def _kv(mem2, bsz, mlen, g, wkv):
    return pl.pallas_call(
        _kv_kernel,
        out_shape=(jax.ShapeDtypeStruct(mem2.shape, BF16), jax.ShapeDtypeStruct(mem2.shape, BF16)),
        grid=(bsz,),
        in_specs=[pl.BlockSpec((mlen, D_MODEL), lambda b: (b, 0)),
                  pl.BlockSpec((1, D_MODEL), lambda b: (0, 0)),
                  pl.BlockSpec((D_MODEL, 2 * D_MODEL), lambda b: (0, 0))],
        out_specs=(pl.BlockSpec((mlen, D_MODEL), lambda b: (b, 0)),
                   pl.BlockSpec((mlen, D_MODEL), lambda b: (b, 0))),
        compiler_params=pltpu.CompilerParams(dimension_semantics=("arbitrary",), vmem_limit_bytes=VMEM_LIMIT),
        name="kv",
    )(mem2, g, wkv)


def _attn_kernel(h_ref, g_ref, wq_ref, k_ref, v_ref, wo_ref, o_ref):
    h = h_ref[...]
    xn = _rms(h, g_ref[...]).astype(BF16)
    q = jnp.dot(xn, wq_ref[...], preferred_element_type=F32).astype(BF16)
    outs = []
    for hd in range(XA_HEADS):
        cs = slice(hd * XA_HD, (hd + 1) * XA_HD)
        s = lax.dot_general(q[:, cs], k_ref[:, cs], (((1,), (1,)), ((), ())),
                            preferred_element_type=F32) * (XA_HD ** -0.5)
        s = s - jnp.max(s, axis=-1, keepdims=True)
        e = jnp.exp(s)
        pr = e / jnp.sum(e, axis=-1, keepdims=True)
        outs.append(jnp.dot(pr.astype(BF16), v_ref[:, cs], preferred_element_type=F32))
    o = jnp.concatenate(outs, axis=1).astype(BF16)
    o_ref[...] = h + jnp.dot(o, wo_ref[...], preferred_element_type=F32)


def _attn(h2, bsz, seq, mlen, g, wq, k, v, wo):
    t = min(T_ATT, seq)
    nc = seq // t
    return pl.pallas_call(
        _attn_kernel,
        out_shape=jax.ShapeDtypeStruct(h2.shape, F32),
        grid=(bsz, nc),
        in_specs=[pl.BlockSpec((t, D_MODEL), lambda b, c: (b * nc + c, 0)),
                  pl.BlockSpec((1, D_MODEL), lambda b, c: (0, 0)),
                  pl.BlockSpec((D_MODEL, D_MODEL), lambda b, c: (0, 0)),
                  pl.BlockSpec((mlen, D_MODEL), lambda b, c: (b, 0)),
                  pl.BlockSpec((mlen, D_MODEL), lambda b, c: (b, 0)),
                  pl.BlockSpec((D_MODEL, D_MODEL), lambda b, c: (0, 0))],
        out_specs=pl.BlockSpec((t, D_MODEL), lambda b, c: (b * nc + c, 0)),
        compiler_params=pltpu.CompilerParams(
            dimension_semantics=("arbitrary", "arbitrary"), vmem_limit_bytes=VMEM_LIMIT),
        name="attn",
    )(h2, g, wq, k, v, wo)


def _route_kernel(h_ref, g_ref, wr_ref, br_ref, xn_ref, lpos_ref, gate_ref, tcnt_ref):
    t = h_ref.shape[0]
    xn = _rms(h_ref[...], g_ref[...])
    x_hi = xn.astype(BF16)
    x_lo = (xn - x_hi.astype(F32)).astype(BF16)
    xn_ref[...] = x_hi
    dims = (((1,), (1,)), ((), ()))
    p_hi = lax.dot_general(wr_ref[...], x_hi, dims, preferred_element_type=F32)
    p_lo = lax.dot_general(wr_ref[0:N_EXPERTS, :], x_lo, dims, preferred_element_type=F32)
    lg = p_hi[0:N_EXPERTS, :] + (p_hi[N_EXPERTS:, :] + p_lo) + br_ref[...]
    erow = lax.broadcasted_iota(jnp.int32, (N_EXPERTS, t), 0)
    vals, hots = [], []
    for _ in range(TOP_K):
        m = jnp.max(lg, axis=0, keepdims=True)
        ix = jnp.min(jnp.where(lg == m, erow, N_EXPERTS), axis=0, keepdims=True)
        hot = erow == ix
        lg = jnp.where(hot, -jnp.inf, lg)
        vals.append(m)
        hots.append(hot)
    es = [jnp.exp(v - vals[0]) for v in vals]
    den = es[0] + es[1] + es[2] + es[3]
    sel = jnp.zeros((N_EXPERTS, t), F32)
    for hot in hots:
        sel = sel + jnp.where(hot, 1.0, 0.0)
    tri = jnp.where(lax.broadcasted_iota(jnp.int32, (t, t), 0) < lax.broadcasted_iota(jnp.int32, (t, t), 1),
                    1.0, 0.0).astype(BF16)
    rank = jnp.dot(sel.astype(BF16), tri, preferred_element_type=F32)
    cnt = jnp.sum(sel, axis=1, keepdims=True)
    run8 = jnp.floor((cnt + (SUBLANES - 1.0)) * (1.0 / SUBLANES))
    below = jnp.where(lax.broadcasted_iota(jnp.int32, (N_EXPERTS, N_EXPERTS), 1)
                      < lax.broadcasted_iota(jnp.int32, (N_EXPERTS, N_EXPERTS), 0), 1.0, 0.0).astype(BF16)
    start8 = jnp.dot(below, jnp.broadcast_to(run8, (N_EXPERTS, 128)).astype(BF16), preferred_element_type=F32)
    pos = rank + SUBLANES * start8[:, 0:1]
    for k in range(TOP_K):
        gate_ref[k:k + 1, :] = es[k] / den
        lpos_ref[k:k + 1, :] = jnp.sum(jnp.where(hots[k], pos, 0.0), axis=0, keepdims=True).astype(jnp.int32)
    tcnt_ref[0] = jnp.broadcast_to(cnt, (N_EXPERTS, 128)).astype(jnp.int32)


def _route(h2, g, wr_t, br):
    n = h2.shape[0]
    t = min(T_TILE, n)
    return pl.pallas_call(
        _route_kernel,
        out_shape=(jax.ShapeDtypeStruct((n, D_MODEL), BF16),
                   jax.ShapeDtypeStruct((TOP_K, n), jnp.int32),
                   jax.ShapeDtypeStruct((TOP_K, n), F32),
                   jax.ShapeDtypeStruct((n // t, N_EXPERTS, 128), jnp.int32)),
        grid=(n // t,),
        in_specs=[pl.BlockSpec((t, D_MODEL), lambda i: (i, 0)),
                  pl.BlockSpec((1, D_MODEL), lambda i: (0, 0)),
                  pl.BlockSpec((2 * N_EXPERTS, D_MODEL), lambda i: (0, 0)),
                  pl.BlockSpec((N_EXPERTS, 1), lambda i: (0, 0))],
        out_specs=(pl.BlockSpec((t, D_MODEL), lambda i: (i, 0)),
                   pl.BlockSpec((TOP_K, t), lambda i: (0, i)),
                   pl.BlockSpec((TOP_K, t), lambda i: (0, i)),
                   pl.BlockSpec((1, N_EXPERTS, 128), lambda i: (i, 0, 0))),
        compiler_params=pltpu.CompilerParams(dimension_semantics=("arbitrary",), vmem_limit_bytes=VMEM_LIMIT),
        name="route",
    )(h2, g, wr_t, br)


RUN_CHUNK = 64
WAIT_CHUNK = 256


def _run_copies(i, lo_ref, go_ref, n8_ref, make_copy):
    def expert(e, carry):
        idx = i * N_EXPERTS + e
        lo = lo_ref[idx]
        go = go_ref[idx]
        n8 = n8_ref[idx]
        nbig = lax.shift_right_logical(n8, 3)

        def big(j, c):
            make_copy(pl.multiple_of(lo + j * RUN_CHUNK, SUBLANES), pl.multiple_of(go + j * RUN_CHUNK, SUBLANES),
                      RUN_CHUNK).start()
            return c

        lax.fori_loop(0, nbig, big, 0)
        off = nbig * RUN_CHUNK
        for rows in (32, 16, 8):
            has = (n8 & (rows // SUBLANES)) != 0

            @pl.when(has)
            def _(off=off, rows=rows):
                make_copy(pl.multiple_of(lo + off, SUBLANES), pl.multiple_of(go + off, SUBLANES), rows).start()

            off = off + jnp.where(has, rows, 0)
        return carry

    lax.fori_loop(0, N_EXPERTS, expert, 0)


def _wait_rows(n8, make_wait):
    def big(j, c):
        make_wait(WAIT_CHUNK).wait()
        return c

    lax.fori_loop(0, lax.shift_right_logical(n8, 5), big, 0)
    for rows in (128, 64, 32, 16, 8):
        @pl.when((n8 & (rows // SUBLANES)) != 0)
        def _(rows=rows):
            make_wait(rows).wait()


def _dispatch_kernel(lo_ref, go_ref, n8_ref, nt_ref, zs_ref, zn_ref, xn_ref, lpos_ref, xs_hbm, loc, zbuf, sem,
                     sem_z):
    i = pl.program_id(0)
    last = pl.num_programs(0) - 1
    slot = lax.rem(i, 2)
    nloc, t = loc.shape[1], xn_ref.shape[0]

    def row_copy(sl, lo, go, rows):
        return pltpu.make_async_copy(loc.at[sl, pl.ds(lo, rows), :], xs_hbm.at[pl.ds(go, rows), :], sem.at[sl])

    def wait_copies(sl, count):
        _wait_rows(count, lambda rows: row_copy(sl, 0, 0, rows))

    @pl.when(i >= 2)
    def _():
        wait_copies(slot, nt_ref[jnp.maximum(i - 2, 0)])

    mc = 256
    liota = lax.broadcasted_iota(jnp.int32, (mc, t), 0)
    xn = xn_ref[...]
    for r in range(nloc // mc):
        pick = jnp.zeros((mc, t), F32)
        for k in range(TOP_K):
            pick = jnp.where(liota == lpos_ref[k:k + 1, :] - r * mc, 1.0, pick)
        loc[slot, r * mc:(r + 1) * mc, :] = jnp.dot(pick.astype(BF16), xn, preferred_element_type=F32)
    _run_copies(i, lo_ref, go_ref, n8_ref, lambda lo, go, rows: row_copy(slot, lo, go, rows))

    @pl.when(i == last)
    def _():
        wait_copies(slot, nt_ref[i])

        @pl.when(i >= 1)
        def _():
            wait_copies(1 - slot, nt_ref[jnp.maximum(i - 1, 0)])

        zbuf[...] = jnp.zeros_like(zbuf)

        def zero_copy(go):
            return pltpu.make_async_copy(zbuf.at[pl.ds(0, SUBLANES), :], xs_hbm.at[pl.ds(go, SUBLANES), :], sem_z)

        def expert(e, total):
            def chunk(j, carry):
                zero_copy(pl.multiple_of(zs_ref[e] + j * SUBLANES, SUBLANES)).start()
                return carry
            lax.fori_loop(0, zn_ref[e], chunk, 0)
            return total + zn_ref[e]

        total = lax.fori_loop(0, N_EXPERTS, expert, 0)

        def wait_zero(j, carry):
            zero_copy(0).wait()
            return carry

        lax.fori_loop(0, total, wait_zero, 0)

        def block_copy(go):
            return pltpu.make_async_copy(zbuf, xs_hbm.at[pl.ds(go, TM_MOE), :], sem_z)

        def start_block(j, carry):
            block_copy(pl.multiple_of(zs_ref[N_EXPERTS] + j * TM_MOE, TM_MOE)).start()
            return carry

        def wait_block(j, carry):
            block_copy(0).wait()
            return carry

        lax.fori_loop(0, zn_ref[N_EXPERTS], start_block, 0)
        lax.fori_loop(0, zn_ref[N_EXPERTS], wait_block, 0)


def _dispatch(tabs, xn, lpos_t, n_slots, nloc):
    n = xn.shape[0]
    t = min(T_TILE, n)
    grid_spec = pltpu.PrefetchScalarGridSpec(
        num_scalar_prefetch=6,
        grid=(n // t,),
        in_specs=[pl.BlockSpec((t, D_MODEL), lambda i, *_: (i, 0)),
                  pl.BlockSpec((TOP_K, t), lambda i, *_: (0, i))],
        out_specs=pl.BlockSpec(memory_space=pl.ANY),
        scratch_shapes=[pltpu.VMEM((2, nloc, D_MODEL), F32), pltpu.VMEM((TM_MOE, D_MODEL), F32),
                        pltpu.SemaphoreType.DMA((2,)), pltpu.SemaphoreType.DMA],
    )
    return pl.pallas_call(
        _dispatch_kernel,
        out_shape=jax.ShapeDtypeStruct((n_slots, D_MODEL), F32),
        grid_spec=grid_spec,
        compiler_params=pltpu.CompilerParams(dimension_semantics=("arbitrary",), vmem_limit_bytes=VMEM_LIMIT),
        name="dispatch",
    )(tabs["lo"], tabs["go"], tabs["n8"], tabs["nt"], tabs["zs"], tabs["zn"], xn, lpos_t)


def _moe_kernel(bexp_ref, nused_ref, nexp_ref, x_ref, wgu_hbm, bgu_ref, wd_hbm, bd_ref, o_ref,
                wgu_f, wd_f, wgu_s, wd_s, sem_w):
    b = pl.program_id(0)
    prev = bexp_ref[jnp.maximum(b - 1, 0)]
    fresh = (b == 0) | (bexp_ref[b] != prev)

    def fetch(ex):
        return (pltpu.make_async_copy(wgu_hbm.at[ex], wgu_f, sem_w.at[0]),
                pltpu.make_async_copy(wd_hbm.at[ex], wd_f, sem_w.at[1]))

    @pl.when(b == 0)
    def _():
        for cp in fetch(bexp_ref[0]):
            cp.start()

    @pl.when(fresh & (b < nused_ref[0]))
    def _():
        for cp in fetch(bexp_ref[b]):
            cp.wait()
        rows = 128
        for r0 in range(0, D_MODEL, rows):
            wgu_s[r0:r0 + rows, :] = wgu_f[r0:r0 + rows, :].astype(BF16)
        for r0 in range(0, D_FF, rows):
            wd_s[r0:r0 + rows, :] = wd_f[r0:r0 + rows, :].astype(BF16)

        @pl.when(nexp_ref[b] >= 0)
        def _():
            for cp in fetch(nexp_ref[b]):
                cp.start()

    @pl.when(b < nused_ref[0])
    def _():
        xb = x_ref[...].astype(BF16)
        gu = jnp.dot(xb, wgu_s[...], preferred_element_type=F32) + bgu_ref[0]
        g = jnp.minimum(gu[:, :D_FF], SWIGLU_LIMIT)
        u = jnp.clip(gu[:, D_FF:], -SWIGLU_LIMIT, SWIGLU_LIMIT)
        h = (u + 1.0) * (g * _sigmoid(SWIGLU_ALPHA * g))
        o_ref[...] = jnp.dot(h.astype(BF16), wd_s[...], preferred_element_type=F32) + bd_ref[0]

    @pl.when(b >= nused_ref[0])
    def _():
        o_ref[...] = jnp.zeros_like(o_ref)


def _moe(block_exp, n_used, next_exp, xs, w_gu, b_gu, w_down, b_down):
    n_slots = xs.shape[0]
    nb = n_slots // TM_MOE
    bspec = lambda shape: pl.BlockSpec(shape, lambda b, be, nu, ne: (be[b], 0, 0))
    grid_spec = pltpu.PrefetchScalarGridSpec(
        num_scalar_prefetch=3,
        grid=(nb,),
        in_specs=[
            pl.BlockSpec((TM_MOE, D_MODEL), lambda b, be, nu, ne: (jnp.minimum(b, nu[0] - 1), 0)),
            pl.BlockSpec(memory_space=pl.ANY), bspec((1, 1, 2 * D_FF)),
            pl.BlockSpec(memory_space=pl.ANY), bspec((1, 1, D_MODEL)),
        ],
        out_specs=pl.BlockSpec((TM_MOE, D_MODEL), lambda b, be, nu, ne: (b, 0)),
        scratch_shapes=[pltpu.VMEM((D_MODEL, 2 * D_FF), F32), pltpu.VMEM((D_FF, D_MODEL), F32),
                        pltpu.VMEM((D_MODEL, 2 * D_FF), BF16), pltpu.VMEM((D_FF, D_MODEL), BF16),
                        pltpu.SemaphoreType.DMA((2,))],
    )
    return pl.pallas_call(
        _moe_kernel,
        out_shape=jax.ShapeDtypeStruct((n_slots, D_MODEL), F32),
        grid_spec=grid_spec,
        compiler_params=pltpu.CompilerParams(dimension_semantics=("arbitrary",), vmem_limit_bytes=VMEM_LIMIT),
        name="moe",
    )(block_exp, n_used, next_exp, xs, w_gu, b_gu, w_down, b_down)


def _combine_kernel(lo_ref, go_ref, n8_ref, nt_ref, ys_hbm, h_ref, lpos_ref, gate_ref, g_ref, o_ref, loc, lpb,
                    gtb, sem):
    i = pl.program_id(0)
    slot = lax.rem(i, 2)
    nloc, t = loc.shape[1], h_ref.shape[0]

    def row_copy(sl, lo, go, rows):
        return pltpu.make_async_copy(ys_hbm.at[pl.ds(go, rows), :], loc.at[sl, pl.ds(lo, rows), :], sem.at[sl])

    def fetch(tile, sl):
        _run_copies(tile, lo_ref, go_ref, n8_ref, lambda lo, go, rows: row_copy(sl, lo, go, rows))

        def zero_rows(j, carry):
            r0 = pl.multiple_of((nt_ref[tile] + j) * SUBLANES, SUBLANES)
            loc[sl, pl.ds(r0, SUBLANES), :] = jnp.zeros((SUBLANES, D_MODEL), F32)
            return carry

        lax.fori_loop(0, nloc // SUBLANES - nt_ref[tile], zero_rows, 0)

    @pl.when(i == 0)
    def _():
        fetch(0, 0)

    @pl.when(i + 1 < pl.num_programs(0))
    def _():
        fetch(i + 1, 1 - slot)

    _wait_rows(nt_ref[i], lambda rows: row_copy(slot, 0, 0, rows))
    kc = 256
    lane = lax.broadcasted_iota(jnp.int32, (t, 128), 1)
    for k in range(TOP_K):
        lpb[k] = jnp.broadcast_to(lpos_ref[:, k:k + 1], (t, 128))
        gtb[k] = jnp.broadcast_to(gate_ref[:, k:k + 1], (t, 128))
    acc = h_ref[...]
    for c in range(nloc // kc):
        pieces = []
        for j in range(kc // 128):
            li = lane + (c * kc + j * 128)
            w = jnp.zeros((t, 128), F32)
            for k in range(TOP_K):
                w = jnp.where(li == lpb[k], gtb[k], w)
            pieces.append(w)
        wc = jnp.concatenate(pieces, axis=1).astype(BF16)
        acc = acc + jnp.dot(wc, loc[slot, c * kc:(c + 1) * kc, :].astype(BF16), preferred_element_type=F32)
    o_ref[...] = _rms(acc, g_ref[...])


def _combine(tabs, ys, h2, lpos_tok, gates_tok, g, nloc):
    n = h2.shape[0]
    t = min(T_TILE, n)
    grid_spec = pltpu.PrefetchScalarGridSpec(
        num_scalar_prefetch=4,
        grid=(n // t,),
        in_specs=[pl.BlockSpec(memory_space=pl.ANY),
                  pl.BlockSpec((t, D_MODEL), lambda i, *_: (i, 0)),
                  pl.BlockSpec((t, TOP_K), lambda i, *_: (i, 0)),
                  pl.BlockSpec((t, TOP_K), lambda i, *_: (i, 0)),
                  pl.BlockSpec((1, D_MODEL), lambda i, *_: (0, 0))],
        out_specs=pl.BlockSpec((t, D_MODEL), lambda i, *_: (i, 0)),
        scratch_shapes=[pltpu.VMEM((2, nloc, D_MODEL), F32), pltpu.VMEM((TOP_K, t, 128), jnp.int32),
                        pltpu.VMEM((TOP_K, t, 128), F32), pltpu.SemaphoreType.DMA((2,))],
    )
    return pl.pallas_call(
        _combine_kernel,
        out_shape=jax.ShapeDtypeStruct((n, D_MODEL), F32),
        grid_spec=grid_spec,
        compiler_params=pltpu.CompilerParams(dimension_semantics=("arbitrary",), vmem_limit_bytes=VMEM_LIMIT),
        name="combine",
    )(tabs["lo"], tabs["go"], tabs["n8"], tabs["nt"], ys, h2, lpos_tok, gates_tok, g)


def _cmul(a, b):
    return a[0] * b[0] - a[1] * b[1], a[0] * b[1] + a[1] * b[0]


def _prep_mixer_params(sub_len, norm_mix_g, w_in, conv_w, conv_b, rg_wa, rg_ba, rg_wx, rg_bx, rg_lambda,
                       s5_lambda_re, s5_lambda_im, s5_b_re, s5_b_im, s5_c_re, s5_c_im, s5_d, s5_log_dt,
                       s5_glu_w, s5_glu_b, w_out):
    row = lambda v: v.reshape(1, -1).astype(F32)
    dt = jnp.exp(s5_log_dt)[:, None]
    mag = jnp.exp(dt * s5_lambda_re)
    abar_re = mag * jnp.cos(dt * s5_lambda_im)
    abar_im = mag * jnp.sin(dt * s5_lambda_im)
    den = s5_lambda_re * s5_lambda_re + s5_lambda_im * s5_lambda_im
    num_re = abar_re - 1.0
    coef_re = (num_re * s5_lambda_re + abar_im * s5_lambda_im) / den
    coef_im = (abar_im * s5_lambda_re - num_re * s5_lambda_im) / den
    bbar_re = coef_re[..., None] * s5_b_re - coef_im[..., None] * s5_b_im
    bbar_im = coef_re[..., None] * s5_b_im + coef_im[..., None] * s5_b_re
    a1 = (abar_re.reshape(-1), abar_im.reshape(-1))
    aq = a1
    for _ in range(int(math.log2(sub_len))):
        aq = _cmul(aq, aq)
    pw = [aq]
    for _ in range(SUBLANES - 1):
        pw.append(_cmul(pw[-1], aq))
    sub = jnp.arange(SUBLANES)[:, None]
    tabs = []
    for d in (1, 2, 4):
        keep = (sub >= d).astype(F32)
        tabs += [keep * pw[d - 1][0][None, :], keep * pw[d - 1][1][None, :]]
    tabs += [jnp.stack([p_[0] for p_ in pw]), jnp.stack([p_[1] for p_ in pw])]
    tabs += [jnp.broadcast_to(a1[0][None, :], (SUBLANES, N_STATE)),
             jnp.broadcast_to(a1[1][None, :], (SUBLANES, N_STATE))]
    s5tab = jnp.stack(tabs).astype(F32)
    gh = S5_GROUPS // 2
    bmat, cmat = [], []
    for hf in range(2):
        gs = slice(hf * gh, (hf + 1) * gh)
        b_re = _block_diag(jnp.transpose(bbar_re[gs], (0, 2, 1)))
        b_im = _block_diag(jnp.transpose(bbar_im[gs], (0, 2, 1)))
        bmat.append(jnp.concatenate([b_re, b_im], axis=1))
        cr = _block_diag(jnp.transpose(s5_c_re[gs], (0, 2, 1)))
        ci = _block_diag(jnp.transpose(s5_c_im[gs], (0, 2, 1)))
        cmat.append(jnp.concatenate([cr, -ci], axis=0))
    return {
        "g_mix": row(norm_mix_g), "w_in": w_in.astype(BF16), "conv_w": conv_w.astype(F32), "conv_b": row(conv_b),
        "w_gates": jnp.concatenate([_block_diag(rg_wa), _block_diag(rg_wx)], axis=1).astype(BF16),
        "b_gates": jnp.concatenate([rg_ba, rg_bx]).reshape(1, -1).astype(F32),
        "sp": row(jax.nn.softplus(-rg_lambda)),
        "bmat": jnp.stack(bmat).astype(BF16), "s5tab": s5tab, "cmat": jnp.stack(cmat).astype(BF16),
        "d_skip": row(s5_d), "glu_w": _block_diag(s5_glu_w).astype(BF16), "glu_b": row(s5_glu_b),
        "w_out": w_out.astype(BF16),
    }


def _slot_tables(tcnt, n_slots):
    i32 = lambda v: v.astype(jnp.int32)
    run = ((tcnt + SUBLANES - 1) // SUBLANES) * SUBLANES
    rows = jnp.sum(run, axis=0)
    blocks = (rows + TM_MOE - 1) // TM_MOE
    bend = jnp.cumsum(blocks)
    estart = (bend - blocks) * TM_MOE
    nb = n_slots // TM_MOE
    n_used = i32(bend[-1])
    block_exp = jnp.minimum(jnp.sum(jnp.arange(nb)[:, None] >= bend[None, :], axis=1), N_EXPERTS - 1)
    block_exp = jnp.where(jnp.arange(nb) < n_used, block_exp, block_exp[jnp.maximum(n_used - 1, 0)])
    eidx = jnp.arange(N_EXPERTS)
    later = (eidx[None, :] > eidx[:, None]) & (blocks[None, :] > 0)
    next_of = jnp.min(jnp.where(later, eidx[None, :], N_EXPERTS), axis=1)
    next_exp = jnp.where(next_of < N_EXPERTS, next_of, -1)[block_exp]
    tabs = {
        "lo": i32(jnp.cumsum(run, axis=1) - run).reshape(-1),
        "go": i32(estart[None, :] + jnp.cumsum(run, axis=0) - run).reshape(-1),
        "n8": i32(run // SUBLANES).reshape(-1),
        "nt": i32(jnp.sum(run, axis=1) // SUBLANES),
        "zs": i32(jnp.concatenate([estart + rows, bend[-1:] * TM_MOE])),
        "zn": i32(jnp.concatenate([(blocks * TM_MOE - rows) // SUBLANES, nb - bend[-1:]])),
    }
    return i32(block_exp), n_used.reshape(1), i32(next_exp), tabs


def kernel(x, mem, norm_mix_g, w_in, conv_w, conv_b, rg_wa, rg_ba, rg_wx, rg_bx, rg_lambda, s5_lambda_re,
           s5_lambda_im, s5_b_re, s5_b_im, s5_c_re, s5_c_im, s5_d, s5_log_dt, s5_glu_w, s5_glu_b, w_out,
           norm_xa_g, mem_norm_g, xa_wq, xa_wkv, xa_wo, norm_ffn_g, router_w, router_b, exp_w_gu, exp_b_gu,
           exp_w_down, exp_b_down, final_norm_g):
    bsz, seq, d = x.shape
    mlen = mem.shape[1]
    n = bsz * seq
    row = lambda v: v.reshape(1, -1).astype(F32)

    p = _prep_mixer_params(min(T_MIX, seq) // SUBLANES, norm_mix_g, w_in, conv_w, conv_b, rg_wa, rg_ba, rg_wx,
                           rg_bx, rg_lambda, s5_lambda_re, s5_lambda_im, s5_b_re, s5_b_im, s5_c_re, s5_c_im,
                           s5_d, s5_log_dt, s5_glu_w, s5_glu_b, w_out)
    h1 = _mixer(x.reshape(n, d), bsz, seq, p)

    k, v = _kv(mem.reshape(bsz * mlen, d), bsz, mlen, row(mem_norm_g), xa_wkv.astype(BF16))
    h2 = _attn(h1, bsz, seq, mlen, row(norm_xa_g), xa_wq.astype(BF16), k, v, xa_wo.astype(BF16))

    wr_t = router_w.T.astype(F32)
    wr_hi = wr_t.astype(BF16)
    wr_lo = (wr_t - wr_hi.astype(F32)).astype(BF16)
    xn, lpos_t, gate_t, tcnt = _route(h2, row(norm_ffn_g), jnp.concatenate([wr_hi, wr_lo], axis=0),
                                      router_b.reshape(-1, 1).astype(F32))
    tile = min(T_TILE, n)
    n_tiles = n // tile
    run_pad = N_EXPERTS * (SUBLANES - 1)
    nloc = -(-(TOP_K * tile + run_pad) // 128) * 128
    n_slots = -(-(TOP_K * n + n_tiles * run_pad + N_EXPERTS * (TM_MOE - SUBLANES)) // TM_MOE) * TM_MOE
    block_exp, n_used, next_exp, tabs = _slot_tables(tcnt[:, :, 0], n_slots)
    xs = _dispatch(tabs, xn, lpos_t, n_slots, nloc)
    ys = _moe(block_exp, n_used, next_exp, xs, exp_w_gu, exp_b_gu.reshape(N_EXPERTS, 1, -1),
              exp_w_down, exp_b_down.reshape(N_EXPERTS, 1, -1))
    out = _combine(tabs, ys, h2, lpos_t.T, gate_t.T, row(final_norm_g), nloc)
    return out.reshape(bsz, seq, d)
```

```python
import math

import jax
import jax.numpy as jnp
from jax import lax
from jax.experimental import pallas as pl
from jax.experimental.pallas import tpu as pltpu

F32 = jnp.float32
BF16 = jnp.bfloat16

D_MODEL = 1024
D_RG = 512
D_S5 = 512
CONV_W = 4
RG_C = 8.0
S5_GROUPS = 32
S5_STATE = 64
N_STATE = S5_GROUPS * S5_STATE
XA_HEADS = 4
XA_HD = 256
N_EXPERTS = 32
TOP_K = 4
D_FF = 1024
SWIGLU_LIMIT = 7.0
SWIGLU_ALPHA = 1.702
NORM_EPS = 1e-6

SUBLANES = 8
T_MIX = 512
T_ATT = 1024
T_TILE = 512
TM_MOE = 512
VMEM_LIMIT = 56 * 1024 * 1024


def _rms(x, g):
    return x * lax.rsqrt(jnp.mean(x * x, axis=-1, keepdims=True) + NORM_EPS) * g


def _gelu_tanh(x):
    return 0.5 * x * (1.0 + jnp.tanh(math.sqrt(2.0 / math.pi) * (x + 0.044715 * (x * x * x))))


def _sigmoid(x):
    return 1.0 / (1.0 + jnp.exp(-x))


def _block_diag(blocks):
    n, r, c = blocks.shape
    eye = jnp.eye(n, dtype=blocks.dtype)
    return jnp.einsum('nrc,nm->nrmc', blocks, eye).reshape(n * r, n * c)


def _sublane_chain(al, e, carry, sub):
    for d in (1, 2, 4):
        al_sh = jnp.where(sub >= d, pltpu.roll(al, d, axis=0), 1.0)
        e_sh = jnp.where(sub >= d, pltpu.roll(e, d, axis=0), 0.0)
        e = al * e_sh + e
        al = al * al_sh
    s_end = e + al * carry
    start = jnp.where(sub == 0, carry, pltpu.roll(s_end, 1, axis=0))
    return start, s_end[SUBLANES - 1:SUBLANES, :]


def _mixer_kernel(x_ref, g_ref, win_ref, convw_ref, convb_ref, wg_ref, bg_ref, sp_ref,
                  bmat_ref, tab_ref, cmat_ref, dskip_ref, glu_ref, glub_ref, wout_ref,
                  o_ref,
                  xin, xout, sem_i, sem_o, xbuf, hist, a_s, b_s, rg_carry, sre, sim, c_re, c_im):
    ngrp = xin.shape[1]
    t = ngrp * SUBLANES
    nh = (CONV_W - 1) * SUBLANES
    c = pl.program_id(1)
    step = pl.program_id(0) * pl.num_programs(1) + c
    nsteps = pl.num_programs(0) * pl.num_programs(1)
    slot = lax.rem(step, 2)

    def in_copy(s, sl, i):
        return pltpu.make_async_copy(x_ref.at[pl.ds(s * t + i * ngrp, ngrp), :], xin.at[sl, :, i, :], sem_i.at[sl])

    def out_copy(s, sl, i):
        return pltpu.make_async_copy(xout.at[sl, :, i, :], o_ref.at[pl.ds(s * t + i * ngrp, ngrp), :], sem_o.at[sl])

    @pl.when(step == 0)
    def _():
        for i in range(SUBLANES):
            in_copy(0, 0, i).start()

    @pl.when(c == 0)
    def _():
        hist[...] = jnp.zeros_like(hist)
        rg_carry[...] = jnp.zeros_like(rg_carry)
        c_re[...] = jnp.zeros_like(c_re)
        c_im[...] = jnp.zeros_like(c_im)

    for i in range(SUBLANES):
        in_copy(step, slot, i).wait()

    @pl.when(step + 1 < nsteps)
    def _():
        for i in range(SUBLANES):
            in_copy(step + 1, 1 - slot, i).start()

    x = xin[slot].reshape(t, D_MODEL)
    xn = _rms(x, g_ref[...]).astype(BF16)
    proj = jnp.dot(xn, win_ref[...], preferred_element_type=F32)
    gate_rg = proj[:, :D_RG]
    x_rg = proj[:, D_RG:2 * D_RG]
    u = proj[:, 2 * D_RG:]

    sub = lax.broadcasted_iota(jnp.int32, (SUBLANES, D_RG), 0)
    cur_tail = x_rg[t - nh:, :]
    for gidx in range(CONV_W - 1):
        rs = slice(gidx * SUBLANES, (gidx + 1) * SUBLANES)
        xbuf[rs, :] = jnp.where(sub == 0, pltpu.roll(hist[rs, :], 1, axis=0), pltpu.roll(cur_tail[rs, :], 1, axis=0))
    hist[...] = cur_tail
    xbuf[nh:, :] = x_rg
    xc = convb_ref[...] + convw_ref[CONV_W - 1:CONV_W, :] * x_rg
    for k in range(CONV_W - 1):
        xc = xc + convw_ref[k:k + 1, :] * xbuf[k * SUBLANES:k * SUBLANES + t, :]
    gates = jnp.dot(xc.astype(BF16), wg_ref[...], preferred_element_type=F32) + bg_ref[...]
    r = _sigmoid(gates[:, :D_RG])
    ig = _sigmoid(gates[:, D_RG:])
    log_a = (-RG_C) * r * sp_ref[...]
    a = jnp.exp(log_a)
    mult = jnp.sqrt(1.0 - a * a)
    row = lax.broadcasted_iota(jnp.int32, (t, D_RG), 0)
    mult = jnp.where((row == 0) & (c == 0), 1.0, mult)
    a_s[...] = a
    b_s[...] = mult * (ig * xc)

    def rg_pass(store):
        def body(q, carry):
            h, ac = carry
            r0 = pl.multiple_of(q * SUBLANES, SUBLANES)
            aa = a_s[pl.ds(r0, SUBLANES), :]
            h = aa * h + b_s[pl.ds(r0, SUBLANES), :]
            if store:
                b_s[pl.ds(r0, SUBLANES), :] = h
                return h, ac
            return h, aa * ac
        return body

    zeros8 = jnp.zeros((SUBLANES, D_RG), F32)
    ones8 = jnp.ones((SUBLANES, D_RG), F32)
    e_loc, a_loc = lax.fori_loop(0, ngrp, rg_pass(False), (zeros8, ones8), unroll=4)
    start, last = _sublane_chain(a_loc, e_loc, rg_carry[...], sub)
    rg_carry[...] = last
    lax.fori_loop(0, ngrp, rg_pass(True), (start, ones8), unroll=4)
    y_rg = b_s[...] * _gelu_tanh(gate_rg)

    ub = u.astype(BF16)
    half = N_STATE // 2
    for hf in range(2):
        bu = jnp.dot(ub[:, hf * 256:(hf + 1) * 256], bmat_ref[hf], preferred_element_type=F32)
        sre[:, hf * half:(hf + 1) * half] = bu[:, :half]
        sim[:, hf * half:(hf + 1) * half] = bu[:, half:]

    lb = 1024
    subl = lax.broadcasted_iota(jnp.int32, (SUBLANES, lb), 0)
    for blk in range(N_STATE // lb):
        cs = slice(blk * lb, (blk + 1) * lb)
        ar = tab_ref[8, :, cs]
        ai = tab_ref[9, :, cs]

        def s5_pass(store, cs=cs, ar=ar, ai=ai):
            def body(q, carry):
                hr, hi = carry
                r0 = pl.multiple_of(q * SUBLANES, SUBLANES)
                nr = ar * hr - ai * hi + sre[pl.ds(r0, SUBLANES), cs]
                ni = ar * hi + ai * hr + sim[pl.ds(r0, SUBLANES), cs]
                if store:
                    sre[pl.ds(r0, SUBLANES), cs] = nr
                    sim[pl.ds(r0, SUBLANES), cs] = ni
                return nr, ni
            return body

        z8 = jnp.zeros((SUBLANES, lb), F32)
        er, ei = lax.fori_loop(0, ngrp, s5_pass(False), (z8, z8), unroll=4)
        for i, d in enumerate((1, 2, 4)):
            mr = tab_ref[2 * i, :, cs]
            mi = tab_ref[2 * i + 1, :, cs]
            rr = pltpu.roll(er, d, axis=0)
            ri = pltpu.roll(ei, d, axis=0)
            er, ei = er + (mr * rr - mi * ri), ei + (mr * ri + mi * rr)
        pr = tab_ref[6, :, cs]
        pi_ = tab_ref[7, :, cs]
        cr = c_re[:, cs]
        ci = c_im[:, cs]
        sr = er + (pr * cr - pi_ * ci)
        si = ei + (pr * ci + pi_ * cr)
        c_re[:, cs] = sr[SUBLANES - 1:SUBLANES, :]
        c_im[:, cs] = si[SUBLANES - 1:SUBLANES, :]
        st_r = jnp.where(subl == 0, cr, pltpu.roll(sr, 1, axis=0))
        st_i = jnp.where(subl == 0, ci, pltpu.roll(si, 1, axis=0))
        lax.fori_loop(0, ngrp, s5_pass(True), (st_r, st_i), unroll=4)

    ys = []
    for hf in range(2):
        hcat = jnp.concatenate([sre[:, hf * half:(hf + 1) * half], sim[:, hf * half:(hf + 1) * half]], axis=1)
        ys.append(jnp.dot(hcat.astype(BF16), cmat_ref[hf], preferred_element_type=F32))
    y = jnp.concatenate(ys, axis=1) + dskip_ref[...] * u
    z = _gelu_tanh(y)
    zg = _sigmoid(jnp.dot(z.astype(BF16), glu_ref[...], preferred_element_type=F32) + glub_ref[...])
    y_s5 = z * zg

    ycat = jnp.concatenate([y_rg, y_s5], axis=1).astype(BF16)
    res = x + jnp.dot(ycat, wout_ref[...], preferred_element_type=F32)

    @pl.when(step >= 2)
    def _():
        for i in range(SUBLANES):
            out_copy(step - 2, slot, i).wait()

    xout[slot] = res.reshape(ngrp, SUBLANES, D_MODEL)
    for i in range(SUBLANES):
        out_copy(step, slot, i).start()

    @pl.when(step == nsteps - 1)
    def _():
        for i in range(SUBLANES):
            out_copy(step, slot, i).wait()

        @pl.when(step >= 1)
        def _():
            for i in range(SUBLANES):
                out_copy(step - 1, 1 - slot, i).wait()


def _mixer(x2, bsz, seq, p):
    t = min(T_MIX, seq)
    nc = seq // t
    nh = (CONV_W - 1) * SUBLANES
    const = lambda shape: pl.BlockSpec(shape, lambda b, c: (0,) * len(shape))
    return pl.pallas_call(
        _mixer_kernel,
        out_shape=jax.ShapeDtypeStruct(x2.shape, F32),
        grid=(bsz, nc),
        in_specs=[
            pl.BlockSpec(memory_space=pl.ANY),
            const((1, D_MODEL)), const((D_MODEL, 3 * D_RG)), const((CONV_W, D_RG)), const((1, D_RG)),
            const((D_RG, 2 * D_RG)), const((1, 2 * D_RG)), const((1, D_RG)),
            const((2, 256, N_STATE)), const((10, SUBLANES, N_STATE)), const((2, N_STATE, 256)),
            const((1, D_S5)), const((D_S5, D_S5)), const((1, D_S5)), const((D_MODEL, D_MODEL)),
        ],
        out_specs=pl.BlockSpec(memory_space=pl.ANY),
        scratch_shapes=[
            pltpu.VMEM((2, t // SUBLANES, SUBLANES, D_MODEL), F32),
            pltpu.VMEM((2, t // SUBLANES, SUBLANES, D_MODEL), F32),
            pltpu.SemaphoreType.DMA((2,)), pltpu.SemaphoreType.DMA((2,)),
            pltpu.VMEM((t + nh, D_RG), F32), pltpu.VMEM((nh, D_RG), F32),
            pltpu.VMEM((t, D_RG), F32), pltpu.VMEM((t, D_RG), F32), pltpu.VMEM((1, D_RG), F32),
            pltpu.VMEM((t, N_STATE), F32), pltpu.VMEM((t, N_STATE), F32),
            pltpu.VMEM((1, N_STATE), F32), pltpu.VMEM((1, N_STATE), F32),
        ],
        compiler_params=pltpu.CompilerParams(
            dimension_semantics=("arbitrary", "arbitrary"), vmem_limit_bytes=VMEM_LIMIT),
        name="mixer",
    )(x2, p["g_mix"], p["w_in"], p["conv_w"], p["conv_b"], p["w_gates"], p["b_gates"], p["sp"],
      p["bmat"], p["s5tab"], p["cmat"], p["d_skip"], p["glu_w"], p["glu_b"], p["w_out"])


def _kv_kernel(m_ref, g_ref, w_ref, k_ref, v_ref):
    mn = _rms(m_ref[...], g_ref[...]).astype(BF16)
    kv = jnp.dot(mn, w_ref[...], preferred_element_type=F32)
    k_ref[...] = kv[:, :D_MODEL].astype(BF16)
    v_ref[...] = kv[:, D_MODEL:].astype(BF16)


def _kv(mem2, bsz, mlen, g, wkv):
    return pl.pallas_call(
        _kv_kernel,
        out_shape=(jax.ShapeDtypeStruct(mem2.shape, BF16), jax.ShapeDtypeStruct(mem2.shape, BF16)),
        grid=(bsz,),
        in_specs=[pl.BlockSpec((mlen, D_MODEL), lambda b: (b, 0)),
                  pl.BlockSpec((1, D_MODEL), lambda b: (0, 0)),
                  pl.BlockSpec((D_MODEL, 2 * D_MODEL), lambda b: (0, 0))],
        out_specs=(pl.BlockSpec((mlen, D_MODEL), lambda b: (b, 0)),
                   pl.BlockSpec((mlen, D_MODEL), lambda b: (b, 0))),
        compiler_params=pltpu.CompilerParams(dimension_semantics=("arbitrary",), vmem_limit_bytes=VMEM_LIMIT),
        name="kv",
    )(mem2, g, wkv)


def _attn_kernel(h_ref, g_ref, wq_ref, k_ref, v_ref, wo_ref, o_ref):
    h = h_ref[...]
    xn = _rms(h, g_ref[...]).astype(BF16)
    q = jnp.dot(xn, wq_ref[...], preferred_element_type=F32).astype(BF16)
    outs = []
    for hd in range(XA_HEADS):
        cs = slice(hd * XA_HD, (hd + 1) * XA_HD)
        s = lax.dot_general(q[:, cs], k_ref[:, cs], (((1,), (1,)), ((), ())),
                            preferred_element_type=F32) * (XA_HD ** -0.5)
        s = s - jnp.max(s, axis=-1, keepdims=True)
        e = jnp.exp(s)
        pr = e / jnp.sum(e, axis=-1, keepdims=True)
        outs.append(jnp.dot(pr.astype(BF16), v_ref[:, cs], preferred_element_type=F32))
    o = jnp.concatenate(outs, axis=1).astype(BF16)
    o_ref[...] = h + jnp.dot(o, wo_ref[...], preferred_element_type=F32)


def _attn(h2, bsz, seq, mlen, g, wq, k, v, wo):
    t = min(T_ATT, seq)
    nc = seq // t
    return pl.pallas_call(
        _attn_kernel,
        out_shape=jax.ShapeDtypeStruct(h2.shape, F32),
        grid=(bsz, nc),
        in_specs=[pl.BlockSpec((t, D_MODEL), lambda b, c: (b * nc + c, 0)),
                  pl.BlockSpec((1, D_MODEL), lambda b, c: (0, 0)),
                  pl.BlockSpec((D_MODEL, D_MODEL), lambda b, c: (0, 0)),
                  pl.BlockSpec((mlen, D_MODEL), lambda b, c: (b, 0)),
                  pl.BlockSpec((mlen, D_MODEL), lambda b, c: (b, 0)),
                  pl.BlockSpec((D_MODEL, D_MODEL), lambda b, c: (0, 0))],
        out_specs=pl.BlockSpec((t, D_MODEL), lambda b, c: (b * nc + c, 0)),
        compiler_params=pltpu.CompilerParams(
            dimension_semantics=("arbitrary", "arbitrary"), vmem_limit_bytes=VMEM_LIMIT),
        name="attn",
    )(h2, g, wq, k, v, wo)


def _route_kernel(h_ref, g_ref, wr_ref, br_ref, xn_ref, lpos_ref, gate_ref, tcnt_ref):
    t = h_ref.shape[0]
    xn = _rms(h_ref[...], g_ref[...])
    x_hi = xn.astype(BF16)
    x_lo = (xn - x_hi.astype(F32)).astype(BF16)
    xn_ref[...] = x_hi
    dims = (((1,), (1,)), ((), ()))
    p_hi = lax.dot_general(wr_ref[...], x_hi, dims, preferred_element_type=F32)
    p_lo = lax.dot_general(wr_ref[0:N_EXPERTS, :], x_lo, dims, preferred_element_type=F32)
    lg = p_hi[0:N_EXPERTS, :] + (p_hi[N_EXPERTS:, :] + p_lo) + br_ref[...]
    erow = lax.broadcasted_iota(jnp.int32, (N_EXPERTS, t), 0)
    vals, hots = [], []
    for _ in range(TOP_K):
        m = jnp.max(lg, axis=0, keepdims=True)
        ix = jnp.min(jnp.where(lg == m, erow, N_EXPERTS), axis=0, keepdims=True)
        hot = erow == ix
        lg = jnp.where(hot, -jnp.inf, lg)
        vals.append(m)
        hots.append(hot)
    es = [jnp.exp(v - vals[0]) for v in vals]
    den = es[0] + es[1] + es[2] + es[3]
    sel = jnp.zeros((N_EXPERTS, t), F32)
    for hot in hots:
        sel = sel + jnp.where(hot, 1.0, 0.0)
    tri = jnp.where(lax.broadcasted_iota(jnp.int32, (t, t), 0) < lax.broadcasted_iota(jnp.int32, (t, t), 1),
                    1.0, 0.0).astype(BF16)
    rank = jnp.dot(sel.astype(BF16), tri, preferred_element_type=F32)
    cnt = jnp.sum(sel, axis=1, keepdims=True)
    run8 = jnp.floor((cnt + (SUBLANES - 1.0)) * (1.0 / SUBLANES))
    below = jnp.where(lax.broadcasted_iota(jnp.int32, (N_EXPERTS, N_EXPERTS), 1)
                      < lax.broadcasted_iota(jnp.int32, (N_EXPERTS, N_EXPERTS), 0), 1.0, 0.0).astype(BF16)
    start8 = jnp.dot(below, jnp.broadcast_to(run8, (N_EXPERTS, 128)).astype(BF16), preferred_element_type=F32)
    pos = rank + SUBLANES * start8[:, 0:1]
    for k in range(TOP_K):
        gate_ref[k:k + 1, :] = es[k] / den
        lpos_ref[k:k + 1, :] = jnp.sum(jnp.where(hots[k], pos, 0.0), axis=0, keepdims=True).astype(jnp.int32)
    tcnt_ref[0] = jnp.broadcast_to(cnt, (N_EXPERTS, 128)).astype(jnp.int32)


def _route(h2, g, wr_t, br):
    n = h2.shape[0]
    t = min(T_TILE, n)
    return pl.pallas_call(
        _route_kernel,
        out_shape=(jax.ShapeDtypeStruct((n, D_MODEL), BF16),
                   jax.ShapeDtypeStruct((TOP_K, n), jnp.int32),
                   jax.ShapeDtypeStruct((TOP_K, n), F32),
                   jax.ShapeDtypeStruct((n // t, N_EXPERTS, 128), jnp.int32)),
        grid=(n // t,),
        in_specs=[pl.BlockSpec((t, D_MODEL), lambda i: (i, 0)),
                  pl.BlockSpec((1, D_MODEL), lambda i: (0, 0)),
                  pl.BlockSpec((2 * N_EXPERTS, D_MODEL), lambda i: (0, 0)),
                  pl.BlockSpec((N_EXPERTS, 1), lambda i: (0, 0))],
        out_specs=(pl.BlockSpec((t, D_MODEL), lambda i: (i, 0)),
                   pl.BlockSpec((TOP_K, t), lambda i: (0, i)),
                   pl.BlockSpec((TOP_K, t), lambda i: (0, i)),
                   pl.BlockSpec((1, N_EXPERTS, 128), lambda i: (i, 0, 0))),
        compiler_params=pltpu.CompilerParams(dimension_semantics=("arbitrary",), vmem_limit_bytes=VMEM_LIMIT),
        name="route",
    )(h2, g, wr_t, br)


RUN_CHUNK = 64
WAIT_CHUNK = 256


def _run_copies(i, lo_ref, go_ref, n8_ref, make_copy):
    def expert(e, carry):
        idx = i * N_EXPERTS + e
        lo = lo_ref[idx]
        go = go_ref[idx]
        n8 = n8_ref[idx]
        nbig = lax.shift_right_logical(n8, 3)

        def big(j, c):
            make_copy(pl.multiple_of(lo + j * RUN_CHUNK, SUBLANES), pl.multiple_of(go + j * RUN_CHUNK, SUBLANES),
                      RUN_CHUNK).start()
            return c

        lax.fori_loop(0, nbig, big, 0)
        off = nbig * RUN_CHUNK
        for rows in (32, 16, 8):
            has = (n8 & (rows // SUBLANES)) != 0

            @pl.when(has)
            def _(off=off, rows=rows):
                make_copy(pl.multiple_of(lo + off, SUBLANES), pl.multiple_of(go + off, SUBLANES), rows).start()

            off = off + jnp.where(has, rows, 0)
        return carry

    lax.fori_loop(0, N_EXPERTS, expert, 0)


def _wait_rows(n8, make_wait):
    def big(j, c):
        make_wait(WAIT_CHUNK).wait()
        return c

    lax.fori_loop(0, lax.shift_right_logical(n8, 5), big, 0)
    for rows in (128, 64, 32, 16, 8):
        @pl.when((n8 & (rows // SUBLANES)) != 0)
        def _(rows=rows):
            make_wait(rows).wait()


def _dispatch_kernel(lo_ref, go_ref, n8_ref, nt_ref, zs_ref, zn_ref, xn_ref, lpos_ref, xs_hbm, loc, zbuf, sem,
                     sem_z):
    i = pl.program_id(0)
    last = pl.num_programs(0) - 1
    slot = lax.rem(i, 2)
    nloc, t = loc.shape[1], xn_ref.shape[0]

    def row_copy(sl, lo, go, rows):
        return pltpu.make_async_copy(loc.at[sl, pl.ds(lo, rows), :], xs_hbm.at[pl.ds(go, rows), :], sem.at[sl])

    def wait_copies(sl, count):
        _wait_rows(count, lambda rows: row_copy(sl, 0, 0, rows))

    @pl.when(i >= 2)
    def _():
        wait_copies(slot, nt_ref[jnp.maximum(i - 2, 0)])

    mc = 256
    liota = lax.broadcasted_iota(jnp.int32, (mc, t), 0)
    xn = xn_ref[...]
    for r in range(nloc // mc):
        pick = jnp.zeros((mc, t), F32)
        for k in range(TOP_K):
            pick = jnp.where(liota == lpos_ref[k:k + 1, :] - r * mc, 1.0, pick)
        loc[slot, r * mc:(r + 1) * mc, :] = jnp.dot(pick.astype(BF16), xn, preferred_element_type=F32)
    _run_copies(i, lo_ref, go_ref, n8_ref, lambda lo, go, rows: row_copy(slot, lo, go, rows))

    @pl.when(i == last)
    def _():
        wait_copies(slot, nt_ref[i])

        @pl.when(i >= 1)
        def _():
            wait_copies(1 - slot, nt_ref[jnp.maximum(i - 1, 0)])

        zbuf[...] = jnp.zeros_like(zbuf)

        def zero_copy(go):
            return pltpu.make_async_copy(zbuf.at[pl.ds(0, SUBLANES), :], xs_hbm.at[pl.ds(go, SUBLANES), :], sem_z)

        def expert(e, total):
            def chunk(j, carry):
                zero_copy(pl.multiple_of(zs_ref[e] + j * SUBLANES, SUBLANES)).start()
                return carry
            lax.fori_loop(0, zn_ref[e], chunk, 0)
            return total + zn_ref[e]

        total = lax.fori_loop(0, N_EXPERTS, expert, 0)

        def wait_zero(j, carry):
            zero_copy(0).wait()
            return carry

        lax.fori_loop(0, total, wait_zero, 0)

        def block_copy(go):
            return pltpu.make_async_copy(zbuf, xs_hbm.at[pl.ds(go, TM_MOE), :], sem_z)

        def start_block(j, carry):
            block_copy(pl.multiple_of(zs_ref[N_EXPERTS] + j * TM_MOE, TM_MOE)).start()
            return carry

        def wait_block(j, carry):
            block_copy(0).wait()
            return carry

        lax.fori_loop(0, zn_ref[N_EXPERTS], start_block, 0)
        lax.fori_loop(0, zn_ref[N_EXPERTS], wait_block, 0)


def _dispatch(tabs, xn, lpos_t, n_slots, nloc):
    n = xn.shape[0]
    t = min(T_TILE, n)
    grid_spec = pltpu.PrefetchScalarGridSpec(
        num_scalar_prefetch=6,
        grid=(n // t,),
        in_specs=[pl.BlockSpec((t, D_MODEL), lambda i, *_: (i, 0)),
                  pl.BlockSpec((TOP_K, t), lambda i, *_: (0, i))],
        out_specs=pl.BlockSpec(memory_space=pl.ANY),
        scratch_shapes=[pltpu.VMEM((2, nloc, D_MODEL), F32), pltpu.VMEM((TM_MOE, D_MODEL), F32),
                        pltpu.SemaphoreType.DMA((2,)), pltpu.SemaphoreType.DMA],
    )
    return pl.pallas_call(
        _dispatch_kernel,
        out_shape=jax.ShapeDtypeStruct((n_slots, D_MODEL), F32),
        grid_spec=grid_spec,
        compiler_params=pltpu.CompilerParams(dimension_semantics=("arbitrary",), vmem_limit_bytes=VMEM_LIMIT),
        name="dispatch",
    )(tabs["lo"], tabs["go"], tabs["n8"], tabs["nt"], tabs["zs"], tabs["zn"], xn, lpos_t)


def _moe_kernel(bexp_ref, nused_ref, nexp_ref, x_ref, wgu_hbm, bgu_ref, wd_hbm, bd_ref, o_ref,
                wgu_f, wd_f, wgu_s, wd_s, sem_w):
    b = pl.program_id(0)
    prev = bexp_ref[jnp.maximum(b - 1, 0)]
    fresh = (b == 0) | (bexp_ref[b] != prev)

    def fetch(ex):
        return (pltpu.make_async_copy(wgu_hbm.at[ex], wgu_f, sem_w.at[0]),
                pltpu.make_async_copy(wd_hbm.at[ex], wd_f, sem_w.at[1]))

    @pl.when(b == 0)
    def _():
        for cp in fetch(bexp_ref[0]):
            cp.start()

    @pl.when(fresh & (b < nused_ref[0]))
    def _():
        for cp in fetch(bexp_ref[b]):
            cp.wait()
        rows = 128
        for r0 in range(0, D_MODEL, rows):
            wgu_s[r0:r0 + rows, :] = wgu_f[r0:r0 + rows, :].astype(BF16)
        for r0 in range(0, D_FF, rows):
            wd_s[r0:r0 + rows, :] = wd_f[r0:r0 + rows, :].astype(BF16)

        @pl.when(nexp_ref[b] >= 0)
        def _():
            for cp in fetch(nexp_ref[b]):
                cp.start()

    @pl.when(b < nused_ref[0])
    def _():
        xb = x_ref[...].astype(BF16)
        gu = jnp.dot(xb, wgu_s[...], preferred_element_type=F32) + bgu_ref[0]
        g = jnp.minimum(gu[:, :D_FF], SWIGLU_LIMIT)
        u = jnp.clip(gu[:, D_FF:], -SWIGLU_LIMIT, SWIGLU_LIMIT)
        h = (u + 1.0) * (g * _sigmoid(SWIGLU_ALPHA * g))
        o_ref[...] = jnp.dot(h.astype(BF16), wd_s[...], preferred_element_type=F32) + bd_ref[0]

    @pl.when(b >= nused_ref[0])
    def _():
        o_ref[...] = jnp.zeros_like(o_ref)


def _moe(block_exp, n_used, next_exp, xs, w_gu, b_gu, w_down, b_down):
    n_slots = xs.shape[0]
    nb = n_slots // TM_MOE
    bspec = lambda shape: pl.BlockSpec(shape, lambda b, be, nu, ne: (be[b], 0, 0))
    grid_spec = pltpu.PrefetchScalarGridSpec(
        num_scalar_prefetch=3,
        grid=(nb,),
        in_specs=[
            pl.BlockSpec((TM_MOE, D_MODEL), lambda b, be, nu, ne: (jnp.minimum(b, nu[0] - 1), 0)),
            pl.BlockSpec(memory_space=pl.ANY), bspec((1, 1, 2 * D_FF)),
            pl.BlockSpec(memory_space=pl.ANY), bspec((1, 1, D_MODEL)),
        ],
        out_specs=pl.BlockSpec((TM_MOE, D_MODEL), lambda b, be, nu, ne: (b, 0)),
        scratch_shapes=[pltpu.VMEM((D_MODEL, 2 * D_FF), F32), pltpu.VMEM((D_FF, D_MODEL), F32),
                        pltpu.VMEM((D_MODEL, 2 * D_FF), BF16), pltpu.VMEM((D_FF, D_MODEL), BF16),
                        pltpu.SemaphoreType.DMA((2,))],
    )
    return pl.pallas_call(
        _moe_kernel,
        out_shape=jax.ShapeDtypeStruct((n_slots, D_MODEL), F32),
        grid_spec=grid_spec,
        compiler_params=pltpu.CompilerParams(dimension_semantics=("arbitrary",), vmem_limit_bytes=VMEM_LIMIT),
        name="moe",
    )(block_exp, n_used, next_exp, xs, w_gu, b_gu, w_down, b_down)


def _combine_kernel(lo_ref, go_ref, n8_ref, nt_ref, ys_hbm, h_ref, lpos_ref, gate_ref, g_ref, o_ref, loc, lpb,
                    gtb, sem):
    i = pl.program_id(0)
    slot = lax.rem(i, 2)
    nloc, t = loc.shape[1], h_ref.shape[0]

    def row_copy(sl, lo, go, rows):
        return pltpu.make_async_copy(ys_hbm.at[pl.ds(go, rows), :], loc.at[sl, pl.ds(lo, rows), :], sem.at[sl])

    def fetch(tile, sl):
        _run_copies(tile, lo_ref, go_ref, n8_ref, lambda lo, go, rows: row_copy(sl, lo, go, rows))

        def zero_rows(j, carry):
            r0 = pl.multiple_of((nt_ref[tile] + j) * SUBLANES, SUBLANES)
            loc[sl, pl.ds(r0, SUBLANES), :] = jnp.zeros((SUBLANES, D_MODEL), F32)
            return carry

        lax.fori_loop(0, nloc // SUBLANES - nt_ref[tile], zero_rows, 0)

    @pl.when(i == 0)
    def _():
        fetch(0, 0)

    @pl.when(i + 1 < pl.num_programs(0))
    def _():
        fetch(i + 1, 1 - slot)

    _wait_rows(nt_ref[i], lambda rows: row_copy(slot, 0, 0, rows))
    kc = 256
    lane = lax.broadcasted_iota(jnp.int32, (t, 128), 1)
    for k in range(TOP_K):
        lpb[k] = jnp.broadcast_to(lpos_ref[:, k:k + 1], (t, 128))
        gtb[k] = jnp.broadcast_to(gate_ref[:, k:k + 1], (t, 128))
    acc = h_ref[...]
    for c in range(nloc // kc):
        pieces = []
        for j in range(kc // 128):
            li = lane + (c * kc + j * 128)
            w = jnp.zeros((t, 128), F32)
            for k in range(TOP_K):
                w = jnp.where(li == lpb[k], gtb[k], w)
            pieces.append(w)
        wc = jnp.concatenate(pieces, axis=1).astype(BF16)
        acc = acc + jnp.dot(wc, loc[slot, c * kc:(c + 1) * kc, :].astype(BF16), preferred_element_type=F32)
    o_ref[...] = _rms(acc, g_ref[...])


def _combine(tabs, ys, h2, lpos_tok, gates_tok, g, nloc):
    n = h2.shape[0]
    t = min(T_TILE, n)
    grid_spec = pltpu.PrefetchScalarGridSpec(
        num_scalar_prefetch=4,
        grid=(n // t,),
        in_specs=[pl.BlockSpec(memory_space=pl.ANY),
                  pl.BlockSpec((t, D_MODEL), lambda i, *_: (i, 0)),
                  pl.BlockSpec((t, TOP_K), lambda i, *_: (i, 0)),
                  pl.BlockSpec((t, TOP_K), lambda i, *_: (i, 0)),
                  pl.BlockSpec((1, D_MODEL), lambda i, *_: (0, 0))],
        out_specs=pl.BlockSpec((t, D_MODEL), lambda i, *_: (i, 0)),
        scratch_shapes=[pltpu.VMEM((2, nloc, D_MODEL), F32), pltpu.VMEM((TOP_K, t, 128), jnp.int32),
                        pltpu.VMEM((TOP_K, t, 128), F32), pltpu.SemaphoreType.DMA((2,))],
    )
    return pl.pallas_call(
        _combine_kernel,
        out_shape=jax.ShapeDtypeStruct((n, D_MODEL), F32),
        grid_spec=grid_spec,
        compiler_params=pltpu.CompilerParams(dimension_semantics=("arbitrary",), vmem_limit_bytes=VMEM_LIMIT),
        name="combine",
    )(tabs["lo"], tabs["go"], tabs["n8"], tabs["nt"], ys, h2, lpos_tok, gates_tok, g)


def _cmul(a, b):
    return a[0] * b[0] - a[1] * b[1], a[0] * b[1] + a[1] * b[0]


def _prep_mixer_params(sub_len, norm_mix_g, w_in, conv_w, conv_b, rg_wa, rg_ba, rg_wx, rg_bx, rg_lambda,
                       s5_lambda_re, s5_lambda_im, s5_b_re, s5_b_im, s5_c_re, s5_c_im, s5_d, s5_log_dt,
                       s5_glu_w, s5_glu_b, w_out):
    row = lambda v: v.reshape(1, -1).astype(F32)
    dt = jnp.exp(s5_log_dt)[:, None]
    mag = jnp.exp(dt * s5_lambda_re)
    abar_re = mag * jnp.cos(dt * s5_lambda_im)
    abar_im = mag * jnp.sin(dt * s5_lambda_im)
    den = s5_lambda_re * s5_lambda_re + s5_lambda_im * s5_lambda_im
    num_re = abar_re - 1.0
    coef_re = (num_re * s5_lambda_re + abar_im * s5_lambda_im) / den
    coef_im = (abar_im * s5_lambda_re - num_re * s5_lambda_im) / den
    bbar_re = coef_re[..., None] * s5_b_re - coef_im[..., None] * s5_b_im
    bbar_im = coef_re[..., None] * s5_b_im + coef_im[..., None] * s5_b_re
    a1 = (abar_re.reshape(-1), abar_im.reshape(-1))
    aq = a1
    for _ in range(int(math.log2(sub_len))):
        aq = _cmul(aq, aq)
    pw = [aq]
    for _ in range(SUBLANES - 1):
        pw.append(_cmul(pw[-1], aq))
    sub = jnp.arange(SUBLANES)[:, None]
    tabs = []
    for d in (1, 2, 4):
        keep = (sub >= d).astype(F32)
        tabs += [keep * pw[d - 1][0][None, :], keep * pw[d - 1][1][None, :]]
    tabs += [jnp.stack([p_[0] for p_ in pw]), jnp.stack([p_[1] for p_ in pw])]
    tabs += [jnp.broadcast_to(a1[0][None, :], (SUBLANES, N_STATE)),
             jnp.broadcast_to(a1[1][None, :], (SUBLANES, N_STATE))]
    s5tab = jnp.stack(tabs).astype(F32)
    gh = S5_GROUPS // 2
    bmat, cmat = [], []
    for hf in range(2):
        gs = slice(hf * gh, (hf + 1) * gh)
        b_re = _block_diag(jnp.transpose(bbar_re[gs], (0, 2, 1)))
        b_im = _block_diag(jnp.transpose(bbar_im[gs], (0, 2, 1)))
        bmat.append(jnp.concatenate([b_re, b_im], axis=1))
        cr = _block_diag(jnp.transpose(s5_c_re[gs], (0, 2, 1)))
        ci = _block_diag(jnp.transpose(s5_c_im[gs], (0, 2, 1)))
        cmat.append(jnp.concatenate([cr, -ci], axis=0))
    return {
        "g_mix": row(norm_mix_g), "w_in": w_in.astype(BF16), "conv_w": conv_w.astype(F32), "conv_b": row(conv_b),
        "w_gates": jnp.concatenate([_block_diag(rg_wa), _block_diag(rg_wx)], axis=1).astype(BF16),
        "b_gates": jnp.concatenate([rg_ba, rg_bx]).reshape(1, -1).astype(F32),
        "sp": row(jax.nn.softplus(-rg_lambda)),
        "bmat": jnp.stack(bmat).astype(BF16), "s5tab": s5tab, "cmat": jnp.stack(cmat).astype(BF16),
        "d_skip": row(s5_d), "glu_w": _block_diag(s5_glu_w).astype(BF16), "glu_b": row(s5_glu_b),
        "w_out": w_out.astype(BF16),
    }


def _slot_tables(tcnt, n_slots):
    i32 = lambda v: v.astype(jnp.int32)
    run = ((tcnt + SUBLANES - 1) // SUBLANES) * SUBLANES
    rows = jnp.sum(run, axis=0)
    blocks = (rows + TM_MOE - 1) // TM_MOE
    bend = jnp.cumsum(blocks)
    estart = (bend - blocks) * TM_MOE
    nb = n_slots // TM_MOE
    n_used = i32(bend[-1])
    block_exp = jnp.minimum(jnp.sum(jnp.arange(nb)[:, None] >= bend[None, :], axis=1), N_EXPERTS - 1)
    block_exp = jnp.where(jnp.arange(nb) < n_used, block_exp, block_exp[jnp.maximum(n_used - 1, 0)])
    eidx = jnp.arange(N_EXPERTS)
    later = (eidx[None, :] > eidx[:, None]) & (blocks[None, :] > 0)
    next_of = jnp.min(jnp.where(later, eidx[None, :], N_EXPERTS), axis=1)
    next_exp = jnp.where(next_of < N_EXPERTS, next_of, -1)[block_exp]
    tabs = {
        "lo": i32(jnp.cumsum(run, axis=1) - run).reshape(-1),
        "go": i32(estart[None, :] + jnp.cumsum(run, axis=0) - run).reshape(-1),
        "n8": i32(run // SUBLANES).reshape(-1),
        "nt": i32(jnp.sum(run, axis=1) // SUBLANES),
        "zs": i32(jnp.concatenate([estart + rows, bend[-1:] * TM_MOE])),
        "zn": i32(jnp.concatenate([(blocks * TM_MOE - rows) // SUBLANES, nb - bend[-1:]])),
    }
    return i32(block_exp), n_used.reshape(1), i32(next_exp), tabs


def kernel(x, mem, norm_mix_g, w_in, conv_w, conv_b, rg_wa, rg_ba, rg_wx, rg_bx, rg_lambda, s5_lambda_re,
           s5_lambda_im, s5_b_re, s5_b_im, s5_c_re, s5_c_im, s5_d, s5_log_dt, s5_glu_w, s5_glu_b, w_out,
           norm_xa_g, mem_norm_g, xa_wq, xa_wkv, xa_wo, norm_ffn_g, router_w, router_b, exp_w_gu, exp_b_gu,
           exp_w_down, exp_b_down, final_norm_g):
    bsz, seq, d = x.shape
    mlen = mem.shape[1]
    n = bsz * seq
    row = lambda v: v.reshape(1, -1).astype(F32)

    p = _prep_mixer_params(min(T_MIX, seq) // SUBLANES, norm_mix_g, w_in, conv_w, conv_b, rg_wa, rg_ba, rg_wx,
                           rg_bx, rg_lambda, s5_lambda_re, s5_lambda_im, s5_b_re, s5_b_im, s5_c_re, s5_c_im,
                           s5_d, s5_log_dt, s5_glu_w, s5_glu_b, w_out)
    h1 = _mixer(x.reshape(n, d), bsz, seq, p)

    k, v = _kv(mem.reshape(bsz * mlen, d), bsz, mlen, row(mem_norm_g), xa_wkv.astype(BF16))
    h2 = _attn(h1, bsz, seq, mlen, row(norm_xa_g), xa_wq.astype(BF16), k, v, xa_wo.astype(BF16))

    wr_t = router_w.T.astype(F32)
    wr_hi = wr_t.astype(BF16)
    wr_lo = (wr_t - wr_hi.astype(F32)).astype(BF16)
    xn, lpos_t, gate_t, tcnt = _route(h2, row(norm_ffn_g), jnp.concatenate([wr_hi, wr_lo], axis=0),
                                      router_b.reshape(-1, 1).astype(F32))
    tile = min(T_TILE, n)
    n_tiles = n // tile
    run_pad = N_EXPERTS * (SUBLANES - 1)
    nloc = -(-(TOP_K * tile + run_pad) // 128) * 128
    n_slots = -(-(TOP_K * n + n_tiles * run_pad + N_EXPERTS * (TM_MOE - SUBLANES)) // TM_MOE) * TM_MOE
    block_exp, n_used, next_exp, tabs = _slot_tables(tcnt[:, :, 0], n_slots)
    xs = _dispatch(tabs, xn, lpos_t, n_slots, nloc)
    ys = _moe(block_exp, n_used, next_exp, xs, exp_w_gu, exp_b_gu.reshape(N_EXPERTS, 1, -1),
              exp_w_down, exp_b_down.reshape(N_EXPERTS, 1, -1))
    out = _combine(tabs, ys, h2, lpos_t.T, gate_t.T, row(final_norm_g), nloc)
    return out.reshape(bsz, seq, d)
```

```python
import math

import jax
import jax.numpy as jnp
from jax import lax
from jax.experimental import pallas as pl
from jax.experimental.pallas import tpu as pltpu

F32 = jnp.float32
BF16 = jnp.bfloat16

D_MODEL = 1024
D_RG = 512
D_S5 = 512
CONV_W = 4
RG_C = 8.0
S5_GROUPS = 32
S5_STATE = 64
N_STATE = S5_GROUPS * S5_STATE
XA_HEADS = 4
XA_HD = 256
N_EXPERTS = 32
TOP_K = 4
D_FF = 1024
SWIGLU_LIMIT = 7.0
SWIGLU_ALPHA = 1.702
NORM_EPS = 1e-6

SUBLANES = 8
T_MIX = 512
T_ATT = 1024
T_TILE = 512
TM_MOE = 512
VMEM_LIMIT = 56 * 1024 * 1024


def _rms(x, g):
    return x * lax.rsqrt(jnp.mean(x * x, axis=-1, keepdims=True) + NORM_EPS) * g


def _gelu_tanh(x):
    return 0.5 * x * (1.0 + jnp.tanh(math.sqrt(2.0 / math.pi) * (x + 0.044715 * (x * x * x))))


def _sigmoid(x):
    return 1.0 / (1.0 + jnp.exp(-x))


def _block_diag(blocks):
    n, r, c = blocks.shape
    eye = jnp.eye(n, dtype=blocks.dtype)
    return jnp.einsum('nrc,nm->nrmc', blocks, eye).reshape(n * r, n * c)


def _sublane_chain(al, e, carry, sub):
    for d in (1, 2, 4):
        al_sh = jnp.where(sub >= d, pltpu.roll(al, d, axis=0), 1.0)
        e_sh = jnp.where(sub >= d, pltpu.roll(e, d, axis=0), 0.0)
        e = al * e_sh + e
        al = al * al_sh
    s_end = e + al * carry
    start = jnp.where(sub == 0, carry, pltpu.roll(s_end, 1, axis=0))
    return start, s_end[SUBLANES - 1:SUBLANES, :]


def _mixer_kernel(x_ref, g_ref, win_ref, convw_ref, convb_ref, wg_ref, bg_ref, sp_ref,
                  bmat_ref, tab_ref, cmat_ref, dskip_ref, glu_ref, glub_ref, wout_ref,
                  o_ref,
                  xin, xout, sem_i, sem_o, xbuf, hist, a_s, b_s, rg_carry, sre, sim, c_re, c_im):
    ngrp = xin.shape[1]
    t = ngrp * SUBLANES
    nh = (CONV_W - 1) * SUBLANES
    c = pl.program_id(1)
    step = pl.program_id(0) * pl.num_programs(1) + c
    nsteps = pl.num_programs(0) * pl.num_programs(1)
    slot = lax.rem(step, 2)

    def in_copy(s, sl, i):
        return pltpu.make_async_copy(x_ref.at[pl.ds(s * t + i * ngrp, ngrp), :], xin.at[sl, :, i, :], sem_i.at[sl])

    def out_copy(s, sl, i):
        return pltpu.make_async_copy(xout.at[sl, :, i, :], o_ref.at[pl.ds(s * t + i * ngrp, ngrp), :], sem_o.at[sl])

    @pl.when(step == 0)
    def _():
        for i in range(SUBLANES):
            in_copy(0, 0, i).start()

    @pl.when(c == 0)
    def _():
        hist[...] = jnp.zeros_like(hist)
        rg_carry[...] = jnp.zeros_like(rg_carry)
        c_re[...] = jnp.zeros_like(c_re)
        c_im[...] = jnp.zeros_like(c_im)

    for i in range(SUBLANES):
        in_copy(step, slot, i).wait()

    @pl.when(step + 1 < nsteps)
    def _():
        for i in range(SUBLANES):
            in_copy(step + 1, 1 - slot, i).start()

    x = xin[slot].reshape(t, D_MODEL)
    xn = _rms(x, g_ref[...]).astype(BF16)
    proj = jnp.dot(xn, win_ref[...], preferred_element_type=F32)
    gate_rg = proj[:, :D_RG]
    x_rg = proj[:, D_RG:2 * D_RG]
    u = proj[:, 2 * D_RG:]

    sub = lax.broadcasted_iota(jnp.int32, (SUBLANES, D_RG), 0)
    cur_tail = x_rg[t - nh:, :]
    for gidx in range(CONV_W - 1):
        rs = slice(gidx * SUBLANES, (gidx + 1) * SUBLANES)
        xbuf[rs, :] = jnp.where(sub == 0, pltpu.roll(hist[rs, :], 1, axis=0), pltpu.roll(cur_tail[rs, :], 1, axis=0))
    hist[...] = cur_tail
    xbuf[nh:, :] = x_rg
    xc = convb_ref[...] + convw_ref[CONV_W - 1:CONV_W, :] * x_rg
    for k in range(CONV_W - 1):
        xc = xc + convw_ref[k:k + 1, :] * xbuf[k * SUBLANES:k * SUBLANES + t, :]
    gates = jnp.dot(xc.astype(BF16), wg_ref[...], preferred_element_type=F32) + bg_ref[...]
    r = _sigmoid(gates[:, :D_RG])
    ig = _sigmoid(gates[:, D_RG:])
    log_a = (-RG_C) * r * sp_ref[...]
    a = jnp.exp(log_a)
    mult = jnp.sqrt(1.0 - a * a)
    row = lax.broadcasted_iota(jnp.int32, (t, D_RG), 0)
    mult = jnp.where((row == 0) & (c == 0), 1.0, mult)
    a_s[...] = a
    b_s[...] = mult * (ig * xc)

    def rg_pass(store):
        def body(q, carry):
            h, ac = carry
            r0 = pl.multiple_of(q * SUBLANES, SUBLANES)
            aa = a_s[pl.ds(r0, SUBLANES), :]
            h = aa * h + b_s[pl.ds(r0, SUBLANES), :]
            if store:
                b_s[pl.ds(r0, SUBLANES), :] = h
                return h, ac
            return h, aa * ac
        return body

    zeros8 = jnp.zeros((SUBLANES, D_RG), F32)
    ones8 = jnp.ones((SUBLANES, D_RG), F32)
    e_loc, a_loc = lax.fori_loop(0, ngrp, rg_pass(False), (zeros8, ones8), unroll=4)
    start, last = _sublane_chain(a_loc, e_loc, rg_carry[...], sub)
    rg_carry[...] = last
    lax.fori_loop(0, ngrp, rg_pass(True), (start, ones8), unroll=4)
    y_rg = b_s[...] * _gelu_tanh(gate_rg)

    ub = u.astype(BF16)
    half = N_STATE // 2
    for hf in range(2):
        bu = jnp.dot(ub[:, hf * 256:(hf + 1) * 256], bmat_ref[hf], preferred_element_type=F32)
        sre[:, hf * half:(hf + 1) * half] = bu[:, :half]
        sim[:, hf * half:(hf + 1) * half] = bu[:, half:]

    lb = 1024
    subl = lax.broadcasted_iota(jnp.int32, (SUBLANES, lb), 0)
    for blk in range(N_STATE // lb):
        cs = slice(blk * lb, (blk + 1) * lb)
        ar = tab_ref[8, :, cs]
        ai = tab_ref[9, :, cs]

        def s5_pass(store, cs=cs, ar=ar, ai=ai):
            def body(q, carry):
                hr, hi = carry
                r0 = pl.multiple_of(q * SUBLANES, SUBLANES)
                nr = ar * hr - ai * hi + sre[pl.ds(r0, SUBLANES), cs]
                ni = ar * hi + ai * hr + sim[pl.ds(r0, SUBLANES), cs]
                if store:
                    sre[pl.ds(r0, SUBLANES), cs] = nr
                    sim[pl.ds(r0, SUBLANES), cs] = ni
                return nr, ni
            return body

        z8 = jnp.zeros((SUBLANES, lb), F32)
        er, ei = lax.fori_loop(0, ngrp, s5_pass(False), (z8, z8), unroll=4)
        for i, d in enumerate((1, 2, 4)):
            mr = tab_ref[2 * i, :, cs]
            mi = tab_ref[2 * i + 1, :, cs]
            rr = pltpu.roll(er, d, axis=0)
            ri = pltpu.roll(ei, d, axis=0)
            er, ei = er + (mr * rr - mi * ri), ei + (mr * ri + mi * rr)
        pr = tab_ref[6, :, cs]
        pi_ = tab_ref[7, :, cs]
        cr = c_re[:, cs]
        ci = c_im[:, cs]
        sr = er + (pr * cr - pi_ * ci)
        si = ei + (pr * ci + pi_ * cr)
        c_re[:, cs] = sr[SUBLANES - 1:SUBLANES, :]
        c_im[:, cs] = si[SUBLANES - 1:SUBLANES, :]
        st_r = jnp.where(subl == 0, cr, pltpu.roll(sr, 1, axis=0))
        st_i = jnp.where(subl == 0, ci, pltpu.roll(si, 1, axis=0))
        lax.fori_loop(0, ngrp, s5_pass(True), (st_r, st_i), unroll=4)

    ys = []
    for hf in range(2):
        hcat = jnp.concatenate([sre[:, hf * half:(hf + 1) * half], sim[:, hf * half:(hf + 1) * half]], axis=1)
        ys.append(jnp.dot(hcat.astype(BF16), cmat_ref[hf], preferred_element_type=F32))
    y = jnp.concatenate(ys, axis=1) + dskip_ref[...] * u
    z = _gelu_tanh(y)
    zg = _sigmoid(jnp.dot(z.astype(BF16), glu_ref[...], preferred_element_type=F32) + glub_ref[...])
    y_s5 = z * zg

    ycat = jnp.concatenate([y_rg, y_s5], axis=1).astype(BF16)
    res = x + jnp.dot(ycat, wout_ref[...], preferred_element_type=F32)

    @pl.when(step >= 2)
    def _():
        for i in range(SUBLANES):
            out_copy(step - 2, slot, i).wait()

    xout[slot] = res.reshape(ngrp, SUBLANES, D_MODEL)
    for i in range(SUBLANES):
        out_copy(step, slot, i).start()

    @pl.when(step == nsteps - 1)
    def _():
        for i in range(SUBLANES):
            out_copy(step, slot, i).wait()

        @pl.when(step >= 1)
        def _():
            for i in range(SUBLANES):
                out_copy(step - 1, 1 - slot, i).wait()


def _mixer(x2, bsz, seq, p):
    t = min(T_MIX, seq)
    nc = seq // t
    nh = (CONV_W - 1) * SUBLANES
    const = lambda shape: pl.BlockSpec(shape, lambda b, c: (0,) * len(shape))
    return pl.pallas_call(
        _mixer_kernel,
        out_shape=jax.ShapeDtypeStruct(x2.shape, F32),
        grid=(bsz, nc),
        in_specs=[
            pl.BlockSpec(memory_space=pl.ANY),
            const((1, D_MODEL)), const((D_MODEL, 3 * D_RG)), const((CONV_W, D_RG)), const((1, D_RG)),
            const((D_RG, 2 * D_RG)), const((1, 2 * D_RG)), const((1, D_RG)),
            const((2, 256, N_STATE)), const((10, SUBLANES, N_STATE)), const((2, N_STATE, 256)),
            const((1, D_S5)), const((D_S5, D_S5)), const((1, D_S5)), const((D_MODEL, D_MODEL)),
        ],
        out_specs=pl.BlockSpec(memory_space=pl.ANY),
        scratch_shapes=[
            pltpu.VMEM((2, t // SUBLANES, SUBLANES, D_MODEL), F32),
            pltpu.VMEM((2, t // SUBLANES, SUBLANES, D_MODEL), F32),
            pltpu.SemaphoreType.DMA((2,)), pltpu.SemaphoreType.DMA((2,)),
            pltpu.VMEM((t + nh, D_RG), F32), pltpu.VMEM((nh, D_RG), F32),
            pltpu.VMEM((t, D_RG), F32), pltpu.VMEM((t, D_RG), F32), pltpu.VMEM((1, D_RG), F32),
            pltpu.VMEM((t, N_STATE), F32), pltpu.VMEM((t, N_STATE), F32),
            pltpu.VMEM((1, N_STATE), F32), pltpu.VMEM((1, N_STATE), F32),
        ],
        compiler_params=pltpu.CompilerParams(
            dimension_semantics=("arbitrary", "arbitrary"), vmem_limit_bytes=VMEM_LIMIT),
        name="mixer",
    )(x2, p["g_mix"], p["w_in"], p["conv_w"], p["conv_b"], p["w_gates"], p["b_gates"], p["sp"],
      p["bmat"], p["s5tab"], p["cmat"], p["d_skip"], p["glu_w"], p["glu_b"], p["w_out"])


N_ROWS = 2


def _mixer2_kernel(x_ref, g_ref, win_ref, convw_ref, convb_ref, wg_ref, bg_ref, sp_ref,
                   bmat_ref, tab_ref, cmat_ref, dskip_ref, glu_ref, glub_ref, wout_ref,
                   o_ref,
                   xin, xout, sem_i, sem_o, xbuf, hist, a_s, b_s, rg_carry, sre, sim, c_re, c_im):
    ngrp = xin.shape[2]
    t = ngrp * SUBLANES
    nh = (CONV_W - 1) * SUBLANES
    nc = pl.num_programs(1)
    c = pl.program_id(1)
    step = pl.program_id(0) * nc + c
    nsteps = pl.num_programs(0) * nc
    slot = lax.rem(step, 2)

    def row_base(s, r):
        return ((N_ROWS * lax.div(s, nc) + r) * nc + lax.rem(s, nc)) * t

    def in_copy(s, sl, r, i):
        return pltpu.make_async_copy(x_ref.at[pl.ds(row_base(s, r) + i * ngrp, ngrp), :], xin.at[r, sl, :, i, :],
                                     sem_i.at[r, sl])

    def out_copy(s, sl, r, i):
        return pltpu.make_async_copy(xout.at[r, sl, :, i, :], o_ref.at[pl.ds(row_base(s, r) + i * ngrp, ngrp), :],
                                     sem_o.at[r, sl])

    @pl.when(step == 0)
    def _():
        for r in range(N_ROWS):
            for i in range(SUBLANES):
                in_copy(0, 0, r, i).start()

    @pl.when(c == 0)
    def _():
        hist[...] = jnp.zeros_like(hist)
        rg_carry[...] = jnp.zeros_like(rg_carry)
        c_re[...] = jnp.zeros_like(c_re)
        c_im[...] = jnp.zeros_like(c_im)

    for r in range(N_ROWS):
        for i in range(SUBLANES):
            in_copy(step, slot, r, i).wait()

    @pl.when(step + 1 < nsteps)
    def _():
        for r in range(N_ROWS):
            for i in range(SUBLANES):
                in_copy(step + 1, 1 - slot, r, i).start()

    @pl.when(step >= 2)
    def _():
        for r in range(N_ROWS):
            for i in range(SUBLANES):
                out_copy(step - 2, slot, r, i).wait()

    sub = lax.broadcasted_iota(jnp.int32, (SUBLANES, D_RG), 0)
    half = N_STATE // 2
    lb = 512
    subl = lax.broadcasted_iota(jnp.int32, (SUBLANES, lb), 0)
    grp = lambda q: slice(q * SUBLANES, (q + 1) * SUBLANES)

    def front(r):
        x = xin[r, slot].reshape(t, D_MODEL)
        xn = _rms(x, g_ref[...]).astype(BF16)
        proj = jnp.dot(xn, win_ref[...], preferred_element_type=F32)
        gate_rg = proj[:, :D_RG]
        x_rg = proj[:, D_RG:2 * D_RG]
        u = proj[:, 2 * D_RG:]
        cur_tail = x_rg[t - nh:, :]
        for gidx in range(CONV_W - 1):
            rs = slice(gidx * SUBLANES, (gidx + 1) * SUBLANES)
            xbuf[r, rs, :] = jnp.where(sub == 0, pltpu.roll(hist[r, rs, :], 1, axis=0),
                                       pltpu.roll(cur_tail[rs, :], 1, axis=0))
        hist[r] = cur_tail
        xbuf[r, nh:, :] = x_rg
        xc = convb_ref[...] + convw_ref[CONV_W - 1:CONV_W, :] * x_rg
        for k in range(CONV_W - 1):
            xc = xc + convw_ref[k:k + 1, :] * xbuf[r, k * SUBLANES:k * SUBLANES + t, :]
        gates = jnp.dot(xc.astype(BF16), wg_ref[...], preferred_element_type=F32) + bg_ref[...]
        rgate = _sigmoid(gates[:, :D_RG])
        igate = _sigmoid(gates[:, D_RG:])
        a = jnp.exp((-RG_C) * rgate * sp_ref[...])
        mult = jnp.sqrt(1.0 - a * a)
        row = lax.broadcasted_iota(jnp.int32, (t, D_RG), 0)
        mult = jnp.where((row == 0) & (c == 0), 1.0, mult)
        a_s[r] = a
        b_s[r] = mult * (igate * xc)
        ub = u.astype(BF16)
        for hf in range(2):
            bu = jnp.dot(ub[:, hf * 256:(hf + 1) * 256], bmat_ref[hf], preferred_element_type=F32)
            sre[r, :, hf * half:(hf + 1) * half] = bu[:, :half]
            sim[r, :, hf * half:(hf + 1) * half] = bu[:, half:]
        return x, gate_rg, u

    def scans(r):
        h = jnp.zeros((SUBLANES, D_RG), F32)
        ac = jnp.ones((SUBLANES, D_RG), F32)
        for q in range(ngrp):
            aa = a_s[r, grp(q), :]
            h = aa * h + b_s[r, grp(q), :]
            ac = aa * ac
        h, last = _sublane_chain(ac, h, rg_carry[r], sub)
        rg_carry[r] = last
        for q in range(ngrp):
            h = a_s[r, grp(q), :] * h + b_s[r, grp(q), :]
            b_s[r, grp(q), :] = h
        for blk in range(N_STATE // lb):
            cs = slice(blk * lb, (blk + 1) * lb)
            ar = tab_ref[8, :, cs]
            ai = tab_ref[9, :, cs]
            er = jnp.zeros((SUBLANES, lb), F32)
            ei = jnp.zeros((SUBLANES, lb), F32)
            for q in range(ngrp):
                er, ei = (ar * er - ai * ei + sre[r, grp(q), cs], ar * ei + ai * er + sim[r, grp(q), cs])
            for i, d in enumerate((1, 2, 4)):
                mr = tab_ref[2 * i, :, cs]
                mi = tab_ref[2 * i + 1, :, cs]
                rr = pltpu.roll(er, d, axis=0)
                ri = pltpu.roll(ei, d, axis=0)
                er, ei = er + (mr * rr - mi * ri), ei + (mr * ri + mi * rr)
            pr = tab_ref[6, :, cs]
            pi_ = tab_ref[7, :, cs]
            cr = c_re[r, :, cs]
            ci = c_im[r, :, cs]
            sr = er + (pr * cr - pi_ * ci)
            si = ei + (pr * ci + pi_ * cr)
            c_re[r, :, cs] = sr[SUBLANES - 1:SUBLANES, :]
            c_im[r, :, cs] = si[SUBLANES - 1:SUBLANES, :]
            hr = jnp.where(subl == 0, cr, pltpu.roll(sr, 1, axis=0))
            hi = jnp.where(subl == 0, ci, pltpu.roll(si, 1, axis=0))
            for q in range(ngrp):
                hr, hi = (ar * hr - ai * hi + sre[r, grp(q), cs], ar * hi + ai * hr + sim[r, grp(q), cs])
                sre[r, grp(q), cs] = hr
                sim[r, grp(q), cs] = hi

    def back(r, x, gate_rg, u):
        y_rg = b_s[r] * _gelu_tanh(gate_rg)
        ys = []
        for hf in range(2):
            hcat = jnp.concatenate([sre[r, :, hf * half:(hf + 1) * half], sim[r, :, hf * half:(hf + 1) * half]],
                                   axis=1)
            ys.append(jnp.dot(hcat.astype(BF16), cmat_ref[hf], preferred_element_type=F32))
        y = jnp.concatenate(ys, axis=1) + dskip_ref[...] * u
        z = _gelu_tanh(y)
        zg = _sigmoid(jnp.dot(z.astype(BF16), glu_ref[...], preferred_element_type=F32) + glub_ref[...])
        ycat = jnp.concatenate([y_rg, z * zg], axis=1).astype(BF16)
        res = x + jnp.dot(ycat, wout_ref[...], preferred_element_type=F32)
        xout[r, slot] = res.reshape(ngrp, SUBLANES, D_MODEL)
        for i in range(SUBLANES):
            out_copy(step, slot, r, i).start()

    kept = [front(r) for r in range(N_ROWS)]
    for r in range(N_ROWS):
        scans(r)
    for r in range(N_ROWS):
        back(r, *kept[r])

    @pl.when(step == nsteps - 1)
    def _():
        for r in range(N_ROWS):
            for i in range(SUBLANES):
                out_copy(step, slot, r, i).wait()

        @pl.when(step >= 1)
        def _():
            for r in range(N_ROWS):
                for i in range(SUBLANES):
                    out_copy(step - 1, 1 - slot, r, i).wait()


def _mixer2(x2, bsz, seq, p):
    t = min(T_MIX, seq)
    nc = seq // t
    nh = (CONV_W - 1) * SUBLANES
    assert bsz % N_ROWS == 0
    const = lambda shape: pl.BlockSpec(shape, lambda b, c: (0,) * len(shape), pipeline_mode=pl.Buffered(1))
    per_row = lambda *shape: pltpu.VMEM((N_ROWS,) + shape, F32)
    return pl.pallas_call(
        _mixer2_kernel,
        out_shape=jax.ShapeDtypeStruct(x2.shape, F32),
        grid=(bsz // N_ROWS, nc),
        in_specs=[
            pl.BlockSpec(memory_space=pl.ANY),
            const((1, D_MODEL)), const((D_MODEL, 3 * D_RG)), const((CONV_W, D_RG)), const((1, D_RG)),
            const((D_RG, 2 * D_RG)), const((1, 2 * D_RG)), const((1, D_RG)),
            const((2, 256, N_STATE)), const((10, SUBLANES, N_STATE)), const((2, N_STATE, 256)),
            const((1, D_S5)), const((D_S5, D_S5)), const((1, D_S5)), const((D_MODEL, D_MODEL)),
        ],
        out_specs=pl.BlockSpec(memory_space=pl.ANY),
        scratch_shapes=[
            per_row(2, t // SUBLANES, SUBLANES, D_MODEL), per_row(2, t // SUBLANES, SUBLANES, D_MODEL),
            pltpu.SemaphoreType.DMA((N_ROWS, 2)), pltpu.SemaphoreType.DMA((N_ROWS, 2)),
            per_row(t + nh, D_RG), per_row(nh, D_RG),
            per_row(t, D_RG), per_row(t, D_RG), per_row(1, D_RG),
            per_row(t, N_STATE), per_row(t, N_STATE),
            per_row(1, N_STATE), per_row(1, N_STATE),
        ],
        compiler_params=pltpu.CompilerParams(
            dimension_semantics=("arbitrary", "arbitrary"), vmem_limit_bytes=VMEM_LIMIT),
        name="mixer",
    )(x2, p["g_mix"], p["w_in"], p["conv_w"], p["conv_b"], p["w_gates"], p["b_gates"], p["sp"],
      p["bmat"], p["s5tab"], p["cmat"], p["d_skip"], p["glu_w"], p["glu_b"], p["w_out"])


def _kv_kernel(m_ref, g_ref, w_ref, k_ref, v_ref):
    mn = _rms(m_ref[...], g_ref[...]).astype(BF16)
    kv = jnp.dot(mn, w_ref[...], preferred_element_type=F32)
    k_ref[...] = kv[:, :D_MODEL].astype(BF16)
    v_ref[...] = kv[:, D_MODEL:].astype(BF16)


def _kv(mem2, bsz, mlen, g, wkv):
    return pl.pallas_call(
        _kv_kernel,
        out_shape=(jax.ShapeDtypeStruct(mem2.shape, BF16), jax.ShapeDtypeStruct(mem2.shape, BF16)),
        grid=(bsz,),
        in_specs=[pl.BlockSpec((mlen, D_MODEL), lambda b: (b, 0)),
                  pl.BlockSpec((1, D_MODEL), lambda b: (0, 0)),
                  pl.BlockSpec((D_MODEL, 2 * D_MODEL), lambda b: (0, 0))],
        out_specs=(pl.BlockSpec((mlen, D_MODEL), lambda b: (b, 0)),
                   pl.BlockSpec((mlen, D_MODEL), lambda b: (b, 0))),
        compiler_params=pltpu.CompilerParams(dimension_semantics=("arbitrary",), vmem_limit_bytes=VMEM_LIMIT),
        name="kv",
    )(mem2, g, wkv)


def _attn_kernel(h_ref, g_ref, wq_ref, k_ref, v_ref, wo_ref, o_ref):
    h = h_ref[...]
    xn = _rms(h, g_ref[...]).astype(BF16)
    q = jnp.dot(xn, wq_ref[...], preferred_element_type=F32).astype(BF16)
    outs = []
    for hd in range(XA_HEADS):
        cs = slice(hd * XA_HD, (hd + 1) * XA_HD)
        s = lax.dot_general(q[:, cs], k_ref[:, cs], (((1,), (1,)), ((), ())),
                            preferred_element_type=F32) * (XA_HD ** -0.5)
        s = s - jnp.max(s, axis=-1, keepdims=True)
        e = jnp.exp(s)
        pr = e / jnp.sum(e, axis=-1, keepdims=True)
        outs.append(jnp.dot(pr.astype(BF16), v_ref[:, cs], preferred_element_type=F32))
    o = jnp.concatenate(outs, axis=1).astype(BF16)
    o_ref[...] = h + jnp.dot(o, wo_ref[...], preferred_element_type=F32)


def _attn(h2, bsz, seq, mlen, g, wq, k, v, wo):
    t = min(T_ATT, seq)
    nc = seq // t
    return pl.pallas_call(
        _attn_kernel,
        out_shape=jax.ShapeDtypeStruct(h2.shape, F32),
        grid=(bsz, nc),
        in_specs=[pl.BlockSpec((t, D_MODEL), lambda b, c: (b * nc + c, 0)),
                  pl.BlockSpec((1, D_MODEL), lambda b, c: (0, 0)),
                  pl.BlockSpec((D_MODEL, D_MODEL), lambda b, c: (0, 0)),
                  pl.BlockSpec((mlen, D_MODEL), lambda b, c: (b, 0)),
                  pl.BlockSpec((mlen, D_MODEL), lambda b, c: (b, 0)),
                  pl.BlockSpec((D_MODEL, D_MODEL), lambda b, c: (0, 0))],
        out_specs=pl.BlockSpec((t, D_MODEL), lambda b, c: (b * nc + c, 0)),
        compiler_params=pltpu.CompilerParams(
            dimension_semantics=("arbitrary", "arbitrary"), vmem_limit_bytes=VMEM_LIMIT),
        name="attn",
    )(h2, g, wq, k, v, wo)


def _route_kernel(h_ref, g_ref, wr_ref, br_ref, xn_ref, lpos_ref, gate_ref, tcnt_ref):
    t = h_ref.shape[0]
    xn = _rms(h_ref[...], g_ref[...])
    x_hi = xn.astype(BF16)
    x_lo = (xn - x_hi.astype(F32)).astype(BF16)
    xn_ref[...] = x_hi
    dims = (((1,), (1,)), ((), ()))
    p_hi = lax.dot_general(wr_ref[...], x_hi, dims, preferred_element_type=F32)
    p_lo = lax.dot_general(wr_ref[0:N_EXPERTS, :], x_lo, dims, preferred_element_type=F32)
    lg = p_hi[0:N_EXPERTS, :] + (p_hi[N_EXPERTS:, :] + p_lo) + br_ref[...]
    erow = lax.broadcasted_iota(jnp.int32, (N_EXPERTS, t), 0)
    vals, hots = [], []
    for _ in range(TOP_K):
        m = jnp.max(lg, axis=0, keepdims=True)
        ix = jnp.min(jnp.where(lg == m, erow, N_EXPERTS), axis=0, keepdims=True)
        hot = erow == ix
        lg = jnp.where(hot, -jnp.inf, lg)
        vals.append(m)
        hots.append(hot)
    es = [jnp.exp(v - vals[0]) for v in vals]
    den = es[0] + es[1] + es[2] + es[3]
    sel = jnp.zeros((N_EXPERTS, t), F32)
    for hot in hots:
        sel = sel + jnp.where(hot, 1.0, 0.0)
    tri = jnp.where(lax.broadcasted_iota(jnp.int32, (t, t), 0) < lax.broadcasted_iota(jnp.int32, (t, t), 1),
                    1.0, 0.0).astype(BF16)
    rank = jnp.dot(sel.astype(BF16), tri, preferred_element_type=F32)
    cnt = jnp.sum(sel, axis=1, keepdims=True)
    run8 = jnp.floor((cnt + (SUBLANES - 1.0)) * (1.0 / SUBLANES))
    below = jnp.where(lax.broadcasted_iota(jnp.int32, (N_EXPERTS, N_EXPERTS), 1)
                      < lax.broadcasted_iota(jnp.int32, (N_EXPERTS, N_EXPERTS), 0), 1.0, 0.0).astype(BF16)
    start8 = jnp.dot(below, jnp.broadcast_to(run8, (N_EXPERTS, 128)).astype(BF16), preferred_element_type=F32)
    pos = rank + SUBLANES * start8[:, 0:1]
    for k in range(TOP_K):
        gate_ref[k:k + 1, :] = es[k] / den
        lpos_ref[k:k + 1, :] = jnp.sum(jnp.where(hots[k], pos, 0.0), axis=0, keepdims=True).astype(jnp.int32)
    tcnt_ref[0] = jnp.broadcast_to(cnt, (N_EXPERTS, 128)).astype(jnp.int32)


def _route(h2, g, wr_t, br):
    n = h2.shape[0]
    t = min(T_TILE, n)
    return pl.pallas_call(
        _route_kernel,
        out_shape=(jax.ShapeDtypeStruct((n, D_MODEL), BF16),
                   jax.ShapeDtypeStruct((TOP_K, n), jnp.int32),
                   jax.ShapeDtypeStruct((TOP_K, n), F32),
                   jax.ShapeDtypeStruct((n // t, N_EXPERTS, 128), jnp.int32)),
        grid=(n // t,),
        in_specs=[pl.BlockSpec((t, D_MODEL), lambda i: (i, 0)),
                  pl.BlockSpec((1, D_MODEL), lambda i: (0, 0)),
                  pl.BlockSpec((2 * N_EXPERTS, D_MODEL), lambda i: (0, 0)),
                  pl.BlockSpec((N_EXPERTS, 1), lambda i: (0, 0))],
        out_specs=(pl.BlockSpec((t, D_MODEL), lambda i: (i, 0)),
                   pl.BlockSpec((TOP_K, t), lambda i: (0, i)),
                   pl.BlockSpec((TOP_K, t), lambda i: (0, i)),
                   pl.BlockSpec((1, N_EXPERTS, 128), lambda i: (i, 0, 0))),
        compiler_params=pltpu.CompilerParams(dimension_semantics=("arbitrary",), vmem_limit_bytes=VMEM_LIMIT),
        name="route",
    )(h2, g, wr_t, br)


RUN_CHUNK = 64
WAIT_CHUNK = 256


def _run_copies(i, lo_ref, go_ref, n8_ref, make_copy):
    def expert(e, carry):
        idx = i * N_EXPERTS + e
        lo = lo_ref[idx]
        go = go_ref[idx]
        n8 = n8_ref[idx]
        nbig = lax.shift_right_logical(n8, 3)

        def big(j, c):
            make_copy(pl.multiple_of(lo + j * RUN_CHUNK, SUBLANES), pl.multiple_of(go + j * RUN_CHUNK, SUBLANES),
                      RUN_CHUNK).start()
            return c

        lax.fori_loop(0, nbig, big, 0)
        off = nbig * RUN_CHUNK
        for rows in (32, 16, 8):
            has = (n8 & (rows // SUBLANES)) != 0

            @pl.when(has)
            def _(off=off, rows=rows):
                make_copy(pl.multiple_of(lo + off, SUBLANES), pl.multiple_of(go + off, SUBLANES), rows).start()

            off = off + jnp.where(has, rows, 0)
        return carry

    lax.fori_loop(0, N_EXPERTS, expert, 0)


def _wait_rows(n8, make_wait):
    def big(j, c):
        make_wait(WAIT_CHUNK).wait()
        return c

    lax.fori_loop(0, lax.shift_right_logical(n8, 5), big, 0)
    for rows in (128, 64, 32, 16, 8):
        @pl.when((n8 & (rows // SUBLANES)) != 0)
        def _(rows=rows):
            make_wait(rows).wait()


def _dispatch_kernel(lo_ref, go_ref, n8_ref, nt_ref, zs_ref, zn_ref, xn_ref, lpos_ref, xs_hbm, loc, zbuf, sem,
                     sem_z):
    i = pl.program_id(0)
    last = pl.num_programs(0) - 1
    slot = lax.rem(i, 2)
    nloc, t = loc.shape[1], xn_ref.shape[0]

    def row_copy(sl, lo, go, rows):
        return pltpu.make_async_copy(loc.at[sl, pl.ds(lo, rows), :], xs_hbm.at[pl.ds(go, rows), :], sem.at[sl])

    def wait_copies(sl, count):
        _wait_rows(count, lambda rows: row_copy(sl, 0, 0, rows))

    @pl.when(i >= 2)
    def _():
        wait_copies(slot, nt_ref[jnp.maximum(i - 2, 0)])

    mc = 256
    liota = lax.broadcasted_iota(jnp.int32, (mc, t), 0)
    xn = xn_ref[...]
    for r in range(nloc // mc):
        pick = jnp.zeros((mc, t), F32)
        for k in range(TOP_K):
            pick = jnp.where(liota == lpos_ref[k:k + 1, :] - r * mc, 1.0, pick)
        loc[slot, r * mc:(r + 1) * mc, :] = jnp.dot(pick.astype(BF16), xn, preferred_element_type=F32)
    _run_copies(i, lo_ref, go_ref, n8_ref, lambda lo, go, rows: row_copy(slot, lo, go, rows))

    @pl.when(i == last)
    def _():
        wait_copies(slot, nt_ref[i])

        @pl.when(i >= 1)
        def _():
            wait_copies(1 - slot, nt_ref[jnp.maximum(i - 1, 0)])

        zbuf[...] = jnp.zeros_like(zbuf)

        def zero_copy(go):
            return pltpu.make_async_copy(zbuf.at[pl.ds(0, SUBLANES), :], xs_hbm.at[pl.ds(go, SUBLANES), :], sem_z)

        def expert(e, total):
            def chunk(j, carry):
                zero_copy(pl.multiple_of(zs_ref[e] + j * SUBLANES, SUBLANES)).start()
                return carry
            lax.fori_loop(0, zn_ref[e], chunk, 0)
            return total + zn_ref[e]

        total = lax.fori_loop(0, N_EXPERTS, expert, 0)

        def wait_zero(j, carry):
            zero_copy(0).wait()
            return carry

        lax.fori_loop(0, total, wait_zero, 0)

        def block_copy(go):
            return pltpu.make_async_copy(zbuf, xs_hbm.at[pl.ds(go, TM_MOE), :], sem_z)

        def start_block(j, carry):
            block_copy(pl.multiple_of(zs_ref[N_EXPERTS] + j * TM_MOE, TM_MOE)).start()
            return carry

        def wait_block(j, carry):
            block_copy(0).wait()
            return carry

        lax.fori_loop(0, zn_ref[N_EXPERTS], start_block, 0)
        lax.fori_loop(0, zn_ref[N_EXPERTS], wait_block, 0)


def _dispatch(tabs, xn, lpos_t, n_slots, nloc):
    n = xn.shape[0]
    t = min(T_TILE, n)
    grid_spec = pltpu.PrefetchScalarGridSpec(
        num_scalar_prefetch=6,
        grid=(n // t,),
        in_specs=[pl.BlockSpec((t, D_MODEL), lambda i, *_: (i, 0)),
                  pl.BlockSpec((TOP_K, t), lambda i, *_: (0, i))],
        out_specs=pl.BlockSpec(memory_space=pl.ANY),
        scratch_shapes=[pltpu.VMEM((2, nloc, D_MODEL), F32), pltpu.VMEM((TM_MOE, D_MODEL), F32),
                        pltpu.SemaphoreType.DMA((2,)), pltpu.SemaphoreType.DMA],
    )
    return pl.pallas_call(
        _dispatch_kernel,
        out_shape=jax.ShapeDtypeStruct((n_slots, D_MODEL), F32),
        grid_spec=grid_spec,
        compiler_params=pltpu.CompilerParams(dimension_semantics=("arbitrary",), vmem_limit_bytes=VMEM_LIMIT),
        name="dispatch",
    )(tabs["lo"], tabs["go"], tabs["n8"], tabs["nt"], tabs["zs"], tabs["zn"], xn, lpos_t)


def _moe_kernel(bexp_ref, nused_ref, nexp_ref, x_ref, wgu_hbm, bgu_ref, wd_hbm, bd_ref, o_ref,
                wgu_f, wd_f, wgu_s, wd_s, sem_w):
    b = pl.program_id(0)
    prev = bexp_ref[jnp.maximum(b - 1, 0)]
    fresh = (b == 0) | (bexp_ref[b] != prev)

    def fetch(ex):
        return (pltpu.make_async_copy(wgu_hbm.at[ex], wgu_f, sem_w.at[0]),
                pltpu.make_async_copy(wd_hbm.at[ex], wd_f, sem_w.at[1]))

    @pl.when(b == 0)
    def _():
        for cp in fetch(bexp_ref[0]):
            cp.start()

    @pl.when(fresh & (b < nused_ref[0]))
    def _():
        for cp in fetch(bexp_ref[b]):
            cp.wait()
        rows = 128
        for r0 in range(0, D_MODEL, rows):
            wgu_s[r0:r0 + rows, :] = wgu_f[r0:r0 + rows, :].astype(BF16)
        for r0 in range(0, D_FF, rows):
            wd_s[r0:r0 + rows, :] = wd_f[r0:r0 + rows, :].astype(BF16)

        @pl.when(nexp_ref[b] >= 0)
        def _():
            for cp in fetch(nexp_ref[b]):
                cp.start()

    @pl.when(b < nused_ref[0])
    def _():
        xb = x_ref[...].astype(BF16)
        gu = jnp.dot(xb, wgu_s[...], preferred_element_type=F32) + bgu_ref[0]
        g = jnp.minimum(gu[:, :D_FF], SWIGLU_LIMIT)
        u = jnp.clip(gu[:, D_FF:], -SWIGLU_LIMIT, SWIGLU_LIMIT)
        h = (u + 1.0) * (g * _sigmoid(SWIGLU_ALPHA * g))
        o_ref[...] = jnp.dot(h.astype(BF16), wd_s[...], preferred_element_type=F32) + bd_ref[0]

    @pl.when(b >= nused_ref[0])
    def _():
        o_ref[...] = jnp.zeros_like(o_ref)


def _moe(block_exp, n_used, next_exp, xs, w_gu, b_gu, w_down, b_down):
    n_slots = xs.shape[0]
    nb = n_slots // TM_MOE
    bspec = lambda shape: pl.BlockSpec(shape, lambda b, be, nu, ne: (be[b], 0, 0))
    grid_spec = pltpu.PrefetchScalarGridSpec(
        num_scalar_prefetch=3,
        grid=(nb,),
        in_specs=[
            pl.BlockSpec((TM_MOE, D_MODEL), lambda b, be, nu, ne: (jnp.minimum(b, nu[0] - 1), 0)),
            pl.BlockSpec(memory_space=pl.ANY), bspec((1, 1, 2 * D_FF)),
            pl.BlockSpec(memory_space=pl.ANY), bspec((1, 1, D_MODEL)),
        ],
        out_specs=pl.BlockSpec((TM_MOE, D_MODEL), lambda b, be, nu, ne: (b, 0)),
        scratch_shapes=[pltpu.VMEM((D_MODEL, 2 * D_FF), F32), pltpu.VMEM((D_FF, D_MODEL), F32),
                        pltpu.VMEM((D_MODEL, 2 * D_FF), BF16), pltpu.VMEM((D_FF, D_MODEL), BF16),
                        pltpu.SemaphoreType.DMA((2,))],
    )
    return pl.pallas_call(
        _moe_kernel,
        out_shape=jax.ShapeDtypeStruct((n_slots, D_MODEL), F32),
        grid_spec=grid_spec,
        compiler_params=pltpu.CompilerParams(dimension_semantics=("arbitrary",), vmem_limit_bytes=VMEM_LIMIT),
        name="moe",
    )(block_exp, n_used, next_exp, xs, w_gu, b_gu, w_down, b_down)


def _combine_kernel(lo_ref, go_ref, n8_ref, nt_ref, ys_hbm, h_ref, lpos_ref, gate_ref, g_ref, o_ref, loc, lpb,
                    gtb, sem):
    i = pl.program_id(0)
    slot = lax.rem(i, 2)
    nloc, t = loc.shape[1], h_ref.shape[0]

    def row_copy(sl, lo, go, rows):
        return pltpu.make_async_copy(ys_hbm.at[pl.ds(go, rows), :], loc.at[sl, pl.ds(lo, rows), :], sem.at[sl])

    def fetch(tile, sl):
        _run_copies(tile, lo_ref, go_ref, n8_ref, lambda lo, go, rows: row_copy(sl, lo, go, rows))

        def zero_rows(j, carry):
            r0 = pl.multiple_of((nt_ref[tile] + j) * SUBLANES, SUBLANES)
            loc[sl, pl.ds(r0, SUBLANES), :] = jnp.zeros((SUBLANES, D_MODEL), F32)
            return carry

        lax.fori_loop(0, nloc // SUBLANES - nt_ref[tile], zero_rows, 0)

    @pl.when(i == 0)
    def _():
        fetch(0, 0)

    @pl.when(i + 1 < pl.num_programs(0))
    def _():
        fetch(i + 1, 1 - slot)

    _wait_rows(nt_ref[i], lambda rows: row_copy(slot, 0, 0, rows))
    kc = 256
    lane = lax.broadcasted_iota(jnp.int32, (t, 128), 1)
    for k in range(TOP_K):
        lpb[k] = jnp.broadcast_to(lpos_ref[:, k:k + 1], (t, 128))
        gtb[k] = jnp.broadcast_to(gate_ref[:, k:k + 1], (t, 128))
    acc = h_ref[...]
    for c in range(nloc // kc):
        pieces = []
        for j in range(kc // 128):
            li = lane + (c * kc + j * 128)
            w = jnp.zeros((t, 128), F32)
            for k in range(TOP_K):
                w = jnp.where(li == lpb[k], gtb[k], w)
            pieces.append(w)
        wc = jnp.concatenate(pieces, axis=1).astype(BF16)
        acc = acc + jnp.dot(wc, loc[slot, c * kc:(c + 1) * kc, :].astype(BF16), preferred_element_type=F32)
    o_ref[...] = _rms(acc, g_ref[...])


def _combine(tabs, ys, h2, lpos_tok, gates_tok, g, nloc):
    n = h2.shape[0]
    t = min(T_TILE, n)
    grid_spec = pltpu.PrefetchScalarGridSpec(
        num_scalar_prefetch=4,
        grid=(n // t,),
        in_specs=[pl.BlockSpec(memory_space=pl.ANY),
                  pl.BlockSpec((t, D_MODEL), lambda i, *_: (i, 0)),
                  pl.BlockSpec((t, TOP_K), lambda i, *_: (i, 0)),
                  pl.BlockSpec((t, TOP_K), lambda i, *_: (i, 0)),
                  pl.BlockSpec((1, D_MODEL), lambda i, *_: (0, 0))],
        out_specs=pl.BlockSpec((t, D_MODEL), lambda i, *_: (i, 0)),
        scratch_shapes=[pltpu.VMEM((2, nloc, D_MODEL), F32), pltpu.VMEM((TOP_K, t, 128), jnp.int32),
                        pltpu.VMEM((TOP_K, t, 128), F32), pltpu.SemaphoreType.DMA((2,))],
    )
    return pl.pallas_call(
        _combine_kernel,
        out_shape=jax.ShapeDtypeStruct((n, D_MODEL), F32),
        grid_spec=grid_spec,
        compiler_params=pltpu.CompilerParams(dimension_semantics=("arbitrary",), vmem_limit_bytes=VMEM_LIMIT),
        name="combine",
    )(tabs["lo"], tabs["go"], tabs["n8"], tabs["nt"], ys, h2, lpos_tok, gates_tok, g)


def _cmul(a, b):
    return a[0] * b[0] - a[1] * b[1], a[0] * b[1] + a[1] * b[0]


def _prep_mixer_params(sub_len, norm_mix_g, w_in, conv_w, conv_b, rg_wa, rg_ba, rg_wx, rg_bx, rg_lambda,
                       s5_lambda_re, s5_lambda_im, s5_b_re, s5_b_im, s5_c_re, s5_c_im, s5_d, s5_log_dt,
                       s5_glu_w, s5_glu_b, w_out):
    row = lambda v: v.reshape(1, -1).astype(F32)
    dt = jnp.exp(s5_log_dt)[:, None]
    mag = jnp.exp(dt * s5_lambda_re)
    abar_re = mag * jnp.cos(dt * s5_lambda_im)
    abar_im = mag * jnp.sin(dt * s5_lambda_im)
    den = s5_lambda_re * s5_lambda_re + s5_lambda_im * s5_lambda_im
    num_re = abar_re - 1.0
    coef_re = (num_re * s5_lambda_re + abar_im * s5_lambda_im) / den
    coef_im = (abar_im * s5_lambda_re - num_re * s5_lambda_im) / den
    bbar_re = coef_re[..., None] * s5_b_re - coef_im[..., None] * s5_b_im
    bbar_im = coef_re[..., None] * s5_b_im + coef_im[..., None] * s5_b_re
    a1 = (abar_re.reshape(-1), abar_im.reshape(-1))
    aq = a1
    for _ in range(int(math.log2(sub_len))):
        aq = _cmul(aq, aq)
    pw = [aq]
    for _ in range(SUBLANES - 1):
        pw.append(_cmul(pw[-1], aq))
    sub = jnp.arange(SUBLANES)[:, None]
    tabs = []
    for d in (1, 2, 4):
        keep = (sub >= d).astype(F32)
        tabs += [keep * pw[d - 1][0][None, :], keep * pw[d - 1][1][None, :]]
    tabs += [jnp.stack([p_[0] for p_ in pw]), jnp.stack([p_[1] for p_ in pw])]
    tabs += [jnp.broadcast_to(a1[0][None, :], (SUBLANES, N_STATE)),
             jnp.broadcast_to(a1[1][None, :], (SUBLANES, N_STATE))]
    s5tab = jnp.stack(tabs).astype(F32)
    gh = S5_GROUPS // 2
    bmat, cmat = [], []
    for hf in range(2):
        gs = slice(hf * gh, (hf + 1) * gh)
        b_re = _block_diag(jnp.transpose(bbar_re[gs], (0, 2, 1)))
        b_im = _block_diag(jnp.transpose(bbar_im[gs], (0, 2, 1)))
        bmat.append(jnp.concatenate([b_re, b_im], axis=1))
        cr = _block_diag(jnp.transpose(s5_c_re[gs], (0, 2, 1)))
        ci = _block_diag(jnp.transpose(s5_c_im[gs], (0, 2, 1)))
        cmat.append(jnp.concatenate([cr, -ci], axis=0))
    return {
        "g_mix": row(norm_mix_g), "w_in": w_in.astype(BF16), "conv_w": conv_w.astype(F32), "conv_b": row(conv_b),
        "w_gates": jnp.concatenate([_block_diag(rg_wa), _block_diag(rg_wx)], axis=1).astype(BF16),
        "b_gates": jnp.concatenate([rg_ba, rg_bx]).reshape(1, -1).astype(F32),
        "sp": row(jax.nn.softplus(-rg_lambda)),
        "bmat": jnp.stack(bmat).astype(BF16), "s5tab": s5tab, "cmat": jnp.stack(cmat).astype(BF16),
        "d_skip": row(s5_d), "glu_w": _block_diag(s5_glu_w).astype(BF16), "glu_b": row(s5_glu_b),
        "w_out": w_out.astype(BF16),
    }


def _slot_tables(tcnt, n_slots):
    i32 = lambda v: v.astype(jnp.int32)
    run = ((tcnt + SUBLANES - 1) // SUBLANES) * SUBLANES
    rows = jnp.sum(run, axis=0)
    blocks = (rows + TM_MOE - 1) // TM_MOE
    bend = jnp.cumsum(blocks)
    estart = (bend - blocks) * TM_MOE
    nb = n_slots // TM_MOE
    n_used = i32(bend[-1])
    block_exp = jnp.minimum(jnp.sum(jnp.arange(nb)[:, None] >= bend[None, :], axis=1), N_EXPERTS - 1)
    block_exp = jnp.where(jnp.arange(nb) < n_used, block_exp, block_exp[jnp.maximum(n_used - 1, 0)])
    eidx = jnp.arange(N_EXPERTS)
    later = (eidx[None, :] > eidx[:, None]) & (blocks[None, :] > 0)
    next_of = jnp.min(jnp.where(later, eidx[None, :], N_EXPERTS), axis=1)
    next_exp = jnp.where(next_of < N_EXPERTS, next_of, -1)[block_exp]
    tabs = {
        "lo": i32(jnp.cumsum(run, axis=1) - run).reshape(-1),
        "go": i32(estart[None, :] + jnp.cumsum(run, axis=0) - run).reshape(-1),
        "n8": i32(run // SUBLANES).reshape(-1),
        "nt": i32(jnp.sum(run, axis=1) // SUBLANES),
        "zs": i32(jnp.concatenate([estart + rows, bend[-1:] * TM_MOE])),
        "zn": i32(jnp.concatenate([(blocks * TM_MOE - rows) // SUBLANES, nb - bend[-1:]])),
    }
    return i32(block_exp), n_used.reshape(1), i32(next_exp), tabs


def kernel(x, mem, norm_mix_g, w_in, conv_w, conv_b, rg_wa, rg_ba, rg_wx, rg_bx, rg_lambda, s5_lambda_re,
           s5_lambda_im, s5_b_re, s5_b_im, s5_c_re, s5_c_im, s5_d, s5_log_dt, s5_glu_w, s5_glu_b, w_out,
           norm_xa_g, mem_norm_g, xa_wq, xa_wkv, xa_wo, norm_ffn_g, router_w, router_b, exp_w_gu, exp_b_gu,
           exp_w_down, exp_b_down, final_norm_g):
    bsz, seq, d = x.shape
    mlen = mem.shape[1]
    n = bsz * seq
    row = lambda v: v.reshape(1, -1).astype(F32)

    p = _prep_mixer_params(min(T_MIX, seq) // SUBLANES, norm_mix_g, w_in, conv_w, conv_b, rg_wa, rg_ba, rg_wx,
                           rg_bx, rg_lambda, s5_lambda_re, s5_lambda_im, s5_b_re, s5_b_im, s5_c_re, s5_c_im,
                           s5_d, s5_log_dt, s5_glu_w, s5_glu_b, w_out)
    h1 = _mixer2(x.reshape(n, d), bsz, seq, p)

    k, v = _kv(mem.reshape(bsz * mlen, d), bsz, mlen, row(mem_norm_g), xa_wkv.astype(BF16))
    h2 = _attn(h1, bsz, seq, mlen, row(norm_xa_g), xa_wq.astype(BF16), k, v, xa_wo.astype(BF16))

    wr_t = router_w.T.astype(F32)
    wr_hi = wr_t.astype(BF16)
    wr_lo = (wr_t - wr_hi.astype(F32)).astype(BF16)
    xn, lpos_t, gate_t, tcnt = _route(h2, row(norm_ffn_g), jnp.concatenate([wr_hi, wr_lo], axis=0),
                                      router_b.reshape(-1, 1).astype(F32))
    tile = min(T_TILE, n)
    n_tiles = n // tile
    run_pad = N_EXPERTS * (SUBLANES - 1)
    nloc = -(-(TOP_K * tile + run_pad) // 128) * 128
    n_slots = -(-(TOP_K * n + n_tiles * run_pad + N_EXPERTS * (TM_MOE - SUBLANES)) // TM_MOE) * TM_MOE
    block_exp, n_used, next_exp, tabs = _slot_tables(tcnt[:, :, 0], n_slots)
    xs = _dispatch(tabs, xn, lpos_t, n_slots, nloc)
    ys = _moe(block_exp, n_used, next_exp, xs, exp_w_gu, exp_b_gu.reshape(N_EXPERTS, 1, -1),
              exp_w_down, exp_b_down.reshape(N_EXPERTS, 1, -1))
    out = _combine(tabs, ys, h2, lpos_t.T, gate_t.T, row(final_norm_g), nloc)
    return out.reshape(bsz, seq, d)
```

```python
import math

import jax
import jax.numpy as jnp
from jax import lax
from jax.experimental import pallas as pl
from jax.experimental.pallas import tpu as pltpu

F32 = jnp.float32
BF16 = jnp.bfloat16

D_MODEL = 1024
D_RG = 512
D_S5 = 512
CONV_W = 4
RG_C = 8.0
S5_GROUPS = 32
S5_STATE = 64
N_STATE = S5_GROUPS * S5_STATE
XA_HEADS = 4
XA_HD = 256
N_EXPERTS = 32
TOP_K = 4
D_FF = 1024
SWIGLU_LIMIT = 7.0
SWIGLU_ALPHA = 1.702
NORM_EPS = 1e-6

SUBLANES = 8
T_MIX = 512
T_ATT = 1024
T_TILE = 512
TM_MOE = 512
VMEM_LIMIT = 56 * 1024 * 1024


def _rms(x, g):
    return x * lax.rsqrt(jnp.mean(x * x, axis=-1, keepdims=True) + NORM_EPS) * g


def _gelu_tanh(x):
    return 0.5 * x * (1.0 + jnp.tanh(math.sqrt(2.0 / math.pi) * (x + 0.044715 * (x * x * x))))


def _sigmoid(x):
    return 1.0 / (1.0 + jnp.exp(-x))


def _block_diag(blocks):
    n, r, c = blocks.shape
    eye = jnp.eye(n, dtype=blocks.dtype)
    return jnp.einsum('nrc,nm->nrmc', blocks, eye).reshape(n * r, n * c)


def _sublane_chain(al, e, carry, sub):
    for d in (1, 2, 4):
        al_sh = jnp.where(sub >= d, pltpu.roll(al, d, axis=0), 1.0)
        e_sh = jnp.where(sub >= d, pltpu.roll(e, d, axis=0), 0.0)
        e = al * e_sh + e
        al = al * al_sh
    s_end = e + al * carry
    start = jnp.where(sub == 0, carry, pltpu.roll(s_end, 1, axis=0))
    return start, s_end[SUBLANES - 1:SUBLANES, :]


N_ROWS = 2


def _mixer_kernel(x_ref, g_ref, win_ref, convw_ref, convb_ref, wg_ref, bg_ref, sp_ref,
                   bmat_ref, tab_ref, cmat_ref, dskip_ref, glu_ref, glub_ref, wout_ref,
                   o_ref,
                   xin, xout, sem_i, sem_o, xbuf, hist, a_s, b_s, rg_carry, sre, sim, c_re, c_im):
    ngrp = xin.shape[2]
    t = ngrp * SUBLANES
    nh = (CONV_W - 1) * SUBLANES
    nc = pl.num_programs(1)
    c = pl.program_id(1)
    step = pl.program_id(0) * nc + c
    nsteps = pl.num_programs(0) * nc
    slot = lax.rem(step, 2)

    def row_base(s, r):
        return ((N_ROWS * lax.div(s, nc) + r) * nc + lax.rem(s, nc)) * t

    def in_copy(s, sl, r, i):
        return pltpu.make_async_copy(x_ref.at[pl.ds(row_base(s, r) + i * ngrp, ngrp), :], xin.at[r, sl, :, i, :],
                                     sem_i.at[r, sl])

    def out_copy(s, sl, r, i):
        return pltpu.make_async_copy(xout.at[r, sl, :, i, :], o_ref.at[pl.ds(row_base(s, r) + i * ngrp, ngrp), :],
                                     sem_o.at[r, sl])

    @pl.when(step == 0)
    def _():
        for r in range(N_ROWS):
            for i in range(SUBLANES):
                in_copy(0, 0, r, i).start()

    @pl.when(c == 0)
    def _():
        hist[...] = jnp.zeros_like(hist)
        rg_carry[...] = jnp.zeros_like(rg_carry)
        c_re[...] = jnp.zeros_like(c_re)
        c_im[...] = jnp.zeros_like(c_im)

    for r in range(N_ROWS):
        for i in range(SUBLANES):
            in_copy(step, slot, r, i).wait()

    @pl.when(step + 1 < nsteps)
    def _():
        for r in range(N_ROWS):
            for i in range(SUBLANES):
                in_copy(step + 1, 1 - slot, r, i).start()

    @pl.when(step >= 2)
    def _():
        for r in range(N_ROWS):
            for i in range(SUBLANES):
                out_copy(step - 2, slot, r, i).wait()

    sub = lax.broadcasted_iota(jnp.int32, (SUBLANES, D_RG), 0)
    half = N_STATE // 2
    lb = 512
    subl = lax.broadcasted_iota(jnp.int32, (SUBLANES, lb), 0)
    grp = lambda q: slice(q * SUBLANES, (q + 1) * SUBLANES)

    def front(r):
        x = xin[r, slot].reshape(t, D_MODEL)
        xn = _rms(x, g_ref[...]).astype(BF16)
        proj = jnp.dot(xn, win_ref[...], preferred_element_type=F32)
        gate_rg = proj[:, :D_RG]
        x_rg = proj[:, D_RG:2 * D_RG]
        u = proj[:, 2 * D_RG:]
        cur_tail = x_rg[t - nh:, :]
        for gidx in range(CONV_W - 1):
            rs = slice(gidx * SUBLANES, (gidx + 1) * SUBLANES)
            xbuf[r, rs, :] = jnp.where(sub == 0, pltpu.roll(hist[r, rs, :], 1, axis=0),
                                       pltpu.roll(cur_tail[rs, :], 1, axis=0))
        hist[r] = cur_tail
        xbuf[r, nh:, :] = x_rg
        xc = convb_ref[...] + convw_ref[CONV_W - 1:CONV_W, :] * x_rg
        for k in range(CONV_W - 1):
            xc = xc + convw_ref[k:k + 1, :] * xbuf[r, k * SUBLANES:k * SUBLANES + t, :]
        gates = jnp.dot(xc.astype(BF16), wg_ref[...], preferred_element_type=F32) + bg_ref[...]
        rgate = _sigmoid(gates[:, :D_RG])
        igate = _sigmoid(gates[:, D_RG:])
        a = jnp.exp((-RG_C) * rgate * sp_ref[...])
        mult = jnp.sqrt(1.0 - a * a)
        row = lax.broadcasted_iota(jnp.int32, (t, D_RG), 0)
        mult = jnp.where((row == 0) & (c == 0), 1.0, mult)
        a_s[r] = a
        b_s[r] = mult * (igate * xc)
        ub = u.astype(BF16)
        for hf in range(2):
            bu = jnp.dot(ub[:, hf * 256:(hf + 1) * 256], bmat_ref[hf], preferred_element_type=F32)
            sre[r, :, hf * half:(hf + 1) * half] = bu[:, :half]
            sim[r, :, hf * half:(hf + 1) * half] = bu[:, half:]
        return x, gate_rg, u

    def scans(r):
        h = jnp.zeros((SUBLANES, D_RG), F32)
        ac = jnp.ones((SUBLANES, D_RG), F32)
        for q in range(ngrp):
            aa = a_s[r, grp(q), :]
            h = aa * h + b_s[r, grp(q), :]
            ac = aa * ac
        h, last = _sublane_chain(ac, h, rg_carry[r], sub)
        rg_carry[r] = last
        for q in range(ngrp):
            h = a_s[r, grp(q), :] * h + b_s[r, grp(q), :]
            b_s[r, grp(q), :] = h
        for blk in range(N_STATE // lb):
            cs = slice(blk * lb, (blk + 1) * lb)
            ar = tab_ref[8, :, cs]
            ai = tab_ref[9, :, cs]
            er = jnp.zeros((SUBLANES, lb), F32)
            ei = jnp.zeros((SUBLANES, lb), F32)
            for q in range(ngrp):
                er, ei = (ar * er - ai * ei + sre[r, grp(q), cs], ar * ei + ai * er + sim[r, grp(q), cs])
            for i, d in enumerate((1, 2, 4)):
                mr = tab_ref[2 * i, :, cs]
                mi = tab_ref[2 * i + 1, :, cs]
                rr = pltpu.roll(er, d, axis=0)
                ri = pltpu.roll(ei, d, axis=0)
                er, ei = er + (mr * rr - mi * ri), ei + (mr * ri + mi * rr)
            pr = tab_ref[6, :, cs]
            pi_ = tab_ref[7, :, cs]
            cr = c_re[r, :, cs]
            ci = c_im[r, :, cs]
            sr = er + (pr * cr - pi_ * ci)
            si = ei + (pr * ci + pi_ * cr)
            c_re[r, :, cs] = sr[SUBLANES - 1:SUBLANES, :]
            c_im[r, :, cs] = si[SUBLANES - 1:SUBLANES, :]
            hr = jnp.where(subl == 0, cr, pltpu.roll(sr, 1, axis=0))
            hi = jnp.where(subl == 0, ci, pltpu.roll(si, 1, axis=0))
            for q in range(ngrp):
                hr, hi = (ar * hr - ai * hi + sre[r, grp(q), cs], ar * hi + ai * hr + sim[r, grp(q), cs])
                sre[r, grp(q), cs] = hr
                sim[r, grp(q), cs] = hi

    def back(r, x, gate_rg, u):
        y_rg = b_s[r] * _gelu_tanh(gate_rg)
        ys = []
        for hf in range(2):
            hcat = jnp.concatenate([sre[r, :, hf * half:(hf + 1) * half], sim[r, :, hf * half:(hf + 1) * half]],
                                   axis=1)
            ys.append(jnp.dot(hcat.astype(BF16), cmat_ref[hf], preferred_element_type=F32))
        y = jnp.concatenate(ys, axis=1) + dskip_ref[...] * u
        z = _gelu_tanh(y)
        zg = _sigmoid(jnp.dot(z.astype(BF16), glu_ref[...], preferred_element_type=F32) + glub_ref[...])
        ycat = jnp.concatenate([y_rg, z * zg], axis=1).astype(BF16)
        res = x + jnp.dot(ycat, wout_ref[...], preferred_element_type=F32)
        xout[r, slot] = res.reshape(ngrp, SUBLANES, D_MODEL)
        for i in range(SUBLANES):
            out_copy(step, slot, r, i).start()

    kept = [front(r) for r in range(N_ROWS)]
    for r in range(N_ROWS):
        scans(r)
    for r in range(N_ROWS):
        back(r, *kept[r])

    @pl.when(step == nsteps - 1)
    def _():
        for r in range(N_ROWS):
            for i in range(SUBLANES):
                out_copy(step, slot, r, i).wait()

        @pl.when(step >= 1)
        def _():
            for r in range(N_ROWS):
                for i in range(SUBLANES):
                    out_copy(step - 1, 1 - slot, r, i).wait()


def _mixer(x2, bsz, seq, p):
    t = min(T_MIX, seq)
    nc = seq // t
    nh = (CONV_W - 1) * SUBLANES
    assert bsz % N_ROWS == 0
    const = lambda shape: pl.BlockSpec(shape, lambda b, c: (0,) * len(shape), pipeline_mode=pl.Buffered(1))
    per_row = lambda *shape: pltpu.VMEM((N_ROWS,) + shape, F32)
    return pl.pallas_call(
        _mixer_kernel,
        out_shape=jax.ShapeDtypeStruct(x2.shape, F32),
        grid=(bsz // N_ROWS, nc),
        in_specs=[
            pl.BlockSpec(memory_space=pl.ANY),
            const((1, D_MODEL)), const((D_MODEL, 3 * D_RG)), const((CONV_W, D_RG)), const((1, D_RG)),
            const((D_RG, 2 * D_RG)), const((1, 2 * D_RG)), const((1, D_RG)),
            const((2, 256, N_STATE)), const((10, SUBLANES, N_STATE)), const((2, N_STATE, 256)),
            const((1, D_S5)), const((D_S5, D_S5)), const((1, D_S5)), const((D_MODEL, D_MODEL)),
        ],
        out_specs=pl.BlockSpec(memory_space=pl.ANY),
        scratch_shapes=[
            per_row(2, t // SUBLANES, SUBLANES, D_MODEL), per_row(2, t // SUBLANES, SUBLANES, D_MODEL),
            pltpu.SemaphoreType.DMA((N_ROWS, 2)), pltpu.SemaphoreType.DMA((N_ROWS, 2)),
            per_row(t + nh, D_RG), per_row(nh, D_RG),
            per_row(t, D_RG), per_row(t, D_RG), per_row(1, D_RG),
            per_row(t, N_STATE), per_row(t, N_STATE),
            per_row(1, N_STATE), per_row(1, N_STATE),
        ],
        compiler_params=pltpu.CompilerParams(
            dimension_semantics=("arbitrary", "arbitrary"), vmem_limit_bytes=VMEM_LIMIT),
        name="mixer",
    )(x2, p["g_mix"], p["w_in"], p["conv_w"], p["conv_b"], p["w_gates"], p["b_gates"], p["sp"],
      p["bmat"], p["s5tab"], p["cmat"], p["d_skip"], p["glu_w"], p["glu_b"], p["w_out"])


def _kv_kernel(m_ref, g_ref, w_ref, k_ref, v_ref):
    mn = _rms(m_ref[...], g_ref[...]).astype(BF16)
    kv = jnp.dot(mn, w_ref[...], preferred_element_type=F32)
    k_ref[...] = kv[:, :D_MODEL].astype(BF16)
    v_ref[...] = kv[:, D_MODEL:].astype(BF16)


def _kv(mem2, bsz, mlen, g, wkv):
    return pl.pallas_call(
        _kv_kernel,
        out_shape=(jax.ShapeDtypeStruct(mem2.shape, BF16), jax.ShapeDtypeStruct(mem2.shape, BF16)),
        grid=(bsz,),
        in_specs=[pl.BlockSpec((mlen, D_MODEL), lambda b: (b, 0)),
                  pl.BlockSpec((1, D_MODEL), lambda b: (0, 0)),
                  pl.BlockSpec((D_MODEL, 2 * D_MODEL), lambda b: (0, 0))],
        out_specs=(pl.BlockSpec((mlen, D_MODEL), lambda b: (b, 0)),
                   pl.BlockSpec((mlen, D_MODEL), lambda b: (b, 0))),
        compiler_params=pltpu.CompilerParams(dimension_semantics=("arbitrary",), vmem_limit_bytes=VMEM_LIMIT),
        name="kv",
    )(mem2, g, wkv)


def _attn_kernel(h_ref, g_ref, wq_ref, k_ref, v_ref, wo_ref, gf_ref, wr_ref, br_ref,
                 o_ref, xn_ref, lpos_ref, gate_ref, tcnt_ref):
    h = h_ref[...]
    xn = _rms(h, g_ref[...]).astype(BF16)
    q = jnp.dot(xn, wq_ref[...], preferred_element_type=F32).astype(BF16)
    outs = []
    for hd in range(XA_HEADS):
        cs = slice(hd * XA_HD, (hd + 1) * XA_HD)
        s = lax.dot_general(q[:, cs], k_ref[:, cs], (((1,), (1,)), ((), ())),
                            preferred_element_type=F32) * (XA_HD ** -0.5)
        s = s - jnp.max(s, axis=-1, keepdims=True)
        e = jnp.exp(s)
        pr = e / jnp.sum(e, axis=-1, keepdims=True)
        outs.append(jnp.dot(pr.astype(BF16), v_ref[:, cs], preferred_element_type=F32))
    o = jnp.concatenate(outs, axis=1).astype(BF16)
    h2 = h + jnp.dot(o, wo_ref[...], preferred_element_type=F32)
    o_ref[...] = h2
    tile = h2.shape[0] // tcnt_ref.shape[0]
    for j in range(tcnt_ref.shape[0]):
        rows = slice(j * tile, (j + 1) * tile)
        _route_tile(h2[rows, :], rows, j, gf_ref, wr_ref, br_ref, xn_ref, lpos_ref, gate_ref, tcnt_ref)


def _attn_route(h1, bsz, seq, mlen, g, wq, k, v, wo, g_ffn, wr_t, br):
    n = h1.shape[0]
    t = min(T_ATT, seq)
    nc = seq // t
    tile = min(T_TILE, t)
    per = t // tile
    const = lambda shape: pl.BlockSpec(shape, lambda b, c: (0,) * len(shape))
    return pl.pallas_call(
        _attn_kernel,
        out_shape=(jax.ShapeDtypeStruct(h1.shape, F32),
                   jax.ShapeDtypeStruct((n, D_MODEL), BF16),
                   jax.ShapeDtypeStruct((TOP_K, n), jnp.int32),
                   jax.ShapeDtypeStruct((TOP_K, n), F32),
                   jax.ShapeDtypeStruct((n // tile, N_EXPERTS, 128), jnp.int32)),
        grid=(bsz, nc),
        in_specs=[pl.BlockSpec((t, D_MODEL), lambda b, c: (b * nc + c, 0)),
                  const((1, D_MODEL)), const((D_MODEL, D_MODEL)),
                  pl.BlockSpec((mlen, D_MODEL), lambda b, c: (b, 0)),
                  pl.BlockSpec((mlen, D_MODEL), lambda b, c: (b, 0)),
                  const((D_MODEL, D_MODEL)),
                  const((1, D_MODEL)), const((2 * N_EXPERTS, D_MODEL)), const((N_EXPERTS, 1))],
        out_specs=(pl.BlockSpec((t, D_MODEL), lambda b, c: (b * nc + c, 0)),
                   pl.BlockSpec((t, D_MODEL), lambda b, c: (b * nc + c, 0)),
                   pl.BlockSpec((TOP_K, t), lambda b, c: (0, b * nc + c)),
                   pl.BlockSpec((TOP_K, t), lambda b, c: (0, b * nc + c)),
                   pl.BlockSpec((per, N_EXPERTS, 128), lambda b, c: (b * nc + c, 0, 0))),
        compiler_params=pltpu.CompilerParams(
            dimension_semantics=("arbitrary", "arbitrary"), vmem_limit_bytes=VMEM_LIMIT),
        name="attn_route",
    )(h1, g, wq, k, v, wo, g_ffn, wr_t, br)


def _route_tile(h, rows, tile, g_ref, wr_ref, br_ref, xn_ref, lpos_ref, gate_ref, tcnt_ref):
    t = h.shape[0]
    xn = _rms(h, g_ref[...])
    x_hi = xn.astype(BF16)
    x_lo = (xn - x_hi.astype(F32)).astype(BF16)
    xn_ref[rows, :] = x_hi
    dims = (((1,), (1,)), ((), ()))
    p_hi = lax.dot_general(wr_ref[...], x_hi, dims, preferred_element_type=F32)
    p_lo = lax.dot_general(wr_ref[0:N_EXPERTS, :], x_lo, dims, preferred_element_type=F32)
    lg = p_hi[0:N_EXPERTS, :] + (p_hi[N_EXPERTS:, :] + p_lo) + br_ref[...]
    erow = lax.broadcasted_iota(jnp.int32, (N_EXPERTS, t), 0)
    vals, hots = [], []
    for _ in range(TOP_K):
        m = jnp.max(lg, axis=0, keepdims=True)
        ix = jnp.min(jnp.where(lg == m, erow, N_EXPERTS), axis=0, keepdims=True)
        hot = erow == ix
        lg = jnp.where(hot, -jnp.inf, lg)
        vals.append(m)
        hots.append(hot)
    es = [jnp.exp(v - vals[0]) for v in vals]
    den = es[0] + es[1] + es[2] + es[3]
    sel = jnp.zeros((N_EXPERTS, t), F32)
    for hot in hots:
        sel = sel + jnp.where(hot, 1.0, 0.0)
    tri = jnp.where(lax.broadcasted_iota(jnp.int32, (t, t), 0) < lax.broadcasted_iota(jnp.int32, (t, t), 1),
                    1.0, 0.0).astype(BF16)
    rank = jnp.dot(sel.astype(BF16), tri, preferred_element_type=F32)
    cnt = jnp.sum(sel, axis=1, keepdims=True)
    run8 = jnp.floor((cnt + (SUBLANES - 1.0)) * (1.0 / SUBLANES))
    below = jnp.where(lax.broadcasted_iota(jnp.int32, (N_EXPERTS, N_EXPERTS), 1)
                      < lax.broadcasted_iota(jnp.int32, (N_EXPERTS, N_EXPERTS), 0), 1.0, 0.0).astype(BF16)
    start8 = jnp.dot(below, jnp.broadcast_to(run8, (N_EXPERTS, 128)).astype(BF16), preferred_element_type=F32)
    pos = rank + SUBLANES * start8[:, 0:1]
    for k in range(TOP_K):
        gate_ref[k:k + 1, rows] = es[k] / den
        lpos_ref[k:k + 1, rows] = jnp.sum(jnp.where(hots[k], pos, 0.0), axis=0, keepdims=True).astype(jnp.int32)
    tcnt_ref[tile] = jnp.broadcast_to(cnt, (N_EXPERTS, 128)).astype(jnp.int32)


RUN_CHUNK = 64
WAIT_CHUNK = 256


def _run_copies(i, lo_ref, go_ref, n8_ref, make_copy):
    def expert(e, carry):
        idx = i * N_EXPERTS + e
        lo = lo_ref[idx]
        go = go_ref[idx]
        n8 = n8_ref[idx]
        nbig = lax.shift_right_logical(n8, 3)

        def big(j, c):
            make_copy(pl.multiple_of(lo + j * RUN_CHUNK, SUBLANES), pl.multiple_of(go + j * RUN_CHUNK, SUBLANES),
                      RUN_CHUNK).start()
            return c

        lax.fori_loop(0, nbig, big, 0)
        off = nbig * RUN_CHUNK
        for rows in (32, 16, 8):
            has = (n8 & (rows // SUBLANES)) != 0

            @pl.when(has)
            def _(off=off, rows=rows):
                make_copy(pl.multiple_of(lo + off, SUBLANES), pl.multiple_of(go + off, SUBLANES), rows).start()

            off = off + jnp.where(has, rows, 0)
        return carry

    lax.fori_loop(0, N_EXPERTS, expert, 0)


def _wait_rows(n8, make_wait):
    def big(j, c):
        make_wait(WAIT_CHUNK).wait()
        return c

    lax.fori_loop(0, lax.shift_right_logical(n8, 5), big, 0)
    for rows in (128, 64, 32, 16, 8):
        @pl.when((n8 & (rows // SUBLANES)) != 0)
        def _(rows=rows):
            make_wait(rows).wait()


def _dispatch_kernel(lo_ref, go_ref, n8_ref, nt_ref, zs_ref, zn_ref, xn_ref, lpos_ref, xs_hbm, loc, zbuf, sem,
                     sem_z):
    i = pl.program_id(0)
    last = pl.num_programs(0) - 1
    slot = lax.rem(i, 2)
    nloc, t = loc.shape[1], xn_ref.shape[0]

    def row_copy(sl, lo, go, rows):
        return pltpu.make_async_copy(loc.at[sl, pl.ds(lo, rows), :], xs_hbm.at[pl.ds(go, rows), :], sem.at[sl])

    def wait_copies(sl, count):
        _wait_rows(count, lambda rows: row_copy(sl, 0, 0, rows))

    @pl.when(i >= 2)
    def _():
        wait_copies(slot, nt_ref[jnp.maximum(i - 2, 0)])

    mc = 256
    liota = lax.broadcasted_iota(jnp.int32, (mc, t), 0)
    xn = xn_ref[...]
    for r in range(nloc // mc):
        pick = jnp.zeros((mc, t), F32)
        for k in range(TOP_K):
            pick = jnp.where(liota == lpos_ref[k:k + 1, :] - r * mc, 1.0, pick)
        loc[slot, r * mc:(r + 1) * mc, :] = jnp.dot(pick.astype(BF16), xn, preferred_element_type=F32)
    _run_copies(i, lo_ref, go_ref, n8_ref, lambda lo, go, rows: row_copy(slot, lo, go, rows))

    @pl.when(i == last)
    def _():
        wait_copies(slot, nt_ref[i])

        @pl.when(i >= 1)
        def _():
            wait_copies(1 - slot, nt_ref[jnp.maximum(i - 1, 0)])

        zbuf[...] = jnp.zeros_like(zbuf)

        def zero_copy(go):
            return pltpu.make_async_copy(zbuf.at[pl.ds(0, SUBLANES), :], xs_hbm.at[pl.ds(go, SUBLANES), :], sem_z)

        def expert(e, total):
            def chunk(j, carry):
                zero_copy(pl.multiple_of(zs_ref[e] + j * SUBLANES, SUBLANES)).start()
                return carry
            lax.fori_loop(0, zn_ref[e], chunk, 0)
            return total + zn_ref[e]

        total = lax.fori_loop(0, N_EXPERTS, expert, 0)

        def wait_zero(j, carry):
            zero_copy(0).wait()
            return carry

        lax.fori_loop(0, total, wait_zero, 0)

        def block_copy(go):
            return pltpu.make_async_copy(zbuf, xs_hbm.at[pl.ds(go, TM_MOE), :], sem_z)

        def start_block(j, carry):
            block_copy(pl.multiple_of(zs_ref[N_EXPERTS] + j * TM_MOE, TM_MOE)).start()
            return carry

        def wait_block(j, carry):
            block_copy(0).wait()
            return carry

        lax.fori_loop(0, zn_ref[N_EXPERTS], start_block, 0)
        lax.fori_loop(0, zn_ref[N_EXPERTS], wait_block, 0)


def _dispatch(tabs, xn, lpos_t, n_slots, nloc):
    n = xn.shape[0]
    t = min(T_TILE, n)
    grid_spec = pltpu.PrefetchScalarGridSpec(
        num_scalar_prefetch=6,
        grid=(n // t,),
        in_specs=[pl.BlockSpec((t, D_MODEL), lambda i, *_: (i, 0)),
                  pl.BlockSpec((TOP_K, t), lambda i, *_: (0, i))],
        out_specs=pl.BlockSpec(memory_space=pl.ANY),
        scratch_shapes=[pltpu.VMEM((2, nloc, D_MODEL), F32), pltpu.VMEM((TM_MOE, D_MODEL), F32),
                        pltpu.SemaphoreType.DMA((2,)), pltpu.SemaphoreType.DMA],
    )
    return pl.pallas_call(
        _dispatch_kernel,
        out_shape=jax.ShapeDtypeStruct((n_slots, D_MODEL), F32),
        grid_spec=grid_spec,
        compiler_params=pltpu.CompilerParams(dimension_semantics=("arbitrary",), vmem_limit_bytes=VMEM_LIMIT),
        name="dispatch",
    )(tabs["lo"], tabs["go"], tabs["n8"], tabs["nt"], tabs["zs"], tabs["zn"], xn, lpos_t)


def _moe_kernel(bexp_ref, nused_ref, nexp_ref, nrow_ref, x_ref, wgu_hbm, bgu_ref, wd_hbm, bd_ref, o_ref,
                wgu_f, wd_f, wgu_s, wd_s, sem_w):
    b = pl.program_id(0)
    prev = bexp_ref[jnp.maximum(b - 1, 0)]
    fresh = (b == 0) | (bexp_ref[b] != prev)

    def fetch(ex):
        return (pltpu.make_async_copy(wgu_hbm.at[ex], wgu_f, sem_w.at[0]),
                pltpu.make_async_copy(wd_hbm.at[ex], wd_f, sem_w.at[1]))

    @pl.when(b == 0)
    def _():
        for cp in fetch(bexp_ref[0]):
            cp.start()

    @pl.when(fresh & (b < nused_ref[0]))
    def _():
        for cp in fetch(bexp_ref[b]):
            cp.wait()
        rows = 128
        for r0 in range(0, D_MODEL, rows):
            wgu_s[r0:r0 + rows, :] = wgu_f[r0:r0 + rows, :].astype(BF16)
        for r0 in range(0, D_FF, rows):
            wd_s[r0:r0 + rows, :] = wd_f[r0:r0 + rows, :].astype(BF16)

        @pl.when(nexp_ref[b] >= 0)
        def _():
            for cp in fetch(nexp_ref[b]):
                cp.start()

    def mlp(x):
        gu = jnp.dot(x.astype(BF16), wgu_s[...], preferred_element_type=F32) + bgu_ref[0]
        g = jnp.minimum(gu[:, :D_FF], SWIGLU_LIMIT)
        u = jnp.clip(gu[:, D_FF:], -SWIGLU_LIMIT, SWIGLU_LIMIT)
        h = (u + 1.0) * (g * _sigmoid(SWIGLU_ALPHA * g))
        return jnp.dot(h.astype(BF16), wd_s[...], preferred_element_type=F32) + bd_ref[0]

    used = b < nused_ref[0]
    half = TM_MOE // 2
    few = nrow_ref[b] <= half

    @pl.when(used & jnp.logical_not(few))
    def _():
        o_ref[...] = mlp(x_ref[...])

    @pl.when(used & few)
    def _():
        o_ref[0:half, :] = mlp(x_ref[0:half, :])
        o_ref[half:, :] = jnp.zeros((TM_MOE - half, D_MODEL), F32)

    @pl.when(jnp.logical_not(used))
    def _():
        o_ref[...] = jnp.zeros_like(o_ref)


def _moe(block_exp, n_used, next_exp, block_rows, xs, w_gu, b_gu, w_down, b_down):
    n_slots = xs.shape[0]
    nb = n_slots // TM_MOE
    bspec = lambda shape: pl.BlockSpec(shape, lambda b, be, *_: (be[b], 0, 0))
    grid_spec = pltpu.PrefetchScalarGridSpec(
        num_scalar_prefetch=4,
        grid=(nb,),
        in_specs=[
            pl.BlockSpec((TM_MOE, D_MODEL), lambda b, be, nu, *_: (jnp.minimum(b, nu[0] - 1), 0)),
            pl.BlockSpec(memory_space=pl.ANY), bspec((1, 1, 2 * D_FF)),
            pl.BlockSpec(memory_space=pl.ANY), bspec((1, 1, D_MODEL)),
        ],
        out_specs=pl.BlockSpec((TM_MOE, D_MODEL), lambda b, *_: (b, 0)),
        scratch_shapes=[pltpu.VMEM((D_MODEL, 2 * D_FF), F32), pltpu.VMEM((D_FF, D_MODEL), F32),
                        pltpu.VMEM((D_MODEL, 2 * D_FF), BF16), pltpu.VMEM((D_FF, D_MODEL), BF16),
                        pltpu.SemaphoreType.DMA((2,))],
    )
    return pl.pallas_call(
        _moe_kernel,
        out_shape=jax.ShapeDtypeStruct((n_slots, D_MODEL), F32),
        grid_spec=grid_spec,
        compiler_params=pltpu.CompilerParams(dimension_semantics=("arbitrary",), vmem_limit_bytes=VMEM_LIMIT),
        name="moe",
    )(block_exp, n_used, next_exp, block_rows, xs, w_gu, b_gu, w_down, b_down)


def _combine_kernel(lo_ref, go_ref, n8_ref, nt_ref, ys_hbm, h_ref, lpos_ref, gate_ref, g_ref, o_ref, loc, lpb,
                    gtb, sem):
    i = pl.program_id(0)
    slot = lax.rem(i, 2)
    nloc, t = loc.shape[1], h_ref.shape[0]

    def row_copy(sl, lo, go, rows):
        return pltpu.make_async_copy(ys_hbm.at[pl.ds(go, rows), :], loc.at[sl, pl.ds(lo, rows), :], sem.at[sl])

    def fetch(tile, sl):
        _run_copies(tile, lo_ref, go_ref, n8_ref, lambda lo, go, rows: row_copy(sl, lo, go, rows))

        def zero_rows(j, carry):
            r0 = pl.multiple_of((nt_ref[tile] + j) * SUBLANES, SUBLANES)
            loc[sl, pl.ds(r0, SUBLANES), :] = jnp.zeros((SUBLANES, D_MODEL), F32)
            return carry

        lax.fori_loop(0, nloc // SUBLANES - nt_ref[tile], zero_rows, 0)

    @pl.when(i == 0)
    def _():
        fetch(0, 0)

    @pl.when(i + 1 < pl.num_programs(0))
    def _():
        fetch(i + 1, 1 - slot)

    _wait_rows(nt_ref[i], lambda rows: row_copy(slot, 0, 0, rows))
    kc = 256
    lane = lax.broadcasted_iota(jnp.int32, (t, 128), 1)
    for k in range(TOP_K):
        lpb[k] = jnp.broadcast_to(lpos_ref[:, k:k + 1], (t, 128))
        gtb[k] = jnp.broadcast_to(gate_ref[:, k:k + 1], (t, 128))
    acc = h_ref[...]
    for c in range(nloc // kc):
        pieces = []
        for j in range(kc // 128):
            li = lane + (c * kc + j * 128)
            w = jnp.zeros((t, 128), F32)
            for k in range(TOP_K):
                w = jnp.where(li == lpb[k], gtb[k], w)
            pieces.append(w)
        wc = jnp.concatenate(pieces, axis=1).astype(BF16)
        acc = acc + jnp.dot(wc, loc[slot, c * kc:(c + 1) * kc, :].astype(BF16), preferred_element_type=F32)
    o_ref[...] = _rms(acc, g_ref[...])


def _combine(tabs, ys, h2, lpos_tok, gates_tok, g, nloc):
    n = h2.shape[0]
    t = min(T_TILE, n)
    grid_spec = pltpu.PrefetchScalarGridSpec(
        num_scalar_prefetch=4,
        grid=(n // t,),
        in_specs=[pl.BlockSpec(memory_space=pl.ANY),
                  pl.BlockSpec((t, D_MODEL), lambda i, *_: (i, 0)),
                  pl.BlockSpec((t, TOP_K), lambda i, *_: (i, 0)),
                  pl.BlockSpec((t, TOP_K), lambda i, *_: (i, 0)),
                  pl.BlockSpec((1, D_MODEL), lambda i, *_: (0, 0))],
        out_specs=pl.BlockSpec((t, D_MODEL), lambda i, *_: (i, 0)),
        scratch_shapes=[pltpu.VMEM((2, nloc, D_MODEL), F32), pltpu.VMEM((TOP_K, t, 128), jnp.int32),
                        pltpu.VMEM((TOP_K, t, 128), F32), pltpu.SemaphoreType.DMA((2,))],
    )
    return pl.pallas_call(
        _combine_kernel,
        out_shape=jax.ShapeDtypeStruct((n, D_MODEL), F32),
        grid_spec=grid_spec,
        compiler_params=pltpu.CompilerParams(dimension_semantics=("arbitrary",), vmem_limit_bytes=VMEM_LIMIT),
        name="combine",
    )(tabs["lo"], tabs["go"], tabs["n8"], tabs["nt"], ys, h2, lpos_tok, gates_tok, g)


def _cmul(a, b):
    return a[0] * b[0] - a[1] * b[1], a[0] * b[1] + a[1] * b[0]


def _prep_mixer_params(sub_len, norm_mix_g, w_in, conv_w, conv_b, rg_wa, rg_ba, rg_wx, rg_bx, rg_lambda,
                       s5_lambda_re, s5_lambda_im, s5_b_re, s5_b_im, s5_c_re, s5_c_im, s5_d, s5_log_dt,
                       s5_glu_w, s5_glu_b, w_out):
    row = lambda v: v.reshape(1, -1).astype(F32)
    dt = jnp.exp(s5_log_dt)[:, None]
    mag = jnp.exp(dt * s5_lambda_re)
    abar_re = mag * jnp.cos(dt * s5_lambda_im)
    abar_im = mag * jnp.sin(dt * s5_lambda_im)
    den = s5_lambda_re * s5_lambda_re + s5_lambda_im * s5_lambda_im
    num_re = abar_re - 1.0
    coef_re = (num_re * s5_lambda_re + abar_im * s5_lambda_im) / den
    coef_im = (abar_im * s5_lambda_re - num_re * s5_lambda_im) / den
    bbar_re = coef_re[..., None] * s5_b_re - coef_im[..., None] * s5_b_im
    bbar_im = coef_re[..., None] * s5_b_im + coef_im[..., None] * s5_b_re
    a1 = (abar_re.reshape(-1), abar_im.reshape(-1))
    aq = a1
    for _ in range(int(math.log2(sub_len))):
        aq = _cmul(aq, aq)
    pw = [aq]
    for _ in range(SUBLANES - 1):
        pw.append(_cmul(pw[-1], aq))
    sub = jnp.arange(SUBLANES)[:, None]
    tabs = []
    for d in (1, 2, 4):
        keep = (sub >= d).astype(F32)
        tabs += [keep * pw[d - 1][0][None, :], keep * pw[d - 1][1][None, :]]
    tabs += [jnp.stack([p_[0] for p_ in pw]), jnp.stack([p_[1] for p_ in pw])]
    tabs += [jnp.broadcast_to(a1[0][None, :], (SUBLANES, N_STATE)),
             jnp.broadcast_to(a1[1][None, :], (SUBLANES, N_STATE))]
    s5tab = jnp.stack(tabs).astype(F32)
    gh = S5_GROUPS // 2
    bmat, cmat = [], []
    for hf in range(2):
        gs = slice(hf * gh, (hf + 1) * gh)
        b_re = _block_diag(jnp.transpose(bbar_re[gs], (0, 2, 1)))
        b_im = _block_diag(jnp.transpose(bbar_im[gs], (0, 2, 1)))
        bmat.append(jnp.concatenate([b_re, b_im], axis=1))
        cr = _block_diag(jnp.transpose(s5_c_re[gs], (0, 2, 1)))
        ci = _block_diag(jnp.transpose(s5_c_im[gs], (0, 2, 1)))
        cmat.append(jnp.concatenate([cr, -ci], axis=0))
    return {
        "g_mix": row(norm_mix_g), "w_in": w_in.astype(BF16), "conv_w": conv_w.astype(F32), "conv_b": row(conv_b),
        "w_gates": jnp.concatenate([_block_diag(rg_wa), _block_diag(rg_wx)], axis=1).astype(BF16),
        "b_gates": jnp.concatenate([rg_ba, rg_bx]).reshape(1, -1).astype(F32),
        "sp": row(jax.nn.softplus(-rg_lambda)),
        "bmat": jnp.stack(bmat).astype(BF16), "s5tab": s5tab, "cmat": jnp.stack(cmat).astype(BF16),
        "d_skip": row(s5_d), "glu_w": _block_diag(s5_glu_w).astype(BF16), "glu_b": row(s5_glu_b),
        "w_out": w_out.astype(BF16),
    }


def _slot_tables(tcnt, n_slots):
    i32 = lambda v: v.astype(jnp.int32)
    run = ((tcnt + SUBLANES - 1) // SUBLANES) * SUBLANES
    rows = jnp.sum(run, axis=0)
    blocks = (rows + TM_MOE - 1) // TM_MOE
    bend = jnp.cumsum(blocks)
    estart = (bend - blocks) * TM_MOE
    nb = n_slots // TM_MOE
    n_used = i32(bend[-1])
    block_exp = jnp.minimum(jnp.sum(jnp.arange(nb)[:, None] >= bend[None, :], axis=1), N_EXPERTS - 1)
    block_exp = jnp.where(jnp.arange(nb) < n_used, block_exp, block_exp[jnp.maximum(n_used - 1, 0)])
    eidx = jnp.arange(N_EXPERTS)
    later = (eidx[None, :] > eidx[:, None]) & (blocks[None, :] > 0)
    next_of = jnp.min(jnp.where(later, eidx[None, :], N_EXPERTS), axis=1)
    next_exp = jnp.where(next_of < N_EXPERTS, next_of, -1)[block_exp]
    block_rows = jnp.clip(rows[block_exp] - (jnp.arange(nb) - (bend - blocks)[block_exp]) * TM_MOE, 0, TM_MOE)
    tabs = {
        "lo": i32(jnp.cumsum(run, axis=1) - run).reshape(-1),
        "go": i32(estart[None, :] + jnp.cumsum(run, axis=0) - run).reshape(-1),
        "n8": i32(run // SUBLANES).reshape(-1),
        "nt": i32(jnp.sum(run, axis=1) // SUBLANES),
        "zs": i32(jnp.concatenate([estart + rows, bend[-1:] * TM_MOE])),
        "zn": i32(jnp.concatenate([(blocks * TM_MOE - rows) // SUBLANES, nb - bend[-1:]])),
    }
    return i32(block_exp), n_used.reshape(1), i32(next_exp), i32(block_rows), tabs


def kernel(x, mem, norm_mix_g, w_in, conv_w, conv_b, rg_wa, rg_ba, rg_wx, rg_bx, rg_lambda, s5_lambda_re,
           s5_lambda_im, s5_b_re, s5_b_im, s5_c_re, s5_c_im, s5_d, s5_log_dt, s5_glu_w, s5_glu_b, w_out,
           norm_xa_g, mem_norm_g, xa_wq, xa_wkv, xa_wo, norm_ffn_g, router_w, router_b, exp_w_gu, exp_b_gu,
           exp_w_down, exp_b_down, final_norm_g):
    bsz, seq, d = x.shape
    mlen = mem.shape[1]
    n = bsz * seq
    row = lambda v: v.reshape(1, -1).astype(F32)

    p = _prep_mixer_params(min(T_MIX, seq) // SUBLANES, norm_mix_g, w_in, conv_w, conv_b, rg_wa, rg_ba, rg_wx,
                           rg_bx, rg_lambda, s5_lambda_re, s5_lambda_im, s5_b_re, s5_b_im, s5_c_re, s5_c_im,
                           s5_d, s5_log_dt, s5_glu_w, s5_glu_b, w_out)
    h1 = _mixer(x.reshape(n, d), bsz, seq, p)

    k, v = _kv(mem.reshape(bsz * mlen, d), bsz, mlen, row(mem_norm_g), xa_wkv.astype(BF16))
    wr_t = router_w.T.astype(F32)
    wr_hi = wr_t.astype(BF16)
    wr_lo = (wr_t - wr_hi.astype(F32)).astype(BF16)
    h2, xn, lpos_t, gate_t, tcnt = _attn_route(
        h1, bsz, seq, mlen, row(norm_xa_g), xa_wq.astype(BF16), k, v, xa_wo.astype(BF16),
        row(norm_ffn_g), jnp.concatenate([wr_hi, wr_lo], axis=0), router_b.reshape(-1, 1).astype(F32))
    tile = min(T_TILE, min(T_ATT, seq))
    n_tiles = n // tile
    run_pad = N_EXPERTS * (SUBLANES - 1)
    nloc = -(-(TOP_K * tile + run_pad) // 128) * 128
    n_slots = -(-(TOP_K * n + n_tiles * run_pad + N_EXPERTS * (TM_MOE - SUBLANES)) // TM_MOE) * TM_MOE
    block_exp, n_used, next_exp, block_rows, tabs = _slot_tables(tcnt[:, :, 0], n_slots)
    xs = _dispatch(tabs, xn, lpos_t, n_slots, nloc)
    ys = _moe(block_exp, n_used, next_exp, block_rows, xs, exp_w_gu, exp_b_gu.reshape(N_EXPERTS, 1, -1),
              exp_w_down, exp_b_down.reshape(N_EXPERTS, 1, -1))
    out = _combine(tabs, ys, h2, lpos_t.T, gate_t.T, row(final_norm_g), nloc)
    return out.reshape(bsz, seq, d)
```

```python
import math

import jax
import jax.numpy as jnp
from jax import lax
from jax.experimental import pallas as pl
from jax.experimental.pallas import tpu as pltpu

F32 = jnp.float32
BF16 = jnp.bfloat16

D_MODEL = 1024
D_RG = 512
D_S5 = 512
CONV_W = 4
RG_C = 8.0
S5_GROUPS = 32
S5_STATE = 64
N_STATE = S5_GROUPS * S5_STATE
XA_HEADS = 4
XA_HD = 256
N_EXPERTS = 32
TOP_K = 4
D_FF = 1024
SWIGLU_LIMIT = 7.0
SWIGLU_ALPHA = 1.702
NORM_EPS = 1e-6

SUBLANES = 8
T_MIX = 512
T_ATT = 1024
T_TILE = 512
TM_MOE = 512
VMEM_LIMIT = 56 * 1024 * 1024


def _rms(x, g):
    return x * lax.rsqrt(jnp.mean(x * x, axis=-1, keepdims=True) + NORM_EPS) * g


def _gelu_tanh(x):
    return 0.5 * x * (1.0 + jnp.tanh(math.sqrt(2.0 / math.pi) * (x + 0.044715 * (x * x * x))))


def _sigmoid(x):
    return 1.0 / (1.0 + jnp.exp(-x))


def _block_diag(blocks):
    n, r, c = blocks.shape
    eye = jnp.eye(n, dtype=blocks.dtype)
    return jnp.einsum('nrc,nm->nrmc', blocks, eye).reshape(n * r, n * c)


def _sublane_chain(al, e, carry, sub):
    for d in (1, 2, 4):
        al_sh = jnp.where(sub >= d, pltpu.roll(al, d, axis=0), 1.0)
        e_sh = jnp.where(sub >= d, pltpu.roll(e, d, axis=0), 0.0)
        e = al * e_sh + e
        al = al * al_sh
    s_end = e + al * carry
    start = jnp.where(sub == 0, carry, pltpu.roll(s_end, 1, axis=0))
    return start, s_end[SUBLANES - 1:SUBLANES, :]


N_ROWS = 2


def _mixer_kernel(x_ref, g_ref, win_ref, convw_ref, convb_ref, wg_ref, bg_ref, sp_ref,
                   bmat_ref, tab_ref, cmat_ref, dskip_ref, glu_ref, glub_ref, wout_ref,
                   o_ref,
                   xin, xout, sem_i, sem_o, xbuf, hist, a_s, b_s, rg_carry, sre, sim, c_re, c_im):
    ngrp = xin.shape[2]
    t = ngrp * SUBLANES
    nh = (CONV_W - 1) * SUBLANES
    nc = pl.num_programs(1)
    c = pl.program_id(1)
    step = pl.program_id(0) * nc + c
    nsteps = pl.num_programs(0) * nc
    slot = lax.rem(step, 2)

    def row_base(s, r):
        return ((N_ROWS * lax.div(s, nc) + r) * nc + lax.rem(s, nc)) * t

    def in_copy(s, sl, r, i):
        return pltpu.make_async_copy(x_ref.at[pl.ds(row_base(s, r) + i * ngrp, ngrp), :], xin.at[r, sl, :, i, :],
                                     sem_i.at[r, sl])

    def out_copy(s, sl, r, i):
        return pltpu.make_async_copy(xout.at[r, sl, :, i, :], o_ref.at[pl.ds(row_base(s, r) + i * ngrp, ngrp), :],
                                     sem_o.at[r, sl])

    @pl.when(step == 0)
    def _():
        for r in range(N_ROWS):
            for i in range(SUBLANES):
                in_copy(0, 0, r, i).start()

    @pl.when(c == 0)
    def _():
        hist[...] = jnp.zeros_like(hist)
        rg_carry[...] = jnp.zeros_like(rg_carry)
        c_re[...] = jnp.zeros_like(c_re)
        c_im[...] = jnp.zeros_like(c_im)

    for r in range(N_ROWS):
        for i in range(SUBLANES):
            in_copy(step, slot, r, i).wait()

    @pl.when(step + 1 < nsteps)
    def _():
        for r in range(N_ROWS):
            for i in range(SUBLANES):
                in_copy(step + 1, 1 - slot, r, i).start()

    @pl.when(step >= 2)
    def _():
        for r in range(N_ROWS):
            for i in range(SUBLANES):
                out_copy(step - 2, slot, r, i).wait()

    sub = lax.broadcasted_iota(jnp.int32, (SUBLANES, D_RG), 0)
    half = N_STATE // 2
    lb = 512
    subl = lax.broadcasted_iota(jnp.int32, (SUBLANES, lb), 0)
    grp = lambda q: slice(q * SUBLANES, (q + 1) * SUBLANES)

    def front(r):
        x = xin[r, slot].reshape(t, D_MODEL)
        xn = _rms(x, g_ref[...]).astype(BF16)
        proj = jnp.dot(xn, win_ref[...], preferred_element_type=F32)
        gate_rg = proj[:, :D_RG]
        x_rg = proj[:, D_RG:2 * D_RG]
        u = proj[:, 2 * D_RG:]
        cur_tail = x_rg[t - nh:, :]
        for gidx in range(CONV_W - 1):
            rs = slice(gidx * SUBLANES, (gidx + 1) * SUBLANES)
            xbuf[r, rs, :] = jnp.where(sub == 0, pltpu.roll(hist[r, rs, :], 1, axis=0),
                                       pltpu.roll(cur_tail[rs, :], 1, axis=0))
        hist[r] = cur_tail
        xbuf[r, nh:, :] = x_rg
        xc = convb_ref[...] + convw_ref[CONV_W - 1:CONV_W, :] * x_rg
        for k in range(CONV_W - 1):
            xc = xc + convw_ref[k:k + 1, :] * xbuf[r, k * SUBLANES:k * SUBLANES + t, :]
        gates = jnp.dot(xc.astype(BF16), wg_ref[...], preferred_element_type=F32) + bg_ref[...]
        rgate = _sigmoid(gates[:, :D_RG])
        igate = _sigmoid(gates[:, D_RG:])
        a = jnp.exp((-RG_C) * rgate * sp_ref[...])
        mult = jnp.sqrt(1.0 - a * a)
        row = lax.broadcasted_iota(jnp.int32, (t, D_RG), 0)
        mult = jnp.where((row == 0) & (c == 0), 1.0, mult)
        a_s[r] = a
        b_s[r] = mult * (igate * xc)
        ub = u.astype(BF16)
        for hf in range(2):
            bu = jnp.dot(ub[:, hf * 256:(hf + 1) * 256], bmat_ref[hf], preferred_element_type=F32)
            sre[r, :, hf * half:(hf + 1) * half] = bu[:, :half]
            sim[r, :, hf * half:(hf + 1) * half] = bu[:, half:]
        return x, gate_rg, u

    def scans(r):
        h = jnp.zeros((SUBLANES, D_RG), F32)
        ac = jnp.ones((SUBLANES, D_RG), F32)
        for q in range(ngrp):
            aa = a_s[r, grp(q), :]
            h = aa * h + b_s[r, grp(q), :]
            ac = aa * ac
        h, last = _sublane_chain(ac, h, rg_carry[r], sub)
        rg_carry[r] = last
        for q in range(ngrp):
            h = a_s[r, grp(q), :] * h + b_s[r, grp(q), :]
            b_s[r, grp(q), :] = h
        for blk in range(N_STATE // lb):
            cs = slice(blk * lb, (blk + 1) * lb)
            ar = tab_ref[8, :, cs]
            ai = tab_ref[9, :, cs]
            er = jnp.zeros((SUBLANES, lb), F32)
            ei = jnp.zeros((SUBLANES, lb), F32)
            for q in range(ngrp):
                er, ei = (ar * er - ai * ei + sre[r, grp(q), cs], ar * ei + ai * er + sim[r, grp(q), cs])
            for i, d in enumerate((1, 2, 4)):
                mr = tab_ref[2 * i, :, cs]
                mi = tab_ref[2 * i + 1, :, cs]
                rr = pltpu.roll(er, d, axis=0)
                ri = pltpu.roll(ei, d, axis=0)
                er, ei = er + (mr * rr - mi * ri), ei + (mr * ri + mi * rr)
            pr = tab_ref[6, :, cs]
            pi_ = tab_ref[7, :, cs]
            cr = c_re[r, :, cs]
            ci = c_im[r, :, cs]
            sr = er + (pr * cr - pi_ * ci)
            si = ei + (pr * ci + pi_ * cr)
            c_re[r, :, cs] = sr[SUBLANES - 1:SUBLANES, :]
            c_im[r, :, cs] = si[SUBLANES - 1:SUBLANES, :]
            hr = jnp.where(subl == 0, cr, pltpu.roll(sr, 1, axis=0))
            hi = jnp.where(subl == 0, ci, pltpu.roll(si, 1, axis=0))
            for q in range(ngrp):
                hr, hi = (ar * hr - ai * hi + sre[r, grp(q), cs], ar * hi + ai * hr + sim[r, grp(q), cs])
                sre[r, grp(q), cs] = hr
                sim[r, grp(q), cs] = hi

    def back(r, x, gate_rg, u):
        y_rg = b_s[r] * _gelu_tanh(gate_rg)
        ys = []
        for hf in range(2):
            hcat = jnp.concatenate([sre[r, :, hf * half:(hf + 1) * half], sim[r, :, hf * half:(hf + 1) * half]],
                                   axis=1)
            ys.append(jnp.dot(hcat.astype(BF16), cmat_ref[hf], preferred_element_type=F32))
        y = jnp.concatenate(ys, axis=1) + dskip_ref[...] * u
        z = _gelu_tanh(y)
        zg = _sigmoid(jnp.dot(z.astype(BF16), glu_ref[...], preferred_element_type=F32) + glub_ref[...])
        ycat = jnp.concatenate([y_rg, z * zg], axis=1).astype(BF16)
        res = x + jnp.dot(ycat, wout_ref[...], preferred_element_type=F32)
        xout[r, slot] = res.reshape(ngrp, SUBLANES, D_MODEL)
        for i in range(SUBLANES):
            out_copy(step, slot, r, i).start()

    kept = [front(r) for r in range(N_ROWS)]
    for r in range(N_ROWS):
        scans(r)
    for r in range(N_ROWS):
        back(r, *kept[r])

    @pl.when(step == nsteps - 1)
    def _():
        for r in range(N_ROWS):
            for i in range(SUBLANES):
                out_copy(step, slot, r, i).wait()

        @pl.when(step >= 1)
        def _():
            for r in range(N_ROWS):
                for i in range(SUBLANES):
                    out_copy(step - 1, 1 - slot, r, i).wait()


def _mixer(x2, bsz, seq, p):
    t = min(T_MIX, seq)
    nc = seq // t
    nh = (CONV_W - 1) * SUBLANES
    assert bsz % N_ROWS == 0
    const = lambda shape: pl.BlockSpec(shape, lambda b, c: (0,) * len(shape), pipeline_mode=pl.Buffered(1))
    per_row = lambda *shape: pltpu.VMEM((N_ROWS,) + shape, F32)
    return pl.pallas_call(
        _mixer_kernel,
        out_shape=jax.ShapeDtypeStruct(x2.shape, F32),
        grid=(bsz // N_ROWS, nc),
        in_specs=[
            pl.BlockSpec(memory_space=pl.ANY),
            const((1, D_MODEL)), const((D_MODEL, 3 * D_RG)), const((CONV_W, D_RG)), const((1, D_RG)),
            const((D_RG, 2 * D_RG)), const((1, 2 * D_RG)), const((1, D_RG)),
            const((2, 256, N_STATE)), const((10, SUBLANES, N_STATE)), const((2, N_STATE, 256)),
            const((1, D_S5)), const((D_S5, D_S5)), const((1, D_S5)), const((D_MODEL, D_MODEL)),
        ],
        out_specs=pl.BlockSpec(memory_space=pl.ANY),
        scratch_shapes=[
            per_row(2, t // SUBLANES, SUBLANES, D_MODEL), per_row(2, t // SUBLANES, SUBLANES, D_MODEL),
            pltpu.SemaphoreType.DMA((N_ROWS, 2)), pltpu.SemaphoreType.DMA((N_ROWS, 2)),
            per_row(t + nh, D_RG), per_row(nh, D_RG),
            per_row(t, D_RG), per_row(t, D_RG), per_row(1, D_RG),
            per_row(t, N_STATE), per_row(t, N_STATE),
            per_row(1, N_STATE), per_row(1, N_STATE),
        ],
        compiler_params=pltpu.CompilerParams(
            dimension_semantics=("arbitrary", "arbitrary"), vmem_limit_bytes=VMEM_LIMIT),
        name="mixer",
    )(x2, p["g_mix"], p["w_in"], p["conv_w"], p["conv_b"], p["w_gates"], p["b_gates"], p["sp"],
      p["bmat"], p["s5tab"], p["cmat"], p["d_skip"], p["glu_w"], p["glu_b"], p["w_out"])


def _kv_kernel(m_ref, g_ref, w_ref, k_ref, v_ref):
    mn = _rms(m_ref[...], g_ref[...]).astype(BF16)
    kv = jnp.dot(mn, w_ref[...], preferred_element_type=F32)
    k_ref[...] = kv[:, :D_MODEL].astype(BF16)
    v_ref[...] = kv[:, D_MODEL:].astype(BF16)


def _kv(mem2, bsz, mlen, g, wkv):
    return pl.pallas_call(
        _kv_kernel,
        out_shape=(jax.ShapeDtypeStruct(mem2.shape, BF16), jax.ShapeDtypeStruct(mem2.shape, BF16)),
        grid=(bsz,),
        in_specs=[pl.BlockSpec((mlen, D_MODEL), lambda b: (b, 0)),
                  pl.BlockSpec((1, D_MODEL), lambda b: (0, 0)),
                  pl.BlockSpec((D_MODEL, 2 * D_MODEL), lambda b: (0, 0))],
        out_specs=(pl.BlockSpec((mlen, D_MODEL), lambda b: (b, 0)),
                   pl.BlockSpec((mlen, D_MODEL), lambda b: (b, 0))),
        compiler_params=pltpu.CompilerParams(dimension_semantics=("arbitrary",), vmem_limit_bytes=VMEM_LIMIT),
        name="kv",
    )(mem2, g, wkv)


def _attn_kernel(h_ref, g_ref, wq_ref, k_ref, v_ref, wo_ref, gf_ref, wr_ref, br_ref,
                 o_ref, xn_ref, lpos_ref, gate_ref, tcnt_ref):
    h = h_ref[...]
    xn = _rms(h, g_ref[...]).astype(BF16)
    q = jnp.dot(xn, wq_ref[...], preferred_element_type=F32).astype(BF16)
    outs = []
    for hd in range(XA_HEADS):
        cs = slice(hd * XA_HD, (hd + 1) * XA_HD)
        s = lax.dot_general(q[:, cs], k_ref[:, cs], (((1,), (1,)), ((), ())),
                            preferred_element_type=F32) * (XA_HD ** -0.5)
        s = s - jnp.max(s, axis=-1, keepdims=True)
        e = jnp.exp(s)
        pr = e / jnp.sum(e, axis=-1, keepdims=True)
        outs.append(jnp.dot(pr.astype(BF16), v_ref[:, cs], preferred_element_type=F32))
    o = jnp.concatenate(outs, axis=1).astype(BF16)
    h2 = h + jnp.dot(o, wo_ref[...], preferred_element_type=F32)
    o_ref[...] = h2
    tile = h2.shape[0] // tcnt_ref.shape[0]
    for j in range(tcnt_ref.shape[0]):
        rows = slice(j * tile, (j + 1) * tile)
        _route_tile(h2[rows, :], rows, j, gf_ref, wr_ref, br_ref, xn_ref, lpos_ref, gate_ref, tcnt_ref)


def _attn_route(h1, bsz, seq, mlen, g, wq, k, v, wo, g_ffn, wr_t, br):
    n = h1.shape[0]
    t = min(T_ATT, seq)
    nc = seq // t
    tile = min(T_TILE, t)
    per = t // tile
    const = lambda shape: pl.BlockSpec(shape, lambda b, c: (0,) * len(shape))
    return pl.pallas_call(
        _attn_kernel,
        out_shape=(jax.ShapeDtypeStruct(h1.shape, F32),
                   jax.ShapeDtypeStruct((n, D_MODEL), BF16),
                   jax.ShapeDtypeStruct((TOP_K, n), jnp.int32),
                   jax.ShapeDtypeStruct((TOP_K, n), F32),
                   jax.ShapeDtypeStruct((n // tile, N_EXPERTS, 128), jnp.int32)),
        grid=(bsz, nc),
        in_specs=[pl.BlockSpec((t, D_MODEL), lambda b, c: (b * nc + c, 0)),
                  const((1, D_MODEL)), const((D_MODEL, D_MODEL)),
                  pl.BlockSpec((mlen, D_MODEL), lambda b, c: (b, 0)),
                  pl.BlockSpec((mlen, D_MODEL), lambda b, c: (b, 0)),
                  const((D_MODEL, D_MODEL)),
                  const((1, D_MODEL)), const((2 * N_EXPERTS, D_MODEL)), const((N_EXPERTS, 1))],
        out_specs=(pl.BlockSpec((t, D_MODEL), lambda b, c: (b * nc + c, 0)),
                   pl.BlockSpec((t, D_MODEL), lambda b, c: (b * nc + c, 0)),
                   pl.BlockSpec((TOP_K, t), lambda b, c: (0, b * nc + c)),
                   pl.BlockSpec((TOP_K, t), lambda b, c: (0, b * nc + c)),
                   pl.BlockSpec((per, N_EXPERTS, 128), lambda b, c: (b * nc + c, 0, 0))),
        compiler_params=pltpu.CompilerParams(
            dimension_semantics=("arbitrary", "arbitrary"), vmem_limit_bytes=VMEM_LIMIT),
        name="attn_route",
    )(h1, g, wq, k, v, wo, g_ffn, wr_t, br)


def _route_tile(h, rows, tile, g_ref, wr_ref, br_ref, xn_ref, lpos_ref, gate_ref, tcnt_ref):
    t = h.shape[0]
    xn = _rms(h, g_ref[...])
    x_hi = xn.astype(BF16)
    x_lo = (xn - x_hi.astype(F32)).astype(BF16)
    xn_ref[rows, :] = x_hi
    dims = (((1,), (1,)), ((), ()))
    p_hi = lax.dot_general(wr_ref[...], x_hi, dims, preferred_element_type=F32)
    p_lo = lax.dot_general(wr_ref[0:N_EXPERTS, :], x_lo, dims, preferred_element_type=F32)
    lg = p_hi[0:N_EXPERTS, :] + (p_hi[N_EXPERTS:, :] + p_lo) + br_ref[...]
    erow = lax.broadcasted_iota(jnp.int32, (N_EXPERTS, t), 0)
    vals, hots = [], []
    for _ in range(TOP_K):
        m = jnp.max(lg, axis=0, keepdims=True)
        ix = jnp.min(jnp.where(lg == m, erow, N_EXPERTS), axis=0, keepdims=True)
        hot = erow == ix
        lg = jnp.where(hot, -jnp.inf, lg)
        vals.append(m)
        hots.append(hot)
    es = [jnp.exp(v - vals[0]) for v in vals]
    den = es[0] + es[1] + es[2] + es[3]
    sel = jnp.zeros((N_EXPERTS, t), F32)
    for hot in hots:
        sel = sel + jnp.where(hot, 1.0, 0.0)
    tri = jnp.where(lax.broadcasted_iota(jnp.int32, (t, t), 0) < lax.broadcasted_iota(jnp.int32, (t, t), 1),
                    1.0, 0.0).astype(BF16)
    rank = jnp.dot(sel.astype(BF16), tri, preferred_element_type=F32)
    cnt = jnp.sum(sel, axis=1, keepdims=True)
    run8 = jnp.floor((cnt + (SUBLANES - 1.0)) * (1.0 / SUBLANES))
    below = jnp.where(lax.broadcasted_iota(jnp.int32, (N_EXPERTS, N_EXPERTS), 1)
                      < lax.broadcasted_iota(jnp.int32, (N_EXPERTS, N_EXPERTS), 0), 1.0, 0.0).astype(BF16)
    start8 = jnp.dot(below, jnp.broadcast_to(run8, (N_EXPERTS, 128)).astype(BF16), preferred_element_type=F32)
    pos = rank + SUBLANES * start8[:, 0:1]
    for k in range(TOP_K):
        gate_ref[k:k + 1, rows] = es[k] / den
        lpos_ref[k:k + 1, rows] = jnp.sum(jnp.where(hots[k], pos, 0.0), axis=0, keepdims=True).astype(jnp.int32)
    tcnt_ref[tile] = jnp.broadcast_to(cnt, (N_EXPERTS, 128)).astype(jnp.int32)


RUN_CHUNK = 64
WAIT_CHUNK = 256


def _run_copies(i, lo_ref, go_ref, n8_ref, make_copy):
    def expert(e, carry):
        idx = i * N_EXPERTS + e
        lo = lo_ref[idx]
        go = go_ref[idx]
        n8 = n8_ref[idx]
        nbig = lax.shift_right_logical(n8, 3)

        def big(j, c):
            make_copy(pl.multiple_of(lo + j * RUN_CHUNK, SUBLANES), pl.multiple_of(go + j * RUN_CHUNK, SUBLANES),
                      RUN_CHUNK).start()
            return c

        lax.fori_loop(0, nbig, big, 0)
        off = nbig * RUN_CHUNK
        for rows in (32, 16, 8):
            has = (n8 & (rows // SUBLANES)) != 0

            @pl.when(has)
            def _(off=off, rows=rows):
                make_copy(pl.multiple_of(lo + off, SUBLANES), pl.multiple_of(go + off, SUBLANES), rows).start()

            off = off + jnp.where(has, rows, 0)
        return carry

    lax.fori_loop(0, N_EXPERTS, expert, 0)


def _wait_rows(n8, make_wait):
    def big(j, c):
        make_wait(WAIT_CHUNK).wait()
        return c

    lax.fori_loop(0, lax.shift_right_logical(n8, 5), big, 0)
    for rows in (128, 64, 32, 16, 8):
        @pl.when((n8 & (rows // SUBLANES)) != 0)
        def _(rows=rows):
            make_wait(rows).wait()


def _dispatch_kernel(lo_ref, go_ref, n8_ref, nt_ref, zs_ref, zn_ref, xn_ref, lpos_ref, xs_hbm, loc, zbuf, sem,
                     sem_z):
    i = pl.program_id(0)
    last = pl.num_programs(0) - 1
    slot = lax.rem(i, 2)
    nloc, t = loc.shape[1], xn_ref.shape[0]

    def row_copy(sl, lo, go, rows):
        return pltpu.make_async_copy(loc.at[sl, pl.ds(lo, rows), :], xs_hbm.at[pl.ds(go, rows), :], sem.at[sl])

    def wait_copies(sl, count):
        _wait_rows(count, lambda rows: row_copy(sl, 0, 0, rows))

    @pl.when(i >= 2)
    def _():
        wait_copies(slot, nt_ref[jnp.maximum(i - 2, 0)])

    mc = 256
    liota = lax.broadcasted_iota(jnp.int32, (mc, t), 0)
    xn = xn_ref[...]
    for r in range(nloc // mc):
        pick = jnp.zeros((mc, t), F32)
        for k in range(TOP_K):
            pick = jnp.where(liota == lpos_ref[k:k + 1, :] - r * mc, 1.0, pick)
        loc[slot, r * mc:(r + 1) * mc, :] = jnp.dot(pick.astype(BF16), xn, preferred_element_type=F32)
    _run_copies(i, lo_ref, go_ref, n8_ref, lambda lo, go, rows: row_copy(slot, lo, go, rows))

    @pl.when(i == last)
    def _():
        wait_copies(slot, nt_ref[i])

        @pl.when(i >= 1)
        def _():
            wait_copies(1 - slot, nt_ref[jnp.maximum(i - 1, 0)])

        zbuf[...] = jnp.zeros_like(zbuf)

        def zero_copy(go):
            return pltpu.make_async_copy(zbuf.at[pl.ds(0, SUBLANES), :], xs_hbm.at[pl.ds(go, SUBLANES), :], sem_z)

        def expert(e, total):
            def chunk(j, carry):
                zero_copy(pl.multiple_of(zs_ref[e] + j * SUBLANES, SUBLANES)).start()
                return carry
            lax.fori_loop(0, zn_ref[e], chunk, 0)
            return total + zn_ref[e]

        total = lax.fori_loop(0, N_EXPERTS, expert, 0)

        def wait_zero(j, carry):
            zero_copy(0).wait()
            return carry

        lax.fori_loop(0, total, wait_zero, 0)

        def block_copy(go):
            return pltpu.make_async_copy(zbuf, xs_hbm.at[pl.ds(go, TM_MOE), :], sem_z)

        def start_block(j, carry):
            block_copy(pl.multiple_of(zs_ref[N_EXPERTS] + j * TM_MOE, TM_MOE)).start()
            return carry

        def wait_block(j, carry):
            block_copy(0).wait()
            return carry

        lax.fori_loop(0, zn_ref[N_EXPERTS], start_block, 0)
        lax.fori_loop(0, zn_ref[N_EXPERTS], wait_block, 0)


def _dispatch(tabs, xn, lpos_t, n_slots, nloc):
    n = xn.shape[0]
    t = min(T_TILE, n)
    grid_spec = pltpu.PrefetchScalarGridSpec(
        num_scalar_prefetch=6,
        grid=(n // t,),
        in_specs=[pl.BlockSpec((t, D_MODEL), lambda i, *_: (i, 0)),
                  pl.BlockSpec((TOP_K, t), lambda i, *_: (0, i))],
        out_specs=pl.BlockSpec(memory_space=pl.ANY),
        scratch_shapes=[pltpu.VMEM((2, nloc, D_MODEL), F32), pltpu.VMEM((TM_MOE, D_MODEL), F32),
                        pltpu.SemaphoreType.DMA((2,)), pltpu.SemaphoreType.DMA],
    )
    return pl.pallas_call(
        _dispatch_kernel,
        out_shape=jax.ShapeDtypeStruct((n_slots, D_MODEL), F32),
        grid_spec=grid_spec,
        compiler_params=pltpu.CompilerParams(dimension_semantics=("arbitrary",), vmem_limit_bytes=VMEM_LIMIT),
        name="dispatch",
    )(tabs["lo"], tabs["go"], tabs["n8"], tabs["nt"], tabs["zs"], tabs["zn"], xn, lpos_t)


def _moe_kernel(bexp_ref, nused_ref, nexp_ref, nrow_ref, x_ref, wgu_hbm, bgu_ref, wd_hbm, bd_ref, o_ref,
                wgu_f, wd_f, wgu_s, wd_s, sem_w):
    b = pl.program_id(0)
    prev = bexp_ref[jnp.maximum(b - 1, 0)]
    fresh = (b == 0) | (bexp_ref[b] != prev)

    def fetch(ex):
        return (pltpu.make_async_copy(wgu_hbm.at[ex], wgu_f, sem_w.at[0]),
                pltpu.make_async_copy(wd_hbm.at[ex], wd_f, sem_w.at[1]))

    @pl.when(b == 0)
    def _():
        for cp in fetch(bexp_ref[0]):
            cp.start()

    @pl.when(fresh & (b < nused_ref[0]))
    def _():
        for cp in fetch(bexp_ref[b]):
            cp.wait()
        rows = 128
        for r0 in range(0, D_MODEL, rows):
            wgu_s[r0:r0 + rows, :] = wgu_f[r0:r0 + rows, :].astype(BF16)
        for r0 in range(0, D_FF, rows):
            wd_s[r0:r0 + rows, :] = wd_f[r0:r0 + rows, :].astype(BF16)

        @pl.when(nexp_ref[b] >= 0)
        def _():
            for cp in fetch(nexp_ref[b]):
                cp.start()

    def mlp(x):
        gu = jnp.dot(x.astype(BF16), wgu_s[...], preferred_element_type=F32) + bgu_ref[0]
        g = jnp.minimum(gu[:, :D_FF], SWIGLU_LIMIT)
        u = jnp.clip(gu[:, D_FF:], -SWIGLU_LIMIT, SWIGLU_LIMIT)
        h = (u + 1.0) * (g * _sigmoid(SWIGLU_ALPHA * g))
        return jnp.dot(h.astype(BF16), wd_s[...], preferred_element_type=F32) + bd_ref[0]

    used = b < nused_ref[0]
    half = TM_MOE // 2
    few = nrow_ref[b] <= half

    @pl.when(used & jnp.logical_not(few))
    def _():
        o_ref[...] = mlp(x_ref[...])

    @pl.when(used & few)
    def _():
        o_ref[0:half, :] = mlp(x_ref[0:half, :])
        o_ref[half:, :] = jnp.zeros((TM_MOE - half, D_MODEL), F32)

    @pl.when(jnp.logical_not(used))
    def _():
        o_ref[...] = jnp.zeros_like(o_ref)


def _moe(block_exp, n_used, next_exp, block_rows, xs, w_gu, b_gu, w_down, b_down):
    n_slots = xs.shape[0]
    nb = n_slots // TM_MOE
    bspec = lambda shape: pl.BlockSpec(shape, lambda b, be, *_: (be[b], 0, 0))
    grid_spec = pltpu.PrefetchScalarGridSpec(
        num_scalar_prefetch=4,
        grid=(nb,),
        in_specs=[
            pl.BlockSpec((TM_MOE, D_MODEL), lambda b, be, nu, *_: (jnp.minimum(b, nu[0] - 1), 0)),
            pl.BlockSpec(memory_space=pl.ANY), bspec((1, 1, 2 * D_FF)),
            pl.BlockSpec(memory_space=pl.ANY), bspec((1, 1, D_MODEL)),
        ],
        out_specs=pl.BlockSpec((TM_MOE, D_MODEL), lambda b, *_: (b, 0)),
        scratch_shapes=[pltpu.VMEM((D_MODEL, 2 * D_FF), F32), pltpu.VMEM((D_FF, D_MODEL), F32),
                        pltpu.VMEM((D_MODEL, 2 * D_FF), BF16), pltpu.VMEM((D_FF, D_MODEL), BF16),
                        pltpu.SemaphoreType.DMA((2,))],
    )
    return pl.pallas_call(
        _moe_kernel,
        out_shape=jax.ShapeDtypeStruct((n_slots, D_MODEL), F32),
        grid_spec=grid_spec,
        compiler_params=pltpu.CompilerParams(dimension_semantics=("arbitrary",), vmem_limit_bytes=VMEM_LIMIT),
        name="moe",
    )(block_exp, n_used, next_exp, block_rows, xs, w_gu, b_gu, w_down, b_down)


def _combine_kernel(lo_ref, go_ref, n8_ref, nt_ref, ys_hbm, h_ref, lpos_ref, gate_ref, g_ref, o_ref, loc, lpb,
                    gtb, sem):
    i = pl.program_id(0)
    slot = lax.rem(i, 2)
    nloc, t = loc.shape[1], h_ref.shape[0]

    def row_copy(sl, lo, go, rows):
        return pltpu.make_async_copy(ys_hbm.at[pl.ds(go, rows), :], loc.at[sl, pl.ds(lo, rows), :], sem.at[sl])

    def fetch(tile, sl):
        _run_copies(tile, lo_ref, go_ref, n8_ref, lambda lo, go, rows: row_copy(sl, lo, go, rows))

        def zero_rows(j, carry):
            r0 = pl.multiple_of((nt_ref[tile] + j) * SUBLANES, SUBLANES)
            loc[sl, pl.ds(r0, SUBLANES), :] = jnp.zeros((SUBLANES, D_MODEL), F32)
            return carry

        lax.fori_loop(0, nloc // SUBLANES - nt_ref[tile], zero_rows, 0)

    @pl.when(i == 0)
    def _():
        fetch(0, 0)

    @pl.when(i + 1 < pl.num_programs(0))
    def _():
        fetch(i + 1, 1 - slot)

    _wait_rows(nt_ref[i], lambda rows: row_copy(slot, 0, 0, rows))
    kc = 256
    lane = lax.broadcasted_iota(jnp.int32, (t, 128), 1)
    for k in range(TOP_K):
        lpb[k] = jnp.broadcast_to(lpos_ref[:, k:k + 1], (t, 128))
        gtb[k] = jnp.broadcast_to(gate_ref[:, k:k + 1], (t, 128))
    acc = h_ref[...]
    for c in range(nloc // kc):
        pieces = []
        for j in range(kc // 128):
            li = lane + (c * kc + j * 128)
            w = jnp.zeros((t, 128), F32)
            for k in range(TOP_K):
                w = jnp.where(li == lpb[k], gtb[k], w)
            pieces.append(w)
        wc = jnp.concatenate(pieces, axis=1).astype(BF16)
        acc = acc + jnp.dot(wc, loc[slot, c * kc:(c + 1) * kc, :].astype(BF16), preferred_element_type=F32)
    o_ref[...] = _rms(acc, g_ref[...])


def _combine(tabs, ys, h2, lpos_tok, gates_tok, g, nloc):
    n = h2.shape[0]
    t = min(T_TILE, n)
    grid_spec = pltpu.PrefetchScalarGridSpec(
        num_scalar_prefetch=4,
        grid=(n // t,),
        in_specs=[pl.BlockSpec(memory_space=pl.ANY),
                  pl.BlockSpec((t, D_MODEL), lambda i, *_: (i, 0)),
                  pl.BlockSpec((t, TOP_K), lambda i, *_: (i, 0)),
                  pl.BlockSpec((t, TOP_K), lambda i, *_: (i, 0)),
                  pl.BlockSpec((1, D_MODEL), lambda i, *_: (0, 0))],
        out_specs=pl.BlockSpec((t, D_MODEL), lambda i, *_: (i, 0)),
        scratch_shapes=[pltpu.VMEM((2, nloc, D_MODEL), F32), pltpu.VMEM((TOP_K, t, 128), jnp.int32),
                        pltpu.VMEM((TOP_K, t, 128), F32), pltpu.SemaphoreType.DMA((2,))],
    )
    return pl.pallas_call(
        _combine_kernel,
        out_shape=jax.ShapeDtypeStruct((n, D_MODEL), F32),
        grid_spec=grid_spec,
        compiler_params=pltpu.CompilerParams(dimension_semantics=("arbitrary",), vmem_limit_bytes=VMEM_LIMIT),
        name="combine",
    )(tabs["lo"], tabs["go"], tabs["n8"], tabs["nt"], ys, h2, lpos_tok, gates_tok, g)


def _cmul(a, b):
    return a[0] * b[0] - a[1] * b[1], a[0] * b[1] + a[1] * b[0]


def _prep_mixer_params(sub_len, norm_mix_g, w_in, conv_w, conv_b, rg_wa, rg_ba, rg_wx, rg_bx, rg_lambda,
                       s5_lambda_re, s5_lambda_im, s5_b_re, s5_b_im, s5_c_re, s5_c_im, s5_d, s5_log_dt,
                       s5_glu_w, s5_glu_b, w_out):
    row = lambda v: v.reshape(1, -1).astype(F32)
    dt = jnp.exp(s5_log_dt)[:, None]
    mag = jnp.exp(dt * s5_lambda_re)
    abar_re = mag * jnp.cos(dt * s5_lambda_im)
    abar_im = mag * jnp.sin(dt * s5_lambda_im)
    den = s5_lambda_re * s5_lambda_re + s5_lambda_im * s5_lambda_im
    num_re = abar_re - 1.0
    coef_re = (num_re * s5_lambda_re + abar_im * s5_lambda_im) / den
    coef_im = (abar_im * s5_lambda_re - num_re * s5_lambda_im) / den
    bbar_re = coef_re[..., None] * s5_b_re - coef_im[..., None] * s5_b_im
    bbar_im = coef_re[..., None] * s5_b_im + coef_im[..., None] * s5_b_re
    a1 = (abar_re.reshape(-1), abar_im.reshape(-1))
    aq = a1
    for _ in range(int(math.log2(sub_len))):
        aq = _cmul(aq, aq)
    pw = [aq]
    for _ in range(SUBLANES - 1):
        pw.append(_cmul(pw[-1], aq))
    sub = jnp.arange(SUBLANES)[:, None]
    tabs = []
    for d in (1, 2, 4):
        keep = (sub >= d).astype(F32)
        tabs += [keep * pw[d - 1][0][None, :], keep * pw[d - 1][1][None, :]]
    tabs += [jnp.stack([p_[0] for p_ in pw]), jnp.stack([p_[1] for p_ in pw])]
    tabs += [jnp.broadcast_to(a1[0][None, :], (SUBLANES, N_STATE)),
             jnp.broadcast_to(a1[1][None, :], (SUBLANES, N_STATE))]
    s5tab = jnp.stack(tabs).astype(F32)
    gh = S5_GROUPS // 2
    bmat, cmat = [], []
    for hf in range(2):
        gs = slice(hf * gh, (hf + 1) * gh)
        b_re = _block_diag(jnp.transpose(bbar_re[gs], (0, 2, 1)))
        b_im = _block_diag(jnp.transpose(bbar_im[gs], (0, 2, 1)))
        bmat.append(jnp.concatenate([b_re, b_im], axis=1))
        cr = _block_diag(jnp.transpose(s5_c_re[gs], (0, 2, 1)))
        ci = _block_diag(jnp.transpose(s5_c_im[gs], (0, 2, 1)))
        cmat.append(jnp.concatenate([cr, -ci], axis=0))
    return {
        "g_mix": row(norm_mix_g), "w_in": w_in.astype(BF16), "conv_w": conv_w.astype(F32), "conv_b": row(conv_b),
        "w_gates": jnp.concatenate([_block_diag(rg_wa), _block_diag(rg_wx)], axis=1).astype(BF16),
        "b_gates": jnp.concatenate([rg_ba, rg_bx]).reshape(1, -1).astype(F32),
        "sp": row(jax.nn.softplus(-rg_lambda)),
        "bmat": jnp.stack(bmat).astype(BF16), "s5tab": s5tab, "cmat": jnp.stack(cmat).astype(BF16),
        "d_skip": row(s5_d), "glu_w": _block_diag(s5_glu_w).astype(BF16), "glu_b": row(s5_glu_b),
        "w_out": w_out.astype(BF16),
    }


def _slot_tables(tcnt, n_slots):
    i32 = lambda v: v.astype(jnp.int32)
    run = ((tcnt + SUBLANES - 1) // SUBLANES) * SUBLANES
    rows = jnp.sum(run, axis=0)
    blocks = (rows + TM_MOE - 1) // TM_MOE
    bend = jnp.cumsum(blocks)
    estart = (bend - blocks) * TM_MOE
    nb = n_slots // TM_MOE
    n_used = i32(bend[-1])
    eidx = jnp.arange(N_EXPERTS)
    bidx = jnp.arange(nb)[:, None]
    owner = (bidx >= (bend - blocks)[None, :]) & (bidx < bend[None, :])
    pick = lambda per_expert: jnp.sum(jnp.where(owner, per_expert[None, :], 0), axis=1)
    last_exp = jnp.max(jnp.where(blocks > 0, eidx, 0))
    block_exp = jnp.where(jnp.arange(nb) < n_used, pick(eidx), last_exp)
    later = (eidx[None, :] > eidx[:, None]) & (blocks[None, :] > 0)
    next_of = jnp.min(jnp.where(later, eidx[None, :], N_EXPERTS), axis=1)
    next_exp = jnp.where(jnp.arange(nb) < n_used, pick(jnp.where(next_of < N_EXPERTS, next_of, -1)), -1)
    block_rows = jnp.clip(pick(rows + (bend - blocks) * TM_MOE) - jnp.arange(nb) * TM_MOE, 0, TM_MOE)
    tabs = {
        "lo": i32(jnp.cumsum(run, axis=1) - run).reshape(-1),
        "go": i32(estart[None, :] + jnp.cumsum(run, axis=0) - run).reshape(-1),
        "n8": i32(run // SUBLANES).reshape(-1),
        "nt": i32(jnp.sum(run, axis=1) // SUBLANES),
        "zs": i32(jnp.concatenate([estart + rows, bend[-1:] * TM_MOE])),
        "zn": i32(jnp.concatenate([(blocks * TM_MOE - rows) // SUBLANES, nb - bend[-1:]])),
    }
    return i32(block_exp), n_used.reshape(1), i32(next_exp), i32(block_rows), tabs


def kernel(x, mem, norm_mix_g, w_in, conv_w, conv_b, rg_wa, rg_ba, rg_wx, rg_bx, rg_lambda, s5_lambda_re,
           s5_lambda_im, s5_b_re, s5_b_im, s5_c_re, s5_c_im, s5_d, s5_log_dt, s5_glu_w, s5_glu_b, w_out,
           norm_xa_g, mem_norm_g, xa_wq, xa_wkv, xa_wo, norm_ffn_g, router_w, router_b, exp_w_gu, exp_b_gu,
           exp_w_down, exp_b_down, final_norm_g):
    bsz, seq, d = x.shape
    mlen = mem.shape[1]
    n = bsz * seq
    row = lambda v: v.reshape(1, -1).astype(F32)

    p = _prep_mixer_params(min(T_MIX, seq) // SUBLANES, norm_mix_g, w_in, conv_w, conv_b, rg_wa, rg_ba, rg_wx,
                           rg_bx, rg_lambda, s5_lambda_re, s5_lambda_im, s5_b_re, s5_b_im, s5_c_re, s5_c_im,
                           s5_d, s5_log_dt, s5_glu_w, s5_glu_b, w_out)
    h1 = _mixer(x.reshape(n, d), bsz, seq, p)

    k, v = _kv(mem.reshape(bsz * mlen, d), bsz, mlen, row(mem_norm_g), xa_wkv.astype(BF16))
    wr_t = router_w.T.astype(F32)
    wr_hi = wr_t.astype(BF16)
    wr_lo = (wr_t - wr_hi.astype(F32)).astype(BF16)
    h2, xn, lpos_t, gate_t, tcnt = _attn_route(
        h1, bsz, seq, mlen, row(norm_xa_g), xa_wq.astype(BF16), k, v, xa_wo.astype(BF16),
        row(norm_ffn_g), jnp.concatenate([wr_hi, wr_lo], axis=0), router_b.reshape(-1, 1).astype(F32))
    tile = min(T_TILE, min(T_ATT, seq))
    n_tiles = n // tile
    run_pad = N_EXPERTS * (SUBLANES - 1)
    nloc = -(-(TOP_K * tile + run_pad) // 128) * 128
    n_slots = -(-(TOP_K * n + n_tiles * run_pad + N_EXPERTS * (TM_MOE - SUBLANES)) // TM_MOE) * TM_MOE
    block_exp, n_used, next_exp, block_rows, tabs = _slot_tables(tcnt[:, :, 0], n_slots)
    xs = _dispatch(tabs, xn, lpos_t, n_slots, nloc)
    ys = _moe(block_exp, n_used, next_exp, block_rows, xs, exp_w_gu, exp_b_gu.reshape(N_EXPERTS, 1, -1),
              exp_w_down, exp_b_down.reshape(N_EXPERTS, 1, -1))
    out = _combine(tabs, ys, h2, lpos_t.T, gate_t.T, row(final_norm_g), nloc)
    return out.reshape(bsz, seq, d)
```

```python
import math

import jax
import jax.numpy as jnp
from jax import lax
from jax.experimental import pallas as pl
from jax.experimental.pallas import tpu as pltpu

F32 = jnp.float32
BF16 = jnp.bfloat16

D_MODEL = 1024
D_RG = 512
D_S5 = 512
CONV_W = 4
RG_C = 8.0
S5_GROUPS = 32
S5_STATE = 64
N_STATE = S5_GROUPS * S5_STATE
XA_HEADS = 4
XA_HD = 256
N_EXPERTS = 32
TOP_K = 4
D_FF = 1024
SWIGLU_LIMIT = 7.0
SWIGLU_ALPHA = 1.702
NORM_EPS = 1e-6

SUBLANES = 8
T_MIX = 512
T_ATT = 1024
T_TILE = 512
TM_MOE = 512
VMEM_LIMIT = 56 * 1024 * 1024


def _rms(x, g):
    return x * lax.rsqrt(jnp.mean(x * x, axis=-1, keepdims=True) + NORM_EPS) * g


def _gelu_tanh(x):
    return 0.5 * x * (1.0 + jnp.tanh(math.sqrt(2.0 / math.pi) * (x + 0.044715 * (x * x * x))))


def _sigmoid(x):
    return 1.0 / (1.0 + jnp.exp(-x))


def _block_diag(blocks):
    n, r, c = blocks.shape
    eye = jnp.eye(n, dtype=blocks.dtype)
    return jnp.einsum('nrc,nm->nrmc', blocks, eye).reshape(n * r, n * c)


def _sublane_chain(al, e, carry, sub):
    for d in (1, 2, 4):
        al_sh = jnp.where(sub >= d, pltpu.roll(al, d, axis=0), 1.0)
        e_sh = jnp.where(sub >= d, pltpu.roll(e, d, axis=0), 0.0)
        e = al * e_sh + e
        al = al * al_sh
    s_end = e + al * carry
    start = jnp.where(sub == 0, carry, pltpu.roll(s_end, 1, axis=0))
    return start, s_end[SUBLANES - 1:SUBLANES, :]


N_ROWS = 2


def _mixer_kernel(x_ref, g_ref, win_ref, convw_ref, convb_ref, wg_ref, bg_ref, sp_ref,
                   bmat_ref, tab_ref, cmat_ref, dskip_ref, glu_ref, glub_ref, wout_ref,
                   o_ref,
                   xin, xout, sem_i, sem_o, xbuf, hist, a_s, b_s, rg_carry, sre, sim, c_re, c_im):
    ngrp = xin.shape[2]
    t = ngrp * SUBLANES
    nh = (CONV_W - 1) * SUBLANES
    nc = pl.num_programs(1)
    c = pl.program_id(1)
    step = pl.program_id(0) * nc + c
    nsteps = pl.num_programs(0) * nc
    slot = lax.rem(step, 2)

    def row_base(s, r):
        return ((N_ROWS * lax.div(s, nc) + r) * nc + lax.rem(s, nc)) * t

    def in_copy(s, sl, r, i):
        return pltpu.make_async_copy(x_ref.at[pl.ds(row_base(s, r) + i * ngrp, ngrp), :], xin.at[r, sl, :, i, :],
                                     sem_i.at[r, sl])

    def out_copy(s, sl, r, i):
        return pltpu.make_async_copy(xout.at[r, sl, :, i, :], o_ref.at[pl.ds(row_base(s, r) + i * ngrp, ngrp), :],
                                     sem_o.at[r, sl])

    @pl.when(step == 0)
    def _():
        for r in range(N_ROWS):
            for i in range(SUBLANES):
                in_copy(0, 0, r, i).start()

    @pl.when(c == 0)
    def _():
        hist[...] = jnp.zeros_like(hist)
        rg_carry[...] = jnp.zeros_like(rg_carry)
        c_re[...] = jnp.zeros_like(c_re)
        c_im[...] = jnp.zeros_like(c_im)

    for r in range(N_ROWS):
        for i in range(SUBLANES):
            in_copy(step, slot, r, i).wait()

    @pl.when(step + 1 < nsteps)
    def _():
        for r in range(N_ROWS):
            for i in range(SUBLANES):
                in_copy(step + 1, 1 - slot, r, i).start()

    @pl.when(step >= 2)
    def _():
        for r in range(N_ROWS):
            for i in range(SUBLANES):
                out_copy(step - 2, slot, r, i).wait()

    sub = lax.broadcasted_iota(jnp.int32, (SUBLANES, D_RG), 0)
    half = N_STATE // 2
    lb = 512
    subl = lax.broadcasted_iota(jnp.int32, (SUBLANES, lb), 0)
    grp = lambda q: slice(q * SUBLANES, (q + 1) * SUBLANES)

    def front(r):
        x = xin[r, slot].reshape(t, D_MODEL)
        xn = _rms(x, g_ref[...]).astype(BF16)
        proj = jnp.dot(xn, win_ref[...], preferred_element_type=F32)
        gate_rg = proj[:, :D_RG]
        x_rg = proj[:, D_RG:2 * D_RG]
        u = proj[:, 2 * D_RG:]
        cur_tail = x_rg[t - nh:, :]
        for gidx in range(CONV_W - 1):
            rs = slice(gidx * SUBLANES, (gidx + 1) * SUBLANES)
            xbuf[r, rs, :] = jnp.where(sub == 0, pltpu.roll(hist[r, rs, :], 1, axis=0),
                                       pltpu.roll(cur_tail[rs, :], 1, axis=0))
        hist[r] = cur_tail
        xbuf[r, nh:, :] = x_rg
        xc = convb_ref[...] + convw_ref[CONV_W - 1:CONV_W, :] * x_rg
        for k in range(CONV_W - 1):
            xc = xc + convw_ref[k:k + 1, :] * xbuf[r, k * SUBLANES:k * SUBLANES + t, :]
        gates = jnp.dot(xc.astype(BF16), wg_ref[...], preferred_element_type=F32) + bg_ref[...]
        rgate = _sigmoid(gates[:, :D_RG])
        igate = _sigmoid(gates[:, D_RG:])
        a = jnp.exp((-RG_C) * rgate * sp_ref[...])
        mult = jnp.sqrt(1.0 - a * a)
        row = lax.broadcasted_iota(jnp.int32, (t, D_RG), 0)
        mult = jnp.where((row == 0) & (c == 0), 1.0, mult)
        a_s[r] = a
        b_s[r] = mult * (igate * xc)
        ub = u.astype(BF16)
        for hf in range(2):
            bu = jnp.dot(ub[:, hf * 256:(hf + 1) * 256], bmat_ref[hf], preferred_element_type=F32)
            sre[r, :, hf * half:(hf + 1) * half] = bu[:, :half]
            sim[r, :, hf * half:(hf + 1) * half] = bu[:, half:]
        return x, gate_rg, u

    def scans(r):
        h = jnp.zeros((SUBLANES, D_RG), F32)
        ac = jnp.ones((SUBLANES, D_RG), F32)
        for q in range(ngrp):
            aa = a_s[r, grp(q), :]
            h = aa * h + b_s[r, grp(q), :]
            ac = aa * ac
        h, last = _sublane_chain(ac, h, rg_carry[r], sub)
        rg_carry[r] = last
        for q in range(ngrp):
            h = a_s[r, grp(q), :] * h + b_s[r, grp(q), :]
            b_s[r, grp(q), :] = h
        for blk in range(N_STATE // lb):
            cs = slice(blk * lb, (blk + 1) * lb)
            ar = tab_ref[8, :, cs]
            ai = tab_ref[9, :, cs]
            er = jnp.zeros((SUBLANES, lb), F32)
            ei = jnp.zeros((SUBLANES, lb), F32)
            for q in range(ngrp):
                er, ei = (ar * er - ai * ei + sre[r, grp(q), cs], ar * ei + ai * er + sim[r, grp(q), cs])
            for i, d in enumerate((1, 2, 4)):
                mr = tab_ref[2 * i, :, cs]
                mi = tab_ref[2 * i + 1, :, cs]
                rr = pltpu.roll(er, d, axis=0)
                ri = pltpu.roll(ei, d, axis=0)
                er, ei = er + (mr * rr - mi * ri), ei + (mr * ri + mi * rr)
            pr = tab_ref[6, :, cs]
            pi_ = tab_ref[7, :, cs]
            cr = c_re[r, :, cs]
            ci = c_im[r, :, cs]
            sr = er + (pr * cr - pi_ * ci)
            si = ei + (pr * ci + pi_ * cr)
            c_re[r, :, cs] = sr[SUBLANES - 1:SUBLANES, :]
            c_im[r, :, cs] = si[SUBLANES - 1:SUBLANES, :]
            hr = jnp.where(subl == 0, cr, pltpu.roll(sr, 1, axis=0))
            hi = jnp.where(subl == 0, ci, pltpu.roll(si, 1, axis=0))
            for q in range(ngrp):
                hr, hi = (ar * hr - ai * hi + sre[r, grp(q), cs], ar * hi + ai * hr + sim[r, grp(q), cs])
                sre[r, grp(q), cs] = hr
                sim[r, grp(q), cs] = hi

    def back(r, x, gate_rg, u):
        y_rg = b_s[r] * _gelu_tanh(gate_rg)
        ys = []
        for hf in range(2):
            hcat = jnp.concatenate([sre[r, :, hf * half:(hf + 1) * half], sim[r, :, hf * half:(hf + 1) * half]],
                                   axis=1)
            ys.append(jnp.dot(hcat.astype(BF16), cmat_ref[hf], preferred_element_type=F32))
        y = jnp.concatenate(ys, axis=1) + dskip_ref[...] * u
        z = _gelu_tanh(y)
        zg = _sigmoid(jnp.dot(z.astype(BF16), glu_ref[...], preferred_element_type=F32) + glub_ref[...])
        ycat = jnp.concatenate([y_rg, z * zg], axis=1).astype(BF16)
        res = x + jnp.dot(ycat, wout_ref[...], preferred_element_type=F32)
        xout[r, slot] = res.reshape(ngrp, SUBLANES, D_MODEL)
        for i in range(SUBLANES):
            out_copy(step, slot, r, i).start()

    kept = [front(r) for r in range(N_ROWS)]
    for r in range(N_ROWS):
        scans(r)
    for r in range(N_ROWS):
        back(r, *kept[r])

    @pl.when(step == nsteps - 1)
    def _():
        for r in range(N_ROWS):
            for i in range(SUBLANES):
                out_copy(step, slot, r, i).wait()

        @pl.when(step >= 1)
        def _():
            for r in range(N_ROWS):
                for i in range(SUBLANES):
                    out_copy(step - 1, 1 - slot, r, i).wait()


def _mixer(x2, bsz, seq, p):
    t = min(T_MIX, seq)
    nc = seq // t
    nh = (CONV_W - 1) * SUBLANES
    assert bsz % N_ROWS == 0
    const = lambda shape: pl.BlockSpec(shape, lambda b, c: (0,) * len(shape), pipeline_mode=pl.Buffered(1))
    per_row = lambda *shape: pltpu.VMEM((N_ROWS,) + shape, F32)
    return pl.pallas_call(
        _mixer_kernel,
        out_shape=jax.ShapeDtypeStruct(x2.shape, F32),
        grid=(bsz // N_ROWS, nc),
        in_specs=[
            pl.BlockSpec(memory_space=pl.ANY),
            const((1, D_MODEL)), const((D_MODEL, 3 * D_RG)), const((CONV_W, D_RG)), const((1, D_RG)),
            const((D_RG, 2 * D_RG)), const((1, 2 * D_RG)), const((1, D_RG)),
            const((2, 256, N_STATE)), const((10, SUBLANES, N_STATE)), const((2, N_STATE, 256)),
            const((1, D_S5)), const((D_S5, D_S5)), const((1, D_S5)), const((D_MODEL, D_MODEL)),
        ],
        out_specs=pl.BlockSpec(memory_space=pl.ANY),
        scratch_shapes=[
            per_row(2, t // SUBLANES, SUBLANES, D_MODEL), per_row(2, t // SUBLANES, SUBLANES, D_MODEL),
            pltpu.SemaphoreType.DMA((N_ROWS, 2)), pltpu.SemaphoreType.DMA((N_ROWS, 2)),
            per_row(t + nh, D_RG), per_row(nh, D_RG),
            per_row(t, D_RG), per_row(t, D_RG), per_row(1, D_RG),
            per_row(t, N_STATE), per_row(t, N_STATE),
            per_row(1, N_STATE), per_row(1, N_STATE),
        ],
        compiler_params=pltpu.CompilerParams(
            dimension_semantics=("arbitrary", "arbitrary"), vmem_limit_bytes=VMEM_LIMIT),
        name="mixer",
    )(x2, p["g_mix"], p["w_in"], p["conv_w"], p["conv_b"], p["w_gates"], p["b_gates"], p["sp"],
      p["bmat"], p["s5tab"], p["cmat"], p["d_skip"], p["glu_w"], p["glu_b"], p["w_out"])


def _kv_kernel(m_ref, g_ref, w_ref, k_ref, v_ref):
    mn = _rms(m_ref[...], g_ref[...]).astype(BF16)
    kv = jnp.dot(mn, w_ref[...], preferred_element_type=F32)
    k_ref[...] = kv[:, :D_MODEL].astype(BF16)
    v_ref[...] = kv[:, D_MODEL:].astype(BF16)


def _kv(mem2, bsz, mlen, g, wkv):
    return pl.pallas_call(
        _kv_kernel,
        out_shape=(jax.ShapeDtypeStruct(mem2.shape, BF16), jax.ShapeDtypeStruct(mem2.shape, BF16)),
        grid=(bsz,),
        in_specs=[pl.BlockSpec((mlen, D_MODEL), lambda b: (b, 0)),
                  pl.BlockSpec((1, D_MODEL), lambda b: (0, 0)),
                  pl.BlockSpec((D_MODEL, 2 * D_MODEL), lambda b: (0, 0))],
        out_specs=(pl.BlockSpec((mlen, D_MODEL), lambda b: (b, 0)),
                   pl.BlockSpec((mlen, D_MODEL), lambda b: (b, 0))),
        compiler_params=pltpu.CompilerParams(dimension_semantics=("arbitrary",), vmem_limit_bytes=VMEM_LIMIT),
        name="kv",
    )(mem2, g, wkv)


def _attn_kernel(h_ref, g_ref, wq_ref, k_ref, v_ref, wo_ref, gf_ref, wr_ref, br_ref,
                 o_ref, xn_ref, lpos_ref, gate_ref, tcnt_ref):
    h = h_ref[...]
    xn = _rms(h, g_ref[...]).astype(BF16)
    q = jnp.dot(xn, wq_ref[...], preferred_element_type=F32).astype(BF16)
    outs = []
    for hd in range(XA_HEADS):
        cs = slice(hd * XA_HD, (hd + 1) * XA_HD)
        s = lax.dot_general(q[:, cs], k_ref[:, cs], (((1,), (1,)), ((), ())),
                            preferred_element_type=F32) * (XA_HD ** -0.5)
        s = s - jnp.max(s, axis=-1, keepdims=True)
        e = jnp.exp(s)
        pr = e / jnp.sum(e, axis=-1, keepdims=True)
        outs.append(jnp.dot(pr.astype(BF16), v_ref[:, cs], preferred_element_type=F32))
    o = jnp.concatenate(outs, axis=1).astype(BF16)
    h2 = h + jnp.dot(o, wo_ref[...], preferred_element_type=F32)
    o_ref[...] = h2
    tile = h2.shape[0] // tcnt_ref.shape[0]
    for j in range(tcnt_ref.shape[0]):
        rows = slice(j * tile, (j + 1) * tile)
        _route_tile(h2[rows, :], rows, j, gf_ref, wr_ref, br_ref, xn_ref, lpos_ref, gate_ref, tcnt_ref)


def _attn_route(h1, bsz, seq, mlen, g, wq, k, v, wo, g_ffn, wr_t, br):
    n = h1.shape[0]
    t = min(T_ATT, seq)
    nc = seq // t
    tile = min(T_TILE, t)
    per = t // tile
    const = lambda shape: pl.BlockSpec(shape, lambda b, c: (0,) * len(shape))
    return pl.pallas_call(
        _attn_kernel,
        out_shape=(jax.ShapeDtypeStruct(h1.shape, F32),
                   jax.ShapeDtypeStruct((n, D_MODEL), BF16),
                   jax.ShapeDtypeStruct((TOP_K, n), jnp.int32),
                   jax.ShapeDtypeStruct((TOP_K, n), F32),
                   jax.ShapeDtypeStruct((n // tile, N_EXPERTS, 128), jnp.int32)),
        grid=(bsz, nc),
        in_specs=[pl.BlockSpec((t, D_MODEL), lambda b, c: (b * nc + c, 0)),
                  const((1, D_MODEL)), const((D_MODEL, D_MODEL)),
                  pl.BlockSpec((mlen, D_MODEL), lambda b, c: (b, 0)),
                  pl.BlockSpec((mlen, D_MODEL), lambda b, c: (b, 0)),
                  const((D_MODEL, D_MODEL)),
                  const((1, D_MODEL)), const((2 * N_EXPERTS, D_MODEL)), const((N_EXPERTS, 1))],
        out_specs=(pl.BlockSpec((t, D_MODEL), lambda b, c: (b * nc + c, 0)),
                   pl.BlockSpec((t, D_MODEL), lambda b, c: (b * nc + c, 0)),
                   pl.BlockSpec((TOP_K, t), lambda b, c: (0, b * nc + c)),
                   pl.BlockSpec((TOP_K, t), lambda b, c: (0, b * nc + c)),
                   pl.BlockSpec((per, N_EXPERTS, 128), lambda b, c: (b * nc + c, 0, 0))),
        compiler_params=pltpu.CompilerParams(
            dimension_semantics=("arbitrary", "arbitrary"), vmem_limit_bytes=VMEM_LIMIT),
        name="attn_route",
    )(h1, g, wq, k, v, wo, g_ffn, wr_t, br)


def _route_tile(h, rows, tile, g_ref, wr_ref, br_ref, xn_ref, lpos_ref, gate_ref, tcnt_ref):
    t = h.shape[0]
    xn = _rms(h, g_ref[...])
    x_hi = xn.astype(BF16)
    x_lo = (xn - x_hi.astype(F32)).astype(BF16)
    xn_ref[rows, :] = x_hi
    dims = (((1,), (1,)), ((), ()))
    p_hi = lax.dot_general(wr_ref[...], x_hi, dims, preferred_element_type=F32)
    p_lo = lax.dot_general(wr_ref[0:N_EXPERTS, :], x_lo, dims, preferred_element_type=F32)
    lg = p_hi[0:N_EXPERTS, :] + (p_hi[N_EXPERTS:, :] + p_lo) + br_ref[...]
    erow = lax.broadcasted_iota(jnp.int32, (N_EXPERTS, t), 0)
    vals, hots = [], []
    for _ in range(TOP_K):
        m = jnp.max(lg, axis=0, keepdims=True)
        ix = jnp.min(jnp.where(lg == m, erow, N_EXPERTS), axis=0, keepdims=True)
        hot = erow == ix
        lg = jnp.where(hot, -jnp.inf, lg)
        vals.append(m)
        hots.append(hot)
    es = [jnp.exp(v - vals[0]) for v in vals]
    den = es[0] + es[1] + es[2] + es[3]
    sel = jnp.zeros((N_EXPERTS, t), F32)
    for hot in hots:
        sel = sel + jnp.where(hot, 1.0, 0.0)
    tri = jnp.where(lax.broadcasted_iota(jnp.int32, (t, t), 0) < lax.broadcasted_iota(jnp.int32, (t, t), 1),
                    1.0, 0.0).astype(BF16)
    rank = jnp.dot(sel.astype(BF16), tri, preferred_element_type=F32)
    cnt = jnp.sum(sel, axis=1, keepdims=True)
    units = jnp.floor((cnt + (RUN_ALIGN - 1.0)) * (1.0 / RUN_ALIGN))
    below = jnp.where(lax.broadcasted_iota(jnp.int32, (N_EXPERTS, N_EXPERTS), 1)
                      < lax.broadcasted_iota(jnp.int32, (N_EXPERTS, N_EXPERTS), 0), 1.0, 0.0).astype(BF16)
    start = jnp.dot(below, jnp.broadcast_to(units, (N_EXPERTS, 128)).astype(BF16), preferred_element_type=F32)
    pos = rank + RUN_ALIGN * start[:, 0:1]
    for k in range(TOP_K):
        gate_ref[k:k + 1, rows] = es[k] / den
        lpos_ref[k:k + 1, rows] = jnp.sum(jnp.where(hots[k], pos, 0.0), axis=0, keepdims=True).astype(jnp.int32)
    tcnt_ref[tile] = jnp.broadcast_to(cnt, (N_EXPERTS, 128)).astype(jnp.int32)


RUN_ALIGN = 16
RUN_CHUNK = 64
WAIT_CHUNK = 256


def _run_copies(i, lo_ref, go_ref, nu_ref, make_copy):
    def expert(e, carry):
        idx = i * N_EXPERTS + e
        lo = lo_ref[idx]
        go = go_ref[idx]
        nu = nu_ref[idx]
        nbig = lax.shift_right_logical(nu, int(math.log2(RUN_CHUNK // RUN_ALIGN)))

        def big(j, c):
            make_copy(pl.multiple_of(lo + j * RUN_CHUNK, RUN_ALIGN), pl.multiple_of(go + j * RUN_CHUNK, RUN_ALIGN),
                      RUN_CHUNK).start()
            return c

        lax.fori_loop(0, nbig, big, 0)
        off = nbig * RUN_CHUNK
        for rows in (32, 16):
            has = (nu & (rows // RUN_ALIGN)) != 0

            @pl.when(has)
            def _(off=off, rows=rows):
                make_copy(pl.multiple_of(lo + off, RUN_ALIGN), pl.multiple_of(go + off, RUN_ALIGN), rows).start()

            off = off + jnp.where(has, rows, 0)
        return carry

    lax.fori_loop(0, N_EXPERTS, expert, 0)


def _wait_rows(nu, make_wait):
    def big(j, c):
        make_wait(WAIT_CHUNK).wait()
        return c

    lax.fori_loop(0, lax.shift_right_logical(nu, int(math.log2(WAIT_CHUNK // RUN_ALIGN))), big, 0)
    for rows in (128, 64, 32, 16):
        @pl.when((nu & (rows // RUN_ALIGN)) != 0)
        def _(rows=rows):
            make_wait(rows).wait()


def _dispatch_kernel(lo_ref, go_ref, n8_ref, nt_ref, zs_ref, zn_ref, xn_ref, lpos_ref, xs_hbm, loc, zbuf, sem,
                     sem_z):
    i = pl.program_id(0)
    last = pl.num_programs(0) - 1
    slot = lax.rem(i, 2)
    nloc, t = loc.shape[1], xn_ref.shape[0]

    def row_copy(sl, lo, go, rows):
        return pltpu.make_async_copy(loc.at[sl, pl.ds(lo, rows), :], xs_hbm.at[pl.ds(go, rows), :], sem.at[sl])

    def wait_copies(sl, count):
        _wait_rows(count, lambda rows: row_copy(sl, 0, 0, rows))

    @pl.when(i >= 2)
    def _():
        wait_copies(slot, nt_ref[jnp.maximum(i - 2, 0)])

    mc = 256
    liota = lax.broadcasted_iota(jnp.int32, (mc, t), 0)
    xn = xn_ref[...]
    for r in range(nloc // mc):
        pick = jnp.zeros((mc, t), F32)
        for k in range(TOP_K):
            pick = jnp.where(liota == lpos_ref[k:k + 1, :] - r * mc, 1.0, pick)
        loc[slot, r * mc:(r + 1) * mc, :] = jnp.dot(pick.astype(BF16), xn, preferred_element_type=F32).astype(BF16)
    _run_copies(i, lo_ref, go_ref, n8_ref, lambda lo, go, rows: row_copy(slot, lo, go, rows))

    @pl.when(i == last)
    def _():
        wait_copies(slot, nt_ref[i])

        @pl.when(i >= 1)
        def _():
            wait_copies(1 - slot, nt_ref[jnp.maximum(i - 1, 0)])

        zbuf[...] = jnp.zeros_like(zbuf)

        def zero_copy(go):
            return pltpu.make_async_copy(zbuf.at[pl.ds(0, RUN_ALIGN), :], xs_hbm.at[pl.ds(go, RUN_ALIGN), :], sem_z)

        def expert(e, total):
            def chunk(j, carry):
                zero_copy(pl.multiple_of(zs_ref[e] + j * RUN_ALIGN, RUN_ALIGN)).start()
                return carry
            lax.fori_loop(0, zn_ref[e], chunk, 0)
            return total + zn_ref[e]

        total = lax.fori_loop(0, N_EXPERTS, expert, 0)

        def wait_zero(j, carry):
            zero_copy(0).wait()
            return carry

        lax.fori_loop(0, total, wait_zero, 0)

        def block_copy(go):
            return pltpu.make_async_copy(zbuf, xs_hbm.at[pl.ds(go, TM_MOE), :], sem_z)

        def start_block(j, carry):
            block_copy(pl.multiple_of(zs_ref[N_EXPERTS] + j * TM_MOE, TM_MOE)).start()
            return carry

        def wait_block(j, carry):
            block_copy(0).wait()
            return carry

        lax.fori_loop(0, zn_ref[N_EXPERTS], start_block, 0)
        lax.fori_loop(0, zn_ref[N_EXPERTS], wait_block, 0)


def _dispatch(tabs, xn, lpos_t, n_slots, nloc):
    n = xn.shape[0]
    t = min(T_TILE, n)
    grid_spec = pltpu.PrefetchScalarGridSpec(
        num_scalar_prefetch=6,
        grid=(n // t,),
        in_specs=[pl.BlockSpec((t, D_MODEL), lambda i, *_: (i, 0)),
                  pl.BlockSpec((TOP_K, t), lambda i, *_: (0, i))],
        out_specs=pl.BlockSpec(memory_space=pl.ANY),
        scratch_shapes=[pltpu.VMEM((2, nloc, D_MODEL), BF16), pltpu.VMEM((TM_MOE, D_MODEL), BF16),
                        pltpu.SemaphoreType.DMA((2,)), pltpu.SemaphoreType.DMA],
    )
    return pl.pallas_call(
        _dispatch_kernel,
        out_shape=jax.ShapeDtypeStruct((n_slots, D_MODEL), BF16),
        grid_spec=grid_spec,
        compiler_params=pltpu.CompilerParams(dimension_semantics=("arbitrary",), vmem_limit_bytes=VMEM_LIMIT),
        name="dispatch",
    )(tabs["lo"], tabs["go"], tabs["n8"], tabs["nt"], tabs["zs"], tabs["zn"], xn, lpos_t)


def _moe_kernel(bexp_ref, nused_ref, nexp_ref, nrow_ref, x_ref, wgu_hbm, bgu_ref, wd_hbm, bd_ref, o_ref,
                wgu_f, wd_f, wgu_s, wd_s, sem_w):
    b = pl.program_id(0)
    prev = bexp_ref[jnp.maximum(b - 1, 0)]
    fresh = (b == 0) | (bexp_ref[b] != prev)

    def fetch(ex):
        return (pltpu.make_async_copy(wgu_hbm.at[ex], wgu_f, sem_w.at[0]),
                pltpu.make_async_copy(wd_hbm.at[ex], wd_f, sem_w.at[1]))

    @pl.when(b == 0)
    def _():
        for cp in fetch(bexp_ref[0]):
            cp.start()

    @pl.when(fresh & (b < nused_ref[0]))
    def _():
        for cp in fetch(bexp_ref[b]):
            cp.wait()
        rows = 128
        for r0 in range(0, D_MODEL, rows):
            wgu_s[r0:r0 + rows, :] = wgu_f[r0:r0 + rows, :].astype(BF16)
        for r0 in range(0, D_FF, rows):
            wd_s[r0:r0 + rows, :] = wd_f[r0:r0 + rows, :].astype(BF16)

        @pl.when(nexp_ref[b] >= 0)
        def _():
            for cp in fetch(nexp_ref[b]):
                cp.start()

    def mlp(x):
        gu = jnp.dot(x, wgu_s[...], preferred_element_type=F32) + bgu_ref[0]
        g = jnp.minimum(gu[:, :D_FF], SWIGLU_LIMIT)
        u = jnp.clip(gu[:, D_FF:], -SWIGLU_LIMIT, SWIGLU_LIMIT)
        h = (u + 1.0) * (g * _sigmoid(SWIGLU_ALPHA * g))
        return (jnp.dot(h.astype(BF16), wd_s[...], preferred_element_type=F32) + bd_ref[0]).astype(BF16)

    used = b < nused_ref[0]
    half = TM_MOE // 2
    few = nrow_ref[b] <= half

    @pl.when(used & jnp.logical_not(few))
    def _():
        o_ref[...] = mlp(x_ref[...])

    @pl.when(used & few)
    def _():
        o_ref[0:half, :] = mlp(x_ref[0:half, :])
        o_ref[half:, :] = jnp.zeros((TM_MOE - half, D_MODEL), BF16)

    @pl.when(jnp.logical_not(used))
    def _():
        o_ref[...] = jnp.zeros_like(o_ref)


def _moe(block_exp, n_used, next_exp, block_rows, xs, w_gu, b_gu, w_down, b_down):
    n_slots = xs.shape[0]
    nb = n_slots // TM_MOE
    bspec = lambda shape: pl.BlockSpec(shape, lambda b, be, *_: (be[b], 0, 0))
    grid_spec = pltpu.PrefetchScalarGridSpec(
        num_scalar_prefetch=4,
        grid=(nb,),
        in_specs=[
            pl.BlockSpec((TM_MOE, D_MODEL), lambda b, be, nu, *_: (jnp.minimum(b, nu[0] - 1), 0)),
            pl.BlockSpec(memory_space=pl.ANY), bspec((1, 1, 2 * D_FF)),
            pl.BlockSpec(memory_space=pl.ANY), bspec((1, 1, D_MODEL)),
        ],
        out_specs=pl.BlockSpec((TM_MOE, D_MODEL), lambda b, *_: (b, 0)),
        scratch_shapes=[pltpu.VMEM((D_MODEL, 2 * D_FF), F32), pltpu.VMEM((D_FF, D_MODEL), F32),
                        pltpu.VMEM((D_MODEL, 2 * D_FF), BF16), pltpu.VMEM((D_FF, D_MODEL), BF16),
                        pltpu.SemaphoreType.DMA((2,))],
    )
    return pl.pallas_call(
        _moe_kernel,
        out_shape=jax.ShapeDtypeStruct((n_slots, D_MODEL), BF16),
        grid_spec=grid_spec,
        compiler_params=pltpu.CompilerParams(dimension_semantics=("arbitrary",), vmem_limit_bytes=VMEM_LIMIT),
        name="moe",
    )(block_exp, n_used, next_exp, block_rows, xs, w_gu, b_gu, w_down, b_down)


def _combine_kernel(lo_ref, go_ref, n8_ref, nt_ref, ys_hbm, h_ref, lpos_ref, gate_ref, g_ref, o_ref, loc, lpb,
                    gtb, sem):
    i = pl.program_id(0)
    slot = lax.rem(i, 2)
    nloc, t = loc.shape[1], h_ref.shape[0]

    def row_copy(sl, lo, go, rows):
        return pltpu.make_async_copy(ys_hbm.at[pl.ds(go, rows), :], loc.at[sl, pl.ds(lo, rows), :], sem.at[sl])

    def fetch(tile, sl):
        _run_copies(tile, lo_ref, go_ref, n8_ref, lambda lo, go, rows: row_copy(sl, lo, go, rows))

        def zero_rows(j, carry):
            r0 = pl.multiple_of((nt_ref[tile] + j) * RUN_ALIGN, RUN_ALIGN)
            loc[sl, pl.ds(r0, RUN_ALIGN), :] = jnp.zeros((RUN_ALIGN, D_MODEL), BF16)
            return carry

        lax.fori_loop(0, nloc // RUN_ALIGN - nt_ref[tile], zero_rows, 0)

    @pl.when(i == 0)
    def _():
        fetch(0, 0)

    @pl.when(i + 1 < pl.num_programs(0))
    def _():
        fetch(i + 1, 1 - slot)

    _wait_rows(nt_ref[i], lambda rows: row_copy(slot, 0, 0, rows))
    kc = 256
    lane = lax.broadcasted_iota(jnp.int32, (t, 128), 1)
    for k in range(TOP_K):
        lpb[k] = jnp.broadcast_to(lpos_ref[:, k:k + 1], (t, 128))
        gtb[k] = jnp.broadcast_to(gate_ref[:, k:k + 1], (t, 128))
    acc = h_ref[...]
    for c in range(nloc // kc):
        pieces = []
        for j in range(kc // 128):
            li = lane + (c * kc + j * 128)
            w = jnp.zeros((t, 128), F32)
            for k in range(TOP_K):
                w = jnp.where(li == lpb[k], gtb[k], w)
            pieces.append(w)
        wc = jnp.concatenate(pieces, axis=1).astype(BF16)
        acc = acc + jnp.dot(wc, loc[slot, c * kc:(c + 1) * kc, :], preferred_element_type=F32)
    o_ref[...] = _rms(acc, g_ref[...])


def _combine(tabs, ys, h2, lpos_tok, gates_tok, g, nloc):
    n = h2.shape[0]
    t = min(T_TILE, n)
    grid_spec = pltpu.PrefetchScalarGridSpec(
        num_scalar_prefetch=4,
        grid=(n // t,),
        in_specs=[pl.BlockSpec(memory_space=pl.ANY),
                  pl.BlockSpec((t, D_MODEL), lambda i, *_: (i, 0)),
                  pl.BlockSpec((t, TOP_K), lambda i, *_: (i, 0)),
                  pl.BlockSpec((t, TOP_K), lambda i, *_: (i, 0)),
                  pl.BlockSpec((1, D_MODEL), lambda i, *_: (0, 0))],
        out_specs=pl.BlockSpec((t, D_MODEL), lambda i, *_: (i, 0)),
        scratch_shapes=[pltpu.VMEM((2, nloc, D_MODEL), BF16), pltpu.VMEM((TOP_K, t, 128), jnp.int32),
                        pltpu.VMEM((TOP_K, t, 128), F32), pltpu.SemaphoreType.DMA((2,))],
    )
    return pl.pallas_call(
        _combine_kernel,
        out_shape=jax.ShapeDtypeStruct((n, D_MODEL), F32),
        grid_spec=grid_spec,
        compiler_params=pltpu.CompilerParams(dimension_semantics=("arbitrary",), vmem_limit_bytes=VMEM_LIMIT),
        name="combine",
    )(tabs["lo"], tabs["go"], tabs["n8"], tabs["nt"], ys, h2, lpos_tok, gates_tok, g)


def _cmul(a, b):
    return a[0] * b[0] - a[1] * b[1], a[0] * b[1] + a[1] * b[0]


def _prep_mixer_params(sub_len, norm_mix_g, w_in, conv_w, conv_b, rg_wa, rg_ba, rg_wx, rg_bx, rg_lambda,
                       s5_lambda_re, s5_lambda_im, s5_b_re, s5_b_im, s5_c_re, s5_c_im, s5_d, s5_log_dt,
                       s5_glu_w, s5_glu_b, w_out):
    row = lambda v: v.reshape(1, -1).astype(F32)
    dt = jnp.exp(s5_log_dt)[:, None]
    mag = jnp.exp(dt * s5_lambda_re)
    abar_re = mag * jnp.cos(dt * s5_lambda_im)
    abar_im = mag * jnp.sin(dt * s5_lambda_im)
    den = s5_lambda_re * s5_lambda_re + s5_lambda_im * s5_lambda_im
    num_re = abar_re - 1.0
    coef_re = (num_re * s5_lambda_re + abar_im * s5_lambda_im) / den
    coef_im = (abar_im * s5_lambda_re - num_re * s5_lambda_im) / den
    bbar_re = coef_re[..., None] * s5_b_re - coef_im[..., None] * s5_b_im
    bbar_im = coef_re[..., None] * s5_b_im + coef_im[..., None] * s5_b_re
    a1 = (abar_re.reshape(-1), abar_im.reshape(-1))
    aq = a1
    for _ in range(int(math.log2(sub_len))):
        aq = _cmul(aq, aq)
    pw = [aq]
    for _ in range(SUBLANES - 1):
        pw.append(_cmul(pw[-1], aq))
    sub = jnp.arange(SUBLANES)[:, None]
    tabs = []
    for d in (1, 2, 4):
        keep = (sub >= d).astype(F32)
        tabs += [keep * pw[d - 1][0][None, :], keep * pw[d - 1][1][None, :]]
    tabs += [jnp.stack([p_[0] for p_ in pw]), jnp.stack([p_[1] for p_ in pw])]
    tabs += [jnp.broadcast_to(a1[0][None, :], (SUBLANES, N_STATE)),
             jnp.broadcast_to(a1[1][None, :], (SUBLANES, N_STATE))]
    s5tab = jnp.stack(tabs).astype(F32)
    gh = S5_GROUPS // 2
    bmat, cmat = [], []
    for hf in range(2):
        gs = slice(hf * gh, (hf + 1) * gh)
        b_re = _block_diag(jnp.transpose(bbar_re[gs], (0, 2, 1)))
        b_im = _block_diag(jnp.transpose(bbar_im[gs], (0, 2, 1)))
        bmat.append(jnp.concatenate([b_re, b_im], axis=1))
        cr = _block_diag(jnp.transpose(s5_c_re[gs], (0, 2, 1)))
        ci = _block_diag(jnp.transpose(s5_c_im[gs], (0, 2, 1)))
        cmat.append(jnp.concatenate([cr, -ci], axis=0))
    return {
        "g_mix": row(norm_mix_g), "w_in": w_in.astype(BF16), "conv_w": conv_w.astype(F32), "conv_b": row(conv_b),
        "w_gates": jnp.concatenate([_block_diag(rg_wa), _block_diag(rg_wx)], axis=1).astype(BF16),
        "b_gates": jnp.concatenate([rg_ba, rg_bx]).reshape(1, -1).astype(F32),
        "sp": row(jax.nn.softplus(-rg_lambda)),
        "bmat": jnp.stack(bmat).astype(BF16), "s5tab": s5tab, "cmat": jnp.stack(cmat).astype(BF16),
        "d_skip": row(s5_d), "glu_w": _block_diag(s5_glu_w).astype(BF16), "glu_b": row(s5_glu_b),
        "w_out": w_out.astype(BF16),
    }


def _slot_tables(tcnt, n_slots):
    i32 = lambda v: v.astype(jnp.int32)
    run = ((tcnt + RUN_ALIGN - 1) // RUN_ALIGN) * RUN_ALIGN
    rows = jnp.sum(run, axis=0)
    blocks = (rows + TM_MOE - 1) // TM_MOE
    bend = jnp.cumsum(blocks)
    estart = (bend - blocks) * TM_MOE
    nb = n_slots // TM_MOE
    n_used = i32(bend[-1])
    eidx = jnp.arange(N_EXPERTS)
    bidx = jnp.arange(nb)[:, None]
    owner = (bidx >= (bend - blocks)[None, :]) & (bidx < bend[None, :])
    pick = lambda per_expert: jnp.sum(jnp.where(owner, per_expert[None, :], 0), axis=1)
    last_exp = jnp.max(jnp.where(blocks > 0, eidx, 0))
    block_exp = jnp.where(jnp.arange(nb) < n_used, pick(eidx), last_exp)
    later = (eidx[None, :] > eidx[:, None]) & (blocks[None, :] > 0)
    next_of = jnp.min(jnp.where(later, eidx[None, :], N_EXPERTS), axis=1)
    next_exp = jnp.where(jnp.arange(nb) < n_used, pick(jnp.where(next_of < N_EXPERTS, next_of, -1)), -1)
    block_rows = jnp.clip(pick(rows + (bend - blocks) * TM_MOE) - jnp.arange(nb) * TM_MOE, 0, TM_MOE)
    tabs = {
        "lo": i32(jnp.cumsum(run, axis=1) - run).reshape(-1),
        "go": i32(estart[None, :] + jnp.cumsum(run, axis=0) - run).reshape(-1),
        "n8": i32(run // RUN_ALIGN).reshape(-1),
        "nt": i32(jnp.sum(run, axis=1) // RUN_ALIGN),
        "zs": i32(jnp.concatenate([estart + rows, bend[-1:] * TM_MOE])),
        "zn": i32(jnp.concatenate([(blocks * TM_MOE - rows) // RUN_ALIGN, nb - bend[-1:]])),
    }
    return i32(block_exp), n_used.reshape(1), i32(next_exp), i32(block_rows), tabs


def kernel(x, mem, norm_mix_g, w_in, conv_w, conv_b, rg_wa, rg_ba, rg_wx, rg_bx, rg_lambda, s5_lambda_re,
           s5_lambda_im, s5_b_re, s5_b_im, s5_c_re, s5_c_im, s5_d, s5_log_dt, s5_glu_w, s5_glu_b, w_out,
           norm_xa_g, mem_norm_g, xa_wq, xa_wkv, xa_wo, norm_ffn_g, router_w, router_b, exp_w_gu, exp_b_gu,
           exp_w_down, exp_b_down, final_norm_g):
    bsz, seq, d = x.shape
    mlen = mem.shape[1]
    n = bsz * seq
    row = lambda v: v.reshape(1, -1).astype(F32)

    p = _prep_mixer_params(min(T_MIX, seq) // SUBLANES, norm_mix_g, w_in, conv_w, conv_b, rg_wa, rg_ba, rg_wx,
                           rg_bx, rg_lambda, s5_lambda_re, s5_lambda_im, s5_b_re, s5_b_im, s5_c_re, s5_c_im,
                           s5_d, s5_log_dt, s5_glu_w, s5_glu_b, w_out)
    h1 = _mixer(x.reshape(n, d), bsz, seq, p)

    k, v = _kv(mem.reshape(bsz * mlen, d), bsz, mlen, row(mem_norm_g), xa_wkv.astype(BF16))
    wr_t = router_w.T.astype(F32)
    wr_hi = wr_t.astype(BF16)
    wr_lo = (wr_t - wr_hi.astype(F32)).astype(BF16)
    h2, xn, lpos_t, gate_t, tcnt = _attn_route(
        h1, bsz, seq, mlen, row(norm_xa_g), xa_wq.astype(BF16), k, v, xa_wo.astype(BF16),
        row(norm_ffn_g), jnp.concatenate([wr_hi, wr_lo], axis=0), router_b.reshape(-1, 1).astype(F32))
    tile = min(T_TILE, min(T_ATT, seq))
    n_tiles = n // tile
    run_pad = N_EXPERTS * (RUN_ALIGN - 1)
    nloc = -(-(TOP_K * tile + run_pad) // 128) * 128
    n_slots = -(-(TOP_K * n + n_tiles * run_pad + N_EXPERTS * (TM_MOE - RUN_ALIGN)) // TM_MOE) * TM_MOE
    block_exp, n_used, next_exp, block_rows, tabs = _slot_tables(tcnt[:, :, 0], n_slots)
    xs = _dispatch(tabs, xn, lpos_t, n_slots, nloc)
    ys = _moe(block_exp, n_used, next_exp, block_rows, xs, exp_w_gu, exp_b_gu.reshape(N_EXPERTS, 1, -1),
              exp_w_down, exp_b_down.reshape(N_EXPERTS, 1, -1))
    out = _combine(tabs, ys, h2, lpos_t.T, gate_t.T, row(final_norm_g), nloc)
    return out.reshape(bsz, seq, d)
```

```python
import math

import jax
import jax.numpy as jnp
from jax import lax
from jax.experimental import pallas as pl
from jax.experimental.pallas import tpu as pltpu

F32 = jnp.float32
BF16 = jnp.bfloat16

D_MODEL = 1024
D_RG = 512
D_S5 = 512
CONV_W = 4
RG_C = 8.0
S5_GROUPS = 32
S5_STATE = 64
N_STATE = S5_GROUPS * S5_STATE
XA_HEADS = 4
XA_HD = 256
N_EXPERTS = 32
TOP_K = 4
D_FF = 1024
SWIGLU_LIMIT = 7.0
SWIGLU_ALPHA = 1.702
NORM_EPS = 1e-6

SUBLANES = 8
T_MIX = 512
T_ATT = 1024
T_TILE = 512
TM_MOE = 512
VMEM_LIMIT = 56 * 1024 * 1024


def _rms(x, g):
    return x * lax.rsqrt(jnp.mean(x * x, axis=-1, keepdims=True) + NORM_EPS) * g


def _gelu_tanh(x):
    return 0.5 * x * (1.0 + jnp.tanh(math.sqrt(2.0 / math.pi) * (x + 0.044715 * (x * x * x))))


def _sigmoid(x):
    return 1.0 / (1.0 + jnp.exp(-x))


def _block_diag(blocks):
    n, r, c = blocks.shape
    eye = jnp.eye(n, dtype=blocks.dtype)
    return jnp.einsum('nrc,nm->nrmc', blocks, eye).reshape(n * r, n * c)


def _sublane_chain(al, e, carry, sub):
    for d in (1, 2, 4):
        al_sh = jnp.where(sub >= d, pltpu.roll(al, d, axis=0), 1.0)
        e_sh = jnp.where(sub >= d, pltpu.roll(e, d, axis=0), 0.0)
        e = al * e_sh + e
        al = al * al_sh
    s_end = e + al * carry
    start = jnp.where(sub == 0, carry, pltpu.roll(s_end, 1, axis=0))
    return start, s_end[SUBLANES - 1:SUBLANES, :]


N_ROWS = 2


def _mixer_kernel(x_ref, g_ref, win_ref, convw_ref, convb_ref, wg_ref, bg_ref, sp_ref,
                   bmat_ref, tab_ref, cmat_ref, dskip_ref, glu_ref, glub_ref, wout_ref,
                   o_ref,
                   xin, xout, sem_i, sem_o, xbuf, hist, a_s, b_s, rg_carry, sre, sim, c_re, c_im):
    ngrp = xin.shape[2]
    t = ngrp * SUBLANES
    nh = (CONV_W - 1) * SUBLANES
    nc = pl.num_programs(1)
    c = pl.program_id(1)
    step = pl.program_id(0) * nc + c
    nsteps = pl.num_programs(0) * nc
    slot = lax.rem(step, 2)

    def row_base(s, r):
        return ((N_ROWS * lax.div(s, nc) + r) * nc + lax.rem(s, nc)) * t

    def in_copy(s, sl, r, i):
        return pltpu.make_async_copy(x_ref.at[pl.ds(row_base(s, r) + i * ngrp, ngrp), :], xin.at[r, sl, :, i, :],
                                     sem_i.at[r, sl])

    def out_copy(s, sl, r, i):
        return pltpu.make_async_copy(xout.at[r, sl, :, i, :], o_ref.at[pl.ds(row_base(s, r) + i * ngrp, ngrp), :],
                                     sem_o.at[r, sl])

    @pl.when(step == 0)
    def _():
        for r in range(N_ROWS):
            for i in range(SUBLANES):
                in_copy(0, 0, r, i).start()

    @pl.when(c == 0)
    def _():
        hist[...] = jnp.zeros_like(hist)
        rg_carry[...] = jnp.zeros_like(rg_carry)
        c_re[...] = jnp.zeros_like(c_re)
        c_im[...] = jnp.zeros_like(c_im)

    for r in range(N_ROWS):
        for i in range(SUBLANES):
            in_copy(step, slot, r, i).wait()

    @pl.when(step + 1 < nsteps)
    def _():
        for r in range(N_ROWS):
            for i in range(SUBLANES):
                in_copy(step + 1, 1 - slot, r, i).start()

    @pl.when(step >= 2)
    def _():
        for r in range(N_ROWS):
            for i in range(SUBLANES):
                out_copy(step - 2, slot, r, i).wait()

    sub = lax.broadcasted_iota(jnp.int32, (SUBLANES, D_RG), 0)
    half = N_STATE // 2
    lb = 512
    subl = lax.broadcasted_iota(jnp.int32, (SUBLANES, lb), 0)
    grp = lambda q: slice(q * SUBLANES, (q + 1) * SUBLANES)

    def front(r):
        x = xin[r, slot].reshape(t, D_MODEL)
        xn = _rms(x, g_ref[...]).astype(BF16)
        proj = jnp.dot(xn, win_ref[...], preferred_element_type=F32)
        gate_rg = proj[:, :D_RG]
        x_rg = proj[:, D_RG:2 * D_RG]
        u = proj[:, 2 * D_RG:]
        cur_tail = x_rg[t - nh:, :]
        for gidx in range(CONV_W - 1):
            rs = slice(gidx * SUBLANES, (gidx + 1) * SUBLANES)
            xbuf[r, rs, :] = jnp.where(sub == 0, pltpu.roll(hist[r, rs, :], 1, axis=0),
                                       pltpu.roll(cur_tail[rs, :], 1, axis=0))
        hist[r] = cur_tail
        xbuf[r, nh:, :] = x_rg
        xc = convb_ref[...] + convw_ref[CONV_W - 1:CONV_W, :] * x_rg
        for k in range(CONV_W - 1):
            xc = xc + convw_ref[k:k + 1, :] * xbuf[r, k * SUBLANES:k * SUBLANES + t, :]
        gates = jnp.dot(xc.astype(BF16), wg_ref[...], preferred_element_type=F32) + bg_ref[...]
        rgate = _sigmoid(gates[:, :D_RG])
        igate = _sigmoid(gates[:, D_RG:])
        a = jnp.exp((-RG_C) * rgate * sp_ref[...])
        mult = jnp.sqrt(1.0 - a * a)
        row = lax.broadcasted_iota(jnp.int32, (t, D_RG), 0)
        mult = jnp.where((row == 0) & (c == 0), 1.0, mult)
        a_s[r] = a
        b_s[r] = mult * (igate * xc)
        ub = u.astype(BF16)
        for hf in range(2):
            bu = jnp.dot(ub[:, hf * 256:(hf + 1) * 256], bmat_ref[hf], preferred_element_type=F32)
            sre[r, :, hf * half:(hf + 1) * half] = bu[:, :half]
            sim[r, :, hf * half:(hf + 1) * half] = bu[:, half:]
        return x, gate_rg, u

    def scans(r):
        h = jnp.zeros((SUBLANES, D_RG), F32)
        ac = jnp.ones((SUBLANES, D_RG), F32)
        for q in range(ngrp):
            aa = a_s[r, grp(q), :]
            h = aa * h + b_s[r, grp(q), :]
            ac = aa * ac
        h, last = _sublane_chain(ac, h, rg_carry[r], sub)
        rg_carry[r] = last
        for q in range(ngrp):
            h = a_s[r, grp(q), :] * h + b_s[r, grp(q), :]
            b_s[r, grp(q), :] = h
        for blk in range(N_STATE // lb):
            cs = slice(blk * lb, (blk + 1) * lb)
            ar = tab_ref[8, :, cs]
            ai = tab_ref[9, :, cs]
            er = jnp.zeros((SUBLANES, lb), F32)
            ei = jnp.zeros((SUBLANES, lb), F32)
            for q in range(ngrp):
                er, ei = (ar * er - ai * ei + sre[r, grp(q), cs], ar * ei + ai * er + sim[r, grp(q), cs])
            for i, d in enumerate((1, 2, 4)):
                mr = tab_ref[2 * i, :, cs]
                mi = tab_ref[2 * i + 1, :, cs]
                rr = pltpu.roll(er, d, axis=0)
                ri = pltpu.roll(ei, d, axis=0)
                er, ei = er + (mr * rr - mi * ri), ei + (mr * ri + mi * rr)
            pr = tab_ref[6, :, cs]
            pi_ = tab_ref[7, :, cs]
            cr = c_re[r, :, cs]
            ci = c_im[r, :, cs]
            sr = er + (pr * cr - pi_ * ci)
            si = ei + (pr * ci + pi_ * cr)
            c_re[r, :, cs] = sr[SUBLANES - 1:SUBLANES, :]
            c_im[r, :, cs] = si[SUBLANES - 1:SUBLANES, :]
            hr = jnp.where(subl == 0, cr, pltpu.roll(sr, 1, axis=0))
            hi = jnp.where(subl == 0, ci, pltpu.roll(si, 1, axis=0))
            for q in range(ngrp):
                hr, hi = (ar * hr - ai * hi + sre[r, grp(q), cs], ar * hi + ai * hr + sim[r, grp(q), cs])
                sre[r, grp(q), cs] = hr
                sim[r, grp(q), cs] = hi

    def back(r, x, gate_rg, u):
        y_rg = b_s[r] * _gelu_tanh(gate_rg)
        ys = []
        for hf in range(2):
            hcat = jnp.concatenate([sre[r, :, hf * half:(hf + 1) * half], sim[r, :, hf * half:(hf + 1) * half]],
                                   axis=1)
            ys.append(jnp.dot(hcat.astype(BF16), cmat_ref[hf], preferred_element_type=F32))
        y = jnp.concatenate(ys, axis=1) + dskip_ref[...] * u
        z = _gelu_tanh(y)
        zg = _sigmoid(jnp.dot(z.astype(BF16), glu_ref[...], preferred_element_type=F32) + glub_ref[...])
        ycat = jnp.concatenate([y_rg, z * zg], axis=1).astype(BF16)
        res = x + jnp.dot(ycat, wout_ref[...], preferred_element_type=F32)
        xout[r, slot] = res.reshape(ngrp, SUBLANES, D_MODEL)
        for i in range(SUBLANES):
            out_copy(step, slot, r, i).start()

    kept = [front(r) for r in range(N_ROWS)]
    for r in range(N_ROWS):
        scans(r)
    for r in range(N_ROWS):
        back(r, *kept[r])

    @pl.when(step == nsteps - 1)
    def _():
        for r in range(N_ROWS):
            for i in range(SUBLANES):
                out_copy(step, slot, r, i).wait()

        @pl.when(step >= 1)
        def _():
            for r in range(N_ROWS):
                for i in range(SUBLANES):
                    out_copy(step - 1, 1 - slot, r, i).wait()


def _mixer(x2, bsz, seq, p):
    t = min(T_MIX, seq)
    nc = seq // t
    nh = (CONV_W - 1) * SUBLANES
    assert bsz % N_ROWS == 0
    const = lambda shape: pl.BlockSpec(shape, lambda b, c: (0,) * len(shape), pipeline_mode=pl.Buffered(1))
    per_row = lambda *shape: pltpu.VMEM((N_ROWS,) + shape, F32)
    return pl.pallas_call(
        _mixer_kernel,
        out_shape=jax.ShapeDtypeStruct(x2.shape, F32),
        grid=(bsz // N_ROWS, nc),
        in_specs=[
            pl.BlockSpec(memory_space=pl.ANY),
            const((1, D_MODEL)), const((D_MODEL, 3 * D_RG)), const((CONV_W, D_RG)), const((1, D_RG)),
            const((D_RG, 2 * D_RG)), const((1, 2 * D_RG)), const((1, D_RG)),
            const((2, 256, N_STATE)), const((10, SUBLANES, N_STATE)), const((2, N_STATE, 256)),
            const((1, D_S5)), const((D_S5, D_S5)), const((1, D_S5)), const((D_MODEL, D_MODEL)),
        ],
        out_specs=pl.BlockSpec(memory_space=pl.ANY),
        scratch_shapes=[
            per_row(2, t // SUBLANES, SUBLANES, D_MODEL), per_row(2, t // SUBLANES, SUBLANES, D_MODEL),
            pltpu.SemaphoreType.DMA((N_ROWS, 2)), pltpu.SemaphoreType.DMA((N_ROWS, 2)),
            per_row(t + nh, D_RG), per_row(nh, D_RG),
            per_row(t, D_RG), per_row(t, D_RG), per_row(1, D_RG),
            per_row(t, N_STATE), per_row(t, N_STATE),
            per_row(1, N_STATE), per_row(1, N_STATE),
        ],
        compiler_params=pltpu.CompilerParams(
            dimension_semantics=("arbitrary", "arbitrary"), vmem_limit_bytes=VMEM_LIMIT),
        name="mixer",
    )(x2, p["g_mix"], p["w_in"], p["conv_w"], p["conv_b"], p["w_gates"], p["b_gates"], p["sp"],
      p["bmat"], p["s5tab"], p["cmat"], p["d_skip"], p["glu_w"], p["glu_b"], p["w_out"])


def _kv_kernel(m_ref, g_ref, w_ref, k_ref, v_ref):
    mn = _rms(m_ref[...], g_ref[...]).astype(BF16)
    kv = jnp.dot(mn, w_ref[...], preferred_element_type=F32)
    k_ref[...] = kv[:, :D_MODEL].astype(BF16)
    v_ref[...] = kv[:, D_MODEL:].astype(BF16)


def _kv(mem2, bsz, mlen, g, wkv):
    return pl.pallas_call(
        _kv_kernel,
        out_shape=(jax.ShapeDtypeStruct(mem2.shape, BF16), jax.ShapeDtypeStruct(mem2.shape, BF16)),
        grid=(bsz,),
        in_specs=[pl.BlockSpec((mlen, D_MODEL), lambda b: (b, 0)),
                  pl.BlockSpec((1, D_MODEL), lambda b: (0, 0)),
                  pl.BlockSpec((D_MODEL, 2 * D_MODEL), lambda b: (0, 0))],
        out_specs=(pl.BlockSpec((mlen, D_MODEL), lambda b: (b, 0)),
                   pl.BlockSpec((mlen, D_MODEL), lambda b: (b, 0))),
        compiler_params=pltpu.CompilerParams(dimension_semantics=("arbitrary",), vmem_limit_bytes=VMEM_LIMIT),
        name="kv",
    )(mem2, g, wkv)


def _attn_kernel(h_ref, g_ref, wq_ref, k_ref, v_ref, wo_ref, gf_ref, wr_ref, br_ref,
                 o_ref, xn_ref, lpos_ref, gate_ref, tcnt_ref):
    h = h_ref[...]
    xn = _rms(h, g_ref[...]).astype(BF16)
    q = jnp.dot(xn, wq_ref[...], preferred_element_type=F32).astype(BF16)
    outs = []
    for hd in range(XA_HEADS):
        cs = slice(hd * XA_HD, (hd + 1) * XA_HD)
        s = lax.dot_general(q[:, cs], k_ref[:, cs], (((1,), (1,)), ((), ())),
                            preferred_element_type=F32) * (XA_HD ** -0.5)
        s = s - jnp.max(s, axis=-1, keepdims=True)
        e = jnp.exp(s)
        pr = e / jnp.sum(e, axis=-1, keepdims=True)
        outs.append(jnp.dot(pr.astype(BF16), v_ref[:, cs], preferred_element_type=F32))
    o = jnp.concatenate(outs, axis=1).astype(BF16)
    h2 = h + jnp.dot(o, wo_ref[...], preferred_element_type=F32)
    o_ref[...] = h2
    tile = h2.shape[0] // tcnt_ref.shape[0]
    for j in range(tcnt_ref.shape[0]):
        rows = slice(j * tile, (j + 1) * tile)
        _route_tile(h2[rows, :], rows, j, gf_ref, wr_ref, br_ref, xn_ref, lpos_ref, gate_ref, tcnt_ref)


def _attn_route(h1, bsz, seq, mlen, g, wq, k, v, wo, g_ffn, wr_t, br):
    n = h1.shape[0]
    t = min(T_ATT, seq)
    nc = seq // t
    tile = min(T_TILE, t)
    per = t // tile
    const = lambda shape: pl.BlockSpec(shape, lambda b, c: (0,) * len(shape))
    return pl.pallas_call(
        _attn_kernel,
        out_shape=(jax.ShapeDtypeStruct(h1.shape, F32),
                   jax.ShapeDtypeStruct((n, D_MODEL), BF16),
                   jax.ShapeDtypeStruct((TOP_K, n), jnp.int32),
                   jax.ShapeDtypeStruct((TOP_K, n), F32),
                   jax.ShapeDtypeStruct((n // tile, N_EXPERTS, 128), jnp.int32)),
        grid=(bsz, nc),
        in_specs=[pl.BlockSpec((t, D_MODEL), lambda b, c: (b * nc + c, 0)),
                  const((1, D_MODEL)), const((D_MODEL, D_MODEL)),
                  pl.BlockSpec((mlen, D_MODEL), lambda b, c: (b, 0)),
                  pl.BlockSpec((mlen, D_MODEL), lambda b, c: (b, 0)),
                  const((D_MODEL, D_MODEL)),
                  const((1, D_MODEL)), const((2 * N_EXPERTS, D_MODEL)), const((N_EXPERTS, 1))],
        out_specs=(pl.BlockSpec((t, D_MODEL), lambda b, c: (b * nc + c, 0)),
                   pl.BlockSpec((t, D_MODEL), lambda b, c: (b * nc + c, 0)),
                   pl.BlockSpec((TOP_K, t), lambda b, c: (0, b * nc + c)),
                   pl.BlockSpec((TOP_K, t), lambda b, c: (0, b * nc + c)),
                   pl.BlockSpec((per, N_EXPERTS, 128), lambda b, c: (b * nc + c, 0, 0))),
        compiler_params=pltpu.CompilerParams(
            dimension_semantics=("arbitrary", "arbitrary"), vmem_limit_bytes=VMEM_LIMIT),
        name="attn_route",
    )(h1, g, wq, k, v, wo, g_ffn, wr_t, br)


def _route_tile(h, rows, tile, g_ref, wr_ref, br_ref, xn_ref, lpos_ref, gate_ref, tcnt_ref):
    t = h.shape[0]
    xn = _rms(h, g_ref[...])
    x_hi = xn.astype(BF16)
    x_lo = (xn - x_hi.astype(F32)).astype(BF16)
    xn_ref[rows, :] = x_hi
    dims = (((1,), (1,)), ((), ()))
    p_hi = lax.dot_general(wr_ref[...], x_hi, dims, preferred_element_type=F32)
    p_lo = lax.dot_general(wr_ref[0:N_EXPERTS, :], x_lo, dims, preferred_element_type=F32)
    lg = p_hi[0:N_EXPERTS, :] + (p_hi[N_EXPERTS:, :] + p_lo) + br_ref[...]
    erow = lax.broadcasted_iota(jnp.int32, (N_EXPERTS, t), 0)
    vals, hots = [], []
    for _ in range(TOP_K):
        m = jnp.max(lg, axis=0, keepdims=True)
        ix = jnp.min(jnp.where(lg == m, erow, N_EXPERTS), axis=0, keepdims=True)
        hot = erow == ix
        lg = jnp.where(hot, -jnp.inf, lg)
        vals.append(m)
        hots.append(hot)
    es = [jnp.exp(v - vals[0]) for v in vals]
    den = es[0] + es[1] + es[2] + es[3]
    sel = jnp.zeros((N_EXPERTS, t), F32)
    for hot in hots:
        sel = sel + jnp.where(hot, 1.0, 0.0)
    tri = jnp.where(lax.broadcasted_iota(jnp.int32, (t, t), 0) < lax.broadcasted_iota(jnp.int32, (t, t), 1),
                    1.0, 0.0).astype(BF16)
    rank = jnp.dot(sel.astype(BF16), tri, preferred_element_type=F32)
    cnt = jnp.sum(sel, axis=1, keepdims=True)
    units = jnp.floor((cnt + (SUBLANES - 1.0)) * (1.0 / SUBLANES))
    below = jnp.where(lax.broadcasted_iota(jnp.int32, (N_EXPERTS, N_EXPERTS), 1)
                      < lax.broadcasted_iota(jnp.int32, (N_EXPERTS, N_EXPERTS), 0), 1.0, 0.0).astype(BF16)
    start = jnp.dot(below, jnp.broadcast_to(units, (N_EXPERTS, 128)).astype(BF16), preferred_element_type=F32)
    pos = rank + SUBLANES * start[:, 0:1]
    for k in range(TOP_K):
        gate_ref[k:k + 1, rows] = es[k] / den
        lpos_ref[k:k + 1, rows] = jnp.sum(jnp.where(hots[k], pos, 0.0), axis=0, keepdims=True).astype(jnp.int32)
    tcnt_ref[tile] = jnp.broadcast_to(cnt, (N_EXPERTS, 128)).astype(jnp.int32)


RUN_CHUNK = 64
WAIT_CHUNK = 256


def _run_copies(i, lo_ref, go_ref, n8_ref, make_copy):
    def expert(e, carry):
        idx = i * N_EXPERTS + e
        lo = lo_ref[idx]
        go = go_ref[idx]
        n8 = n8_ref[idx]
        nbig = lax.shift_right_logical(n8, 3)

        def big(j, c):
            make_copy(pl.multiple_of(lo + j * RUN_CHUNK, SUBLANES), pl.multiple_of(go + j * RUN_CHUNK, SUBLANES),
                      RUN_CHUNK).start()
            return c

        lax.fori_loop(0, nbig, big, 0)
        off = nbig * RUN_CHUNK
        for rows in (32, 16, 8):
            has = (n8 & (rows // SUBLANES)) != 0

            @pl.when(has)
            def _(off=off, rows=rows):
                make_copy(pl.multiple_of(lo + off, SUBLANES), pl.multiple_of(go + off, SUBLANES), rows).start()

            off = off + jnp.where(has, rows, 0)
        return carry

    lax.fori_loop(0, N_EXPERTS, expert, 0)


def _wait_rows(n8, make_wait):
    def big(j, c):
        make_wait(WAIT_CHUNK).wait()
        return c

    lax.fori_loop(0, lax.shift_right_logical(n8, 5), big, 0)
    for rows in (128, 64, 32, 16, 8):
        @pl.when((n8 & (rows // SUBLANES)) != 0)
        def _(rows=rows):
            make_wait(rows).wait()


def _dispatch_kernel(lo_ref, go_ref, n8_ref, nt_ref, zs_ref, zn_ref, xn_ref, lpos_ref, xs_hbm, loc, zbuf, sem,
                     sem_z):
    i = pl.program_id(0)
    last = pl.num_programs(0) - 1
    slot = lax.rem(i, 2)
    nloc, t = loc.shape[1], xn_ref.shape[0]

    def row_copy(sl, lo, go, rows):
        return pltpu.make_async_copy(loc.at[sl, pl.ds(lo, rows), :], xs_hbm.at[pl.ds(go, rows), :], sem.at[sl])

    def wait_copies(sl, count):
        _wait_rows(count, lambda rows: row_copy(sl, 0, 0, rows))

    @pl.when(i >= 2)
    def _():
        wait_copies(slot, nt_ref[jnp.maximum(i - 2, 0)])

    mc = 256
    liota = lax.broadcasted_iota(jnp.int32, (mc, t), 0)
    xn = xn_ref[...]
    for r in range(nloc // mc):
        pick = jnp.zeros((mc, t), F32)
        for k in range(TOP_K):
            pick = jnp.where(liota == lpos_ref[k:k + 1, :] - r * mc, 1.0, pick)
        loc[slot, r * mc:(r + 1) * mc, :] = jnp.dot(pick.astype(BF16), xn, preferred_element_type=F32)
    _run_copies(i, lo_ref, go_ref, n8_ref, lambda lo, go, rows: row_copy(slot, lo, go, rows))

    @pl.when(i == last)
    def _():
        wait_copies(slot, nt_ref[i])

        @pl.when(i >= 1)
        def _():
            wait_copies(1 - slot, nt_ref[jnp.maximum(i - 1, 0)])

        zbuf[...] = jnp.zeros_like(zbuf)

        def zero_copy(go):
            return pltpu.make_async_copy(zbuf.at[pl.ds(0, SUBLANES), :], xs_hbm.at[pl.ds(go, SUBLANES), :], sem_z)

        def expert(e, total):
            def chunk(j, carry):
                zero_copy(pl.multiple_of(zs_ref[e] + j * SUBLANES, SUBLANES)).start()
                return carry
            lax.fori_loop(0, zn_ref[e], chunk, 0)
            return total + zn_ref[e]

        total = lax.fori_loop(0, N_EXPERTS, expert, 0)

        def wait_zero(j, carry):
            zero_copy(0).wait()
            return carry

        lax.fori_loop(0, total, wait_zero, 0)

        def block_copy(go):
            return pltpu.make_async_copy(zbuf, xs_hbm.at[pl.ds(go, TM_MOE), :], sem_z)

        def start_block(j, carry):
            block_copy(pl.multiple_of(zs_ref[N_EXPERTS] + j * TM_MOE, TM_MOE)).start()
            return carry

        def wait_block(j, carry):
            block_copy(0).wait()
            return carry

        lax.fori_loop(0, zn_ref[N_EXPERTS], start_block, 0)
        lax.fori_loop(0, zn_ref[N_EXPERTS], wait_block, 0)


def _dispatch(tabs, xn, lpos_t, n_slots, nloc):
    n = xn.shape[0]
    t = min(T_TILE, n)
    grid_spec = pltpu.PrefetchScalarGridSpec(
        num_scalar_prefetch=6,
        grid=(n // t,),
        in_specs=[pl.BlockSpec((t, D_MODEL), lambda i, *_: (i, 0)),
                  pl.BlockSpec((TOP_K, t), lambda i, *_: (0, i))],
        out_specs=pl.BlockSpec(memory_space=pl.ANY),
        scratch_shapes=[pltpu.VMEM((2, nloc, D_MODEL), F32), pltpu.VMEM((TM_MOE, D_MODEL), F32),
                        pltpu.SemaphoreType.DMA((2,)), pltpu.SemaphoreType.DMA],
    )
    return pl.pallas_call(
        _dispatch_kernel,
        out_shape=jax.ShapeDtypeStruct((n_slots, D_MODEL), F32),
        grid_spec=grid_spec,
        compiler_params=pltpu.CompilerParams(dimension_semantics=("arbitrary",), vmem_limit_bytes=VMEM_LIMIT),
        name="dispatch",
    )(tabs["lo"], tabs["go"], tabs["n8"], tabs["nt"], tabs["zs"], tabs["zn"], xn, lpos_t)


def _moe_kernel(bexp_ref, nused_ref, nexp_ref, nrow_ref, x_ref, wgu_hbm, bgu_ref, wd_hbm, bd_ref, o_ref,
                wgu_f, wd_f, wgu_s, wd_s, sem_w):
    b = pl.program_id(0)
    prev = bexp_ref[jnp.maximum(b - 1, 0)]
    fresh = (b == 0) | (bexp_ref[b] != prev)

    def fetch(ex):
        return (pltpu.make_async_copy(wgu_hbm.at[ex], wgu_f, sem_w.at[0]),
                pltpu.make_async_copy(wd_hbm.at[ex], wd_f, sem_w.at[1]))

    @pl.when(b == 0)
    def _():
        for cp in fetch(bexp_ref[0]):
            cp.start()

    @pl.when(fresh & (b < nused_ref[0]))
    def _():
        for cp in fetch(bexp_ref[b]):
            cp.wait()
        rows = 128
        for r0 in range(0, D_MODEL, rows):
            wgu_s[r0:r0 + rows, :] = wgu_f[r0:r0 + rows, :].astype(BF16)
        for r0 in range(0, D_FF, rows):
            wd_s[r0:r0 + rows, :] = wd_f[r0:r0 + rows, :].astype(BF16)

        @pl.when(nexp_ref[b] >= 0)
        def _():
            for cp in fetch(nexp_ref[b]):
                cp.start()

    def mlp(x):
        gu = jnp.dot(x.astype(BF16), wgu_s[...], preferred_element_type=F32) + bgu_ref[0]
        g = jnp.minimum(gu[:, :D_FF], SWIGLU_LIMIT)
        u = jnp.clip(gu[:, D_FF:], -SWIGLU_LIMIT, SWIGLU_LIMIT)
        h = (u + 1.0) * (g * _sigmoid(SWIGLU_ALPHA * g))
        return jnp.dot(h.astype(BF16), wd_s[...], preferred_element_type=F32) + bd_ref[0]

    used = b < nused_ref[0]
    half = TM_MOE // 2
    few = nrow_ref[b] <= half

    @pl.when(used & jnp.logical_not(few))
    def _():
        o_ref[...] = mlp(x_ref[...])

    @pl.when(used & few)
    def _():
        o_ref[0:half, :] = mlp(x_ref[0:half, :])
        o_ref[half:, :] = jnp.zeros((TM_MOE - half, D_MODEL), F32)

    @pl.when(jnp.logical_not(used))
    def _():
        o_ref[...] = jnp.zeros_like(o_ref)


def _moe(block_exp, n_used, next_exp, block_rows, xs, w_gu, b_gu, w_down, b_down):
    n_slots = xs.shape[0]
    nb = n_slots // TM_MOE
    bspec = lambda shape: pl.BlockSpec(shape, lambda b, be, *_: (be[b], 0, 0))
    grid_spec = pltpu.PrefetchScalarGridSpec(
        num_scalar_prefetch=4,
        grid=(nb,),
        in_specs=[
            pl.BlockSpec((TM_MOE, D_MODEL), lambda b, be, nu, *_: (jnp.minimum(b, nu[0] - 1), 0)),
            pl.BlockSpec(memory_space=pl.ANY), bspec((1, 1, 2 * D_FF)),
            pl.BlockSpec(memory_space=pl.ANY), bspec((1, 1, D_MODEL)),
        ],
        out_specs=pl.BlockSpec((TM_MOE, D_MODEL), lambda b, *_: (b, 0)),
        scratch_shapes=[pltpu.VMEM((D_MODEL, 2 * D_FF), F32), pltpu.VMEM((D_FF, D_MODEL), F32),
                        pltpu.VMEM((D_MODEL, 2 * D_FF), BF16), pltpu.VMEM((D_FF, D_MODEL), BF16),
                        pltpu.SemaphoreType.DMA((2,))],
    )
    return pl.pallas_call(
        _moe_kernel,
        out_shape=jax.ShapeDtypeStruct((n_slots, D_MODEL), F32),
        grid_spec=grid_spec,
        compiler_params=pltpu.CompilerParams(dimension_semantics=("arbitrary",), vmem_limit_bytes=VMEM_LIMIT),
        name="moe",
    )(block_exp, n_used, next_exp, block_rows, xs, w_gu, b_gu, w_down, b_down)


def _combine_kernel(lo_ref, go_ref, n8_ref, nt_ref, ys_hbm, h_ref, lpos_ref, gate_ref, g_ref, o_ref, loc, lpb,
                    gtb, sem):
    i = pl.program_id(0)
    slot = lax.rem(i, 2)
    nloc, t = loc.shape[1], h_ref.shape[0]

    def row_copy(sl, lo, go, rows):
        return pltpu.make_async_copy(ys_hbm.at[pl.ds(go, rows), :], loc.at[sl, pl.ds(lo, rows), :], sem.at[sl])

    def fetch(tile, sl):
        _run_copies(tile, lo_ref, go_ref, n8_ref, lambda lo, go, rows: row_copy(sl, lo, go, rows))

        def zero_rows(j, carry):
            r0 = pl.multiple_of((nt_ref[tile] + j) * SUBLANES, SUBLANES)
            loc[sl, pl.ds(r0, SUBLANES), :] = jnp.zeros((SUBLANES, D_MODEL), F32)
            return carry

        lax.fori_loop(0, nloc // SUBLANES - nt_ref[tile], zero_rows, 0)

    @pl.when(i == 0)
    def _():
        fetch(0, 0)

    @pl.when(i + 1 < pl.num_programs(0))
    def _():
        fetch(i + 1, 1 - slot)

    _wait_rows(nt_ref[i], lambda rows: row_copy(slot, 0, 0, rows))
    kc = 256
    lane = lax.broadcasted_iota(jnp.int32, (t, 128), 1)
    for k in range(TOP_K):
        lpb[k] = jnp.broadcast_to(lpos_ref[:, k:k + 1], (t, 128))
        gtb[k] = jnp.broadcast_to(gate_ref[:, k:k + 1], (t, 128))
    acc = h_ref[...]
    for c in range(nloc // kc):
        pieces = []
        for j in range(kc // 128):
            li = lane + (c * kc + j * 128)
            w = jnp.zeros((t, 128), F32)
            for k in range(TOP_K):
                w = jnp.where(li == lpb[k], gtb[k], w)
            pieces.append(w)
        wc = jnp.concatenate(pieces, axis=1).astype(BF16)
        acc = acc + jnp.dot(wc, loc[slot, c * kc:(c + 1) * kc, :].astype(BF16), preferred_element_type=F32)
    o_ref[...] = _rms(acc, g_ref[...])


def _combine(tabs, ys, h2, lpos_tok, gates_tok, g, nloc):
    n = h2.shape[0]
    t = min(T_TILE, n)
    grid_spec = pltpu.PrefetchScalarGridSpec(
        num_scalar_prefetch=4,
        grid=(n // t,),
        in_specs=[pl.BlockSpec(memory_space=pl.ANY),
                  pl.BlockSpec((t, D_MODEL), lambda i, *_: (i, 0)),
                  pl.BlockSpec((t, TOP_K), lambda i, *_: (i, 0)),
                  pl.BlockSpec((t, TOP_K), lambda i, *_: (i, 0)),
                  pl.BlockSpec((1, D_MODEL), lambda i, *_: (0, 0))],
        out_specs=pl.BlockSpec((t, D_MODEL), lambda i, *_: (i, 0)),
        scratch_shapes=[pltpu.VMEM((2, nloc, D_MODEL), F32), pltpu.VMEM((TOP_K, t, 128), jnp.int32),
                        pltpu.VMEM((TOP_K, t, 128), F32), pltpu.SemaphoreType.DMA((2,))],
    )
    return pl.pallas_call(
        _combine_kernel,
        out_shape=jax.ShapeDtypeStruct((n, D_MODEL), F32),
        grid_spec=grid_spec,
        compiler_params=pltpu.CompilerParams(dimension_semantics=("arbitrary",), vmem_limit_bytes=VMEM_LIMIT),
        name="combine",
    )(tabs["lo"], tabs["go"], tabs["n8"], tabs["nt"], ys, h2, lpos_tok, gates_tok, g)


def _cmul(a, b):
    return a[0] * b[0] - a[1] * b[1], a[0] * b[1] + a[1] * b[0]


def _prep_mixer_params(sub_len, norm_mix_g, w_in, conv_w, conv_b, rg_wa, rg_ba, rg_wx, rg_bx, rg_lambda,
                       s5_lambda_re, s5_lambda_im, s5_b_re, s5_b_im, s5_c_re, s5_c_im, s5_d, s5_log_dt,
                       s5_glu_w, s5_glu_b, w_out):
    row = lambda v: v.reshape(1, -1).astype(F32)
    dt = jnp.exp(s5_log_dt)[:, None]
    mag = jnp.exp(dt * s5_lambda_re)
    abar_re = mag * jnp.cos(dt * s5_lambda_im)
    abar_im = mag * jnp.sin(dt * s5_lambda_im)
    den = s5_lambda_re * s5_lambda_re + s5_lambda_im * s5_lambda_im
    num_re = abar_re - 1.0
    coef_re = (num_re * s5_lambda_re + abar_im * s5_lambda_im) / den
    coef_im = (abar_im * s5_lambda_re - num_re * s5_lambda_im) / den
    bbar_re = coef_re[..., None] * s5_b_re - coef_im[..., None] * s5_b_im
    bbar_im = coef_re[..., None] * s5_b_im + coef_im[..., None] * s5_b_re
    a1 = (abar_re.reshape(-1), abar_im.reshape(-1))
    aq = a1
    for _ in range(int(math.log2(sub_len))):
        aq = _cmul(aq, aq)
    pw = [aq]
    for _ in range(SUBLANES - 1):
        pw.append(_cmul(pw[-1], aq))
    sub = jnp.arange(SUBLANES)[:, None]
    tabs = []
    for d in (1, 2, 4):
        keep = (sub >= d).astype(F32)
        tabs += [keep * pw[d - 1][0][None, :], keep * pw[d - 1][1][None, :]]
    tabs += [jnp.stack([p_[0] for p_ in pw]), jnp.stack([p_[1] for p_ in pw])]
    tabs += [jnp.broadcast_to(a1[0][None, :], (SUBLANES, N_STATE)),
             jnp.broadcast_to(a1[1][None, :], (SUBLANES, N_STATE))]
    s5tab = jnp.stack(tabs).astype(F32)
    gh = S5_GROUPS // 2
    bmat, cmat = [], []
    for hf in range(2):
        gs = slice(hf * gh, (hf + 1) * gh)
        b_re = _block_diag(jnp.transpose(bbar_re[gs], (0, 2, 1)))
        b_im = _block_diag(jnp.transpose(bbar_im[gs], (0, 2, 1)))
        bmat.append(jnp.concatenate([b_re, b_im], axis=1))
        cr = _block_diag(jnp.transpose(s5_c_re[gs], (0, 2, 1)))
        ci = _block_diag(jnp.transpose(s5_c_im[gs], (0, 2, 1)))
        cmat.append(jnp.concatenate([cr, -ci], axis=0))
    return {
        "g_mix": row(norm_mix_g), "w_in": w_in.astype(BF16), "conv_w": conv_w.astype(F32), "conv_b": row(conv_b),
        "w_gates": jnp.concatenate([_block_diag(rg_wa), _block_diag(rg_wx)], axis=1).astype(BF16),
        "b_gates": jnp.concatenate([rg_ba, rg_bx]).reshape(1, -1).astype(F32),
        "sp": row(jax.nn.softplus(-rg_lambda)),
        "bmat": jnp.stack(bmat).astype(BF16), "s5tab": s5tab, "cmat": jnp.stack(cmat).astype(BF16),
        "d_skip": row(s5_d), "glu_w": _block_diag(s5_glu_w).astype(BF16), "glu_b": row(s5_glu_b),
        "w_out": w_out.astype(BF16),
    }


def _slot_tables(tcnt, n_slots):
    i32 = lambda v: v.astype(jnp.int32)
    run = ((tcnt + SUBLANES - 1) // SUBLANES) * SUBLANES
    rows = jnp.sum(run, axis=0)
    blocks = (rows + TM_MOE - 1) // TM_MOE
    bend = jnp.cumsum(blocks)
    estart = (bend - blocks) * TM_MOE
    nb = n_slots // TM_MOE
    n_used = i32(bend[-1])
    eidx = jnp.arange(N_EXPERTS)
    bidx = jnp.arange(nb)[:, None]
    owner = (bidx >= (bend - blocks)[None, :]) & (bidx < bend[None, :])
    pick = lambda per_expert: jnp.sum(jnp.where(owner, per_expert[None, :], 0), axis=1)
    last_exp = jnp.max(jnp.where(blocks > 0, eidx, 0))
    block_exp = jnp.where(jnp.arange(nb) < n_used, pick(eidx), last_exp)
    later = (eidx[None, :] > eidx[:, None]) & (blocks[None, :] > 0)
    next_of = jnp.min(jnp.where(later, eidx[None, :], N_EXPERTS), axis=1)
    next_exp = jnp.where(jnp.arange(nb) < n_used, pick(jnp.where(next_of < N_EXPERTS, next_of, -1)), -1)
    block_rows = jnp.clip(pick(rows + (bend - blocks) * TM_MOE) - jnp.arange(nb) * TM_MOE, 0, TM_MOE)
    tabs = {
        "lo": i32(jnp.cumsum(run, axis=1) - run).reshape(-1),
        "go": i32(estart[None, :] + jnp.cumsum(run, axis=0) - run).reshape(-1),
        "n8": i32(run // SUBLANES).reshape(-1),
        "nt": i32(jnp.sum(run, axis=1) // SUBLANES),
        "zs": i32(jnp.concatenate([estart + rows, bend[-1:] * TM_MOE])),
        "zn": i32(jnp.concatenate([(blocks * TM_MOE - rows) // SUBLANES, nb - bend[-1:]])),
    }
    return i32(block_exp), n_used.reshape(1), i32(next_exp), i32(block_rows), tabs


def kernel(x, mem, norm_mix_g, w_in, conv_w, conv_b, rg_wa, rg_ba, rg_wx, rg_bx, rg_lambda, s5_lambda_re,
           s5_lambda_im, s5_b_re, s5_b_im, s5_c_re, s5_c_im, s5_d, s5_log_dt, s5_glu_w, s5_glu_b, w_out,
           norm_xa_g, mem_norm_g, xa_wq, xa_wkv, xa_wo, norm_ffn_g, router_w, router_b, exp_w_gu, exp_b_gu,
           exp_w_down, exp_b_down, final_norm_g):
    bsz, seq, d = x.shape
    mlen = mem.shape[1]
    n = bsz * seq
    row = lambda v: v.reshape(1, -1).astype(F32)

    p = _prep_mixer_params(min(T_MIX, seq) // SUBLANES, norm_mix_g, w_in, conv_w, conv_b, rg_wa, rg_ba, rg_wx,
                           rg_bx, rg_lambda, s5_lambda_re, s5_lambda_im, s5_b_re, s5_b_im, s5_c_re, s5_c_im,
                           s5_d, s5_log_dt, s5_glu_w, s5_glu_b, w_out)
    h1 = _mixer(x.reshape(n, d), bsz, seq, p)

    k, v = _kv(mem.reshape(bsz * mlen, d), bsz, mlen, row(mem_norm_g), xa_wkv.astype(BF16))
    wr_t = router_w.T.astype(F32)
    wr_hi = wr_t.astype(BF16)
    wr_lo = (wr_t - wr_hi.astype(F32)).astype(BF16)
    h2, xn, lpos_t, gate_t, tcnt = _attn_route(
        h1, bsz, seq, mlen, row(norm_xa_g), xa_wq.astype(BF16), k, v, xa_wo.astype(BF16),
        row(norm_ffn_g), jnp.concatenate([wr_hi, wr_lo], axis=0), router_b.reshape(-1, 1).astype(F32))
    tile = min(T_TILE, min(T_ATT, seq))
    n_tiles = n // tile
    run_pad = N_EXPERTS * (SUBLANES - 1)
    nloc = -(-(TOP_K * tile + run_pad) // 128) * 128
    n_slots = -(-(TOP_K * n + n_tiles * run_pad + N_EXPERTS * (TM_MOE - SUBLANES)) // TM_MOE) * TM_MOE
    block_exp, n_used, next_exp, block_rows, tabs = _slot_tables(tcnt[:, :, 0], n_slots)
    xs = _dispatch(tabs, xn, lpos_t, n_slots, nloc)
    ys = _moe(block_exp, n_used, next_exp, block_rows, xs, exp_w_gu, exp_b_gu.reshape(N_EXPERTS, 1, -1),
              exp_w_down, exp_b_down.reshape(N_EXPERTS, 1, -1))
    out = _combine(tabs, ys, h2, lpos_t.T, gate_t.T, row(final_norm_g), nloc)
    return out.reshape(bsz, seq, d)
```

```python
import math

import jax
import jax.numpy as jnp
from jax import lax
from jax.experimental import pallas as pl
from jax.experimental.pallas import tpu as pltpu

F32 = jnp.float32
BF16 = jnp.bfloat16

D_MODEL = 1024
D_RG = 512
D_S5 = 512
CONV_W = 4
RG_C = 8.0
S5_GROUPS = 32
S5_STATE = 64
N_STATE = S5_GROUPS * S5_STATE
XA_HEADS = 4
XA_HD = 256
N_EXPERTS = 32
TOP_K = 4
D_FF = 1024
SWIGLU_LIMIT = 7.0
SWIGLU_ALPHA = 1.702
NORM_EPS = 1e-6

SUBLANES = 8
T_MIX = 512
T_ATT = 1024
T_TILE = 512
TM_MOE = 1024
MOE_PARTS = 4
VMEM_LIMIT = 56 * 1024 * 1024


def _rms(x, g):
    return x * lax.rsqrt(jnp.mean(x * x, axis=-1, keepdims=True) + NORM_EPS) * g


def _gelu_tanh(x):
    return 0.5 * x * (1.0 + jnp.tanh(math.sqrt(2.0 / math.pi) * (x + 0.044715 * (x * x * x))))


def _sigmoid(x):
    return 1.0 / (1.0 + jnp.exp(-x))


def _block_diag(blocks):
    n, r, c = blocks.shape
    eye = jnp.eye(n, dtype=blocks.dtype)
    return jnp.einsum('nrc,nm->nrmc', blocks, eye).reshape(n * r, n * c)


def _sublane_chain(al, e, carry, sub):
    for d in (1, 2, 4):
        al_sh = jnp.where(sub >= d, pltpu.roll(al, d, axis=0), 1.0)
        e_sh = jnp.where(sub >= d, pltpu.roll(e, d, axis=0), 0.0)
        e = al * e_sh + e
        al = al * al_sh
    s_end = e + al * carry
    start = jnp.where(sub == 0, carry, pltpu.roll(s_end, 1, axis=0))
    return start, s_end[SUBLANES - 1:SUBLANES, :]


N_ROWS = 2


def _mixer_kernel(x_ref, g_ref, win_ref, convw_ref, convb_ref, wg_ref, bg_ref, sp_ref,
                   bmat_ref, tab_ref, cmat_ref, dskip_ref, glu_ref, glub_ref, wout_ref,
                   o_ref,
                   xin, xout, sem_i, sem_o, xbuf, hist, a_s, b_s, rg_carry, sre, sim, c_re, c_im):
    ngrp = xin.shape[2]
    t = ngrp * SUBLANES
    nh = (CONV_W - 1) * SUBLANES
    nc = pl.num_programs(1)
    c = pl.program_id(1)
    step = pl.program_id(0) * nc + c
    nsteps = pl.num_programs(0) * nc
    slot = lax.rem(step, 2)

    def row_base(s, r):
        return ((N_ROWS * lax.div(s, nc) + r) * nc + lax.rem(s, nc)) * t

    def in_copy(s, sl, r, i):
        return pltpu.make_async_copy(x_ref.at[pl.ds(row_base(s, r) + i * ngrp, ngrp), :], xin.at[r, sl, :, i, :],
                                     sem_i.at[r, sl])

    def out_copy(s, sl, r, i):
        return pltpu.make_async_copy(xout.at[r, sl, :, i, :], o_ref.at[pl.ds(row_base(s, r) + i * ngrp, ngrp), :],
                                     sem_o.at[r, sl])

    @pl.when(step == 0)
    def _():
        for r in range(N_ROWS):
            for i in range(SUBLANES):
                in_copy(0, 0, r, i).start()

    @pl.when(c == 0)
    def _():
        hist[...] = jnp.zeros_like(hist)
        rg_carry[...] = jnp.zeros_like(rg_carry)
        c_re[...] = jnp.zeros_like(c_re)
        c_im[...] = jnp.zeros_like(c_im)

    for r in range(N_ROWS):
        for i in range(SUBLANES):
            in_copy(step, slot, r, i).wait()

    @pl.when(step + 1 < nsteps)
    def _():
        for r in range(N_ROWS):
            for i in range(SUBLANES):
                in_copy(step + 1, 1 - slot, r, i).start()

    @pl.when(step >= 2)
    def _():
        for r in range(N_ROWS):
            for i in range(SUBLANES):
                out_copy(step - 2, slot, r, i).wait()

    sub = lax.broadcasted_iota(jnp.int32, (SUBLANES, D_RG), 0)
    half = N_STATE // 2
    lb = 512
    subl = lax.broadcasted_iota(jnp.int32, (SUBLANES, lb), 0)
    grp = lambda q: slice(q * SUBLANES, (q + 1) * SUBLANES)

    def front(r):
        x = xin[r, slot].reshape(t, D_MODEL)
        xn = _rms(x, g_ref[...]).astype(BF16)
        proj = jnp.dot(xn, win_ref[...], preferred_element_type=F32)
        gate_rg = proj[:, :D_RG]
        x_rg = proj[:, D_RG:2 * D_RG]
        u = proj[:, 2 * D_RG:]
        cur_tail = x_rg[t - nh:, :]
        for gidx in range(CONV_W - 1):
            rs = slice(gidx * SUBLANES, (gidx + 1) * SUBLANES)
            xbuf[r, rs, :] = jnp.where(sub == 0, pltpu.roll(hist[r, rs, :], 1, axis=0),
                                       pltpu.roll(cur_tail[rs, :], 1, axis=0))
        hist[r] = cur_tail
        xbuf[r, nh:, :] = x_rg
        xc = convb_ref[...] + convw_ref[CONV_W - 1:CONV_W, :] * x_rg
        for k in range(CONV_W - 1):
            xc = xc + convw_ref[k:k + 1, :] * xbuf[r, k * SUBLANES:k * SUBLANES + t, :]
        gates = jnp.dot(xc.astype(BF16), wg_ref[...], preferred_element_type=F32) + bg_ref[...]
        rgate = _sigmoid(gates[:, :D_RG])
        igate = _sigmoid(gates[:, D_RG:])
        a = jnp.exp((-RG_C) * rgate * sp_ref[...])
        mult = jnp.sqrt(1.0 - a * a)
        row = lax.broadcasted_iota(jnp.int32, (t, D_RG), 0)
        mult = jnp.where((row == 0) & (c == 0), 1.0, mult)
        a_s[r] = a
        b_s[r] = mult * (igate * xc)
        ub = u.astype(BF16)
        for hf in range(2):
            bu = jnp.dot(ub[:, hf * 256:(hf + 1) * 256], bmat_ref[hf], preferred_element_type=F32)
            sre[r, :, hf * half:(hf + 1) * half] = bu[:, :half]
            sim[r, :, hf * half:(hf + 1) * half] = bu[:, half:]
        return x, gate_rg, u

    def scans(r):
        h = jnp.zeros((SUBLANES, D_RG), F32)
        ac = jnp.ones((SUBLANES, D_RG), F32)
        for q in range(ngrp):
            aa = a_s[r, grp(q), :]
            h = aa * h + b_s[r, grp(q), :]
            ac = aa * ac
        h, last = _sublane_chain(ac, h, rg_carry[r], sub)
        rg_carry[r] = last
        for q in range(ngrp):
            h = a_s[r, grp(q), :] * h + b_s[r, grp(q), :]
            b_s[r, grp(q), :] = h
        for blk in range(N_STATE // lb):
            cs = slice(blk * lb, (blk + 1) * lb)
            ar = tab_ref[8, :, cs]
            ai = tab_ref[9, :, cs]
            er = jnp.zeros((SUBLANES, lb), F32)
            ei = jnp.zeros((SUBLANES, lb), F32)
            for q in range(ngrp):
                er, ei = (ar * er - ai * ei + sre[r, grp(q), cs], ar * ei + ai * er + sim[r, grp(q), cs])
            for i, d in enumerate((1, 2, 4)):
                mr = tab_ref[2 * i, :, cs]
                mi = tab_ref[2 * i + 1, :, cs]
                rr = pltpu.roll(er, d, axis=0)
                ri = pltpu.roll(ei, d, axis=0)
                er, ei = er + (mr * rr - mi * ri), ei + (mr * ri + mi * rr)
            pr = tab_ref[6, :, cs]
            pi_ = tab_ref[7, :, cs]
            cr = c_re[r, :, cs]
            ci = c_im[r, :, cs]
            sr = er + (pr * cr - pi_ * ci)
            si = ei + (pr * ci + pi_ * cr)
            c_re[r, :, cs] = sr[SUBLANES - 1:SUBLANES, :]
            c_im[r, :, cs] = si[SUBLANES - 1:SUBLANES, :]
            hr = jnp.where(subl == 0, cr, pltpu.roll(sr, 1, axis=0))
            hi = jnp.where(subl == 0, ci, pltpu.roll(si, 1, axis=0))
            for q in range(ngrp):
                hr, hi = (ar * hr - ai * hi + sre[r, grp(q), cs], ar * hi + ai * hr + sim[r, grp(q), cs])
                sre[r, grp(q), cs] = hr
                sim[r, grp(q), cs] = hi

    def back(r, x, gate_rg, u):
        y_rg = b_s[r] * _gelu_tanh(gate_rg)
        ys = []
        for hf in range(2):
            hcat = jnp.concatenate([sre[r, :, hf * half:(hf + 1) * half], sim[r, :, hf * half:(hf + 1) * half]],
                                   axis=1)
            ys.append(jnp.dot(hcat.astype(BF16), cmat_ref[hf], preferred_element_type=F32))
        y = jnp.concatenate(ys, axis=1) + dskip_ref[...] * u
        z = _gelu_tanh(y)
        zg = _sigmoid(jnp.dot(z.astype(BF16), glu_ref[...], preferred_element_type=F32) + glub_ref[...])
        ycat = jnp.concatenate([y_rg, z * zg], axis=1).astype(BF16)
        res = x + jnp.dot(ycat, wout_ref[...], preferred_element_type=F32)
        xout[r, slot] = res.reshape(ngrp, SUBLANES, D_MODEL)
        for i in range(SUBLANES):
            out_copy(step, slot, r, i).start()

    kept = [front(r) for r in range(N_ROWS)]
    for r in range(N_ROWS):
        scans(r)
    for r in range(N_ROWS):
        back(r, *kept[r])

    @pl.when(step == nsteps - 1)
    def _():
        for r in range(N_ROWS):
            for i in range(SUBLANES):
                out_copy(step, slot, r, i).wait()

        @pl.when(step >= 1)
        def _():
            for r in range(N_ROWS):
                for i in range(SUBLANES):
                    out_copy(step - 1, 1 - slot, r, i).wait()


def _mixer(x2, bsz, seq, p):
    t = min(T_MIX, seq)
    nc = seq // t
    nh = (CONV_W - 1) * SUBLANES
    assert bsz % N_ROWS == 0
    const = lambda shape: pl.BlockSpec(shape, lambda b, c: (0,) * len(shape), pipeline_mode=pl.Buffered(1))
    per_row = lambda *shape: pltpu.VMEM((N_ROWS,) + shape, F32)
    return pl.pallas_call(
        _mixer_kernel,
        out_shape=jax.ShapeDtypeStruct(x2.shape, F32),
        grid=(bsz // N_ROWS, nc),
        in_specs=[
            pl.BlockSpec(memory_space=pl.ANY),
            const((1, D_MODEL)), const((D_MODEL, 3 * D_RG)), const((CONV_W, D_RG)), const((1, D_RG)),
            const((D_RG, 2 * D_RG)), const((1, 2 * D_RG)), const((1, D_RG)),
            const((2, 256, N_STATE)), const((10, SUBLANES, N_STATE)), const((2, N_STATE, 256)),
            const((1, D_S5)), const((D_S5, D_S5)), const((1, D_S5)), const((D_MODEL, D_MODEL)),
        ],
        out_specs=pl.BlockSpec(memory_space=pl.ANY),
        scratch_shapes=[
            per_row(2, t // SUBLANES, SUBLANES, D_MODEL), per_row(2, t // SUBLANES, SUBLANES, D_MODEL),
            pltpu.SemaphoreType.DMA((N_ROWS, 2)), pltpu.SemaphoreType.DMA((N_ROWS, 2)),
            per_row(t + nh, D_RG), per_row(nh, D_RG),
            per_row(t, D_RG), per_row(t, D_RG), per_row(1, D_RG),
            per_row(t, N_STATE), per_row(t, N_STATE),
            per_row(1, N_STATE), per_row(1, N_STATE),
        ],
        compiler_params=pltpu.CompilerParams(
            dimension_semantics=("arbitrary", "arbitrary"), vmem_limit_bytes=VMEM_LIMIT),
        name="mixer",
    )(x2, p["g_mix"], p["w_in"], p["conv_w"], p["conv_b"], p["w_gates"], p["b_gates"], p["sp"],
      p["bmat"], p["s5tab"], p["cmat"], p["d_skip"], p["glu_w"], p["glu_b"], p["w_out"])


def _kv_kernel(m_ref, g_ref, w_ref, k_ref, v_ref):
    mn = _rms(m_ref[...], g_ref[...]).astype(BF16)
    kv = jnp.dot(mn, w_ref[...], preferred_element_type=F32)
    k_ref[...] = kv[:, :D_MODEL].astype(BF16)
    v_ref[...] = kv[:, D_MODEL:].astype(BF16)


def _kv(mem2, bsz, mlen, g, wkv):
    return pl.pallas_call(
        _kv_kernel,
        out_shape=(jax.ShapeDtypeStruct(mem2.shape, BF16), jax.ShapeDtypeStruct(mem2.shape, BF16)),
        grid=(bsz,),
        in_specs=[pl.BlockSpec((mlen, D_MODEL), lambda b: (b, 0)),
                  pl.BlockSpec((1, D_MODEL), lambda b: (0, 0)),
                  pl.BlockSpec((D_MODEL, 2 * D_MODEL), lambda b: (0, 0))],
        out_specs=(pl.BlockSpec((mlen, D_MODEL), lambda b: (b, 0)),
                   pl.BlockSpec((mlen, D_MODEL), lambda b: (b, 0))),
        compiler_params=pltpu.CompilerParams(dimension_semantics=("arbitrary",), vmem_limit_bytes=VMEM_LIMIT),
        name="kv",
    )(mem2, g, wkv)


def _attn_kernel(h_ref, g_ref, wq_ref, k_ref, v_ref, wo_ref, gf_ref, wr_ref, br_ref,
                 o_ref, xn_ref, lpos_ref, gate_ref, tcnt_ref):
    h = h_ref[...]
    xn = _rms(h, g_ref[...]).astype(BF16)
    q = jnp.dot(xn, wq_ref[...], preferred_element_type=F32).astype(BF16)
    outs = []
    for hd in range(XA_HEADS):
        cs = slice(hd * XA_HD, (hd + 1) * XA_HD)
        s = lax.dot_general(q[:, cs], k_ref[:, cs], (((1,), (1,)), ((), ())),
                            preferred_element_type=F32) * (XA_HD ** -0.5)
        s = s - jnp.max(s, axis=-1, keepdims=True)
        e = jnp.exp(s)
        pr = e / jnp.sum(e, axis=-1, keepdims=True)
        outs.append(jnp.dot(pr.astype(BF16), v_ref[:, cs], preferred_element_type=F32))
    o = jnp.concatenate(outs, axis=1).astype(BF16)
    h2 = h + jnp.dot(o, wo_ref[...], preferred_element_type=F32)
    o_ref[...] = h2
    tile = h2.shape[0] // tcnt_ref.shape[0]
    for j in range(tcnt_ref.shape[0]):
        rows = slice(j * tile, (j + 1) * tile)
        _route_tile(h2[rows, :], rows, j, gf_ref, wr_ref, br_ref, xn_ref, lpos_ref, gate_ref, tcnt_ref)


def _attn_route(h1, bsz, seq, mlen, g, wq, k, v, wo, g_ffn, wr_t, br):
    n = h1.shape[0]
    t = min(T_ATT, seq)
    nc = seq // t
    tile = min(T_TILE, t)
    per = t // tile
    const = lambda shape: pl.BlockSpec(shape, lambda b, c: (0,) * len(shape))
    return pl.pallas_call(
        _attn_kernel,
        out_shape=(jax.ShapeDtypeStruct(h1.shape, F32),
                   jax.ShapeDtypeStruct((n, D_MODEL), BF16),
                   jax.ShapeDtypeStruct((TOP_K, n), jnp.int32),
                   jax.ShapeDtypeStruct((TOP_K, n), F32),
                   jax.ShapeDtypeStruct((n // tile, N_EXPERTS, 128), jnp.int32)),
        grid=(bsz, nc),
        in_specs=[pl.BlockSpec((t, D_MODEL), lambda b, c: (b * nc + c, 0)),
                  const((1, D_MODEL)), const((D_MODEL, D_MODEL)),
                  pl.BlockSpec((mlen, D_MODEL), lambda b, c: (b, 0)),
                  pl.BlockSpec((mlen, D_MODEL), lambda b, c: (b, 0)),
                  const((D_MODEL, D_MODEL)),
                  const((1, D_MODEL)), const((2 * N_EXPERTS, D_MODEL)), const((N_EXPERTS, 1))],
        out_specs=(pl.BlockSpec((t, D_MODEL), lambda b, c: (b * nc + c, 0)),
                   pl.BlockSpec((t, D_MODEL), lambda b, c: (b * nc + c, 0)),
                   pl.BlockSpec((TOP_K, t), lambda b, c: (0, b * nc + c)),
                   pl.BlockSpec((TOP_K, t), lambda b, c: (0, b * nc + c)),
                   pl.BlockSpec((per, N_EXPERTS, 128), lambda b, c: (b * nc + c, 0, 0))),
        compiler_params=pltpu.CompilerParams(
            dimension_semantics=("arbitrary", "arbitrary"), vmem_limit_bytes=VMEM_LIMIT),
        name="attn_route",
    )(h1, g, wq, k, v, wo, g_ffn, wr_t, br)


def _route_tile(h, rows, tile, g_ref, wr_ref, br_ref, xn_ref, lpos_ref, gate_ref, tcnt_ref):
    t = h.shape[0]
    xn = _rms(h, g_ref[...])
    x_hi = xn.astype(BF16)
    x_lo = (xn - x_hi.astype(F32)).astype(BF16)
    xn_ref[rows, :] = x_hi
    dims = (((1,), (1,)), ((), ()))
    p_hi = lax.dot_general(wr_ref[...], x_hi, dims, preferred_element_type=F32)
    p_lo = lax.dot_general(wr_ref[0:N_EXPERTS, :], x_lo, dims, preferred_element_type=F32)
    lg = p_hi[0:N_EXPERTS, :] + (p_hi[N_EXPERTS:, :] + p_lo) + br_ref[...]
    erow = lax.broadcasted_iota(jnp.int32, (N_EXPERTS, t), 0)
    vals, hots = [], []
    for _ in range(TOP_K):
        m = jnp.max(lg, axis=0, keepdims=True)
        ix = jnp.min(jnp.where(lg == m, erow, N_EXPERTS), axis=0, keepdims=True)
        hot = erow == ix
        lg = jnp.where(hot, -jnp.inf, lg)
        vals.append(m)
        hots.append(hot)
    es = [jnp.exp(v - vals[0]) for v in vals]
    den = es[0] + es[1] + es[2] + es[3]
    sel = jnp.zeros((N_EXPERTS, t), F32)
    for hot in hots:
        sel = sel + jnp.where(hot, 1.0, 0.0)
    tri = jnp.where(lax.broadcasted_iota(jnp.int32, (t, t), 0) < lax.broadcasted_iota(jnp.int32, (t, t), 1),
                    1.0, 0.0).astype(BF16)
    rank = jnp.dot(sel.astype(BF16), tri, preferred_element_type=F32)
    cnt = jnp.sum(sel, axis=1, keepdims=True)
    units = jnp.floor((cnt + (SUBLANES - 1.0)) * (1.0 / SUBLANES))
    below = jnp.where(lax.broadcasted_iota(jnp.int32, (N_EXPERTS, N_EXPERTS), 1)
                      < lax.broadcasted_iota(jnp.int32, (N_EXPERTS, N_EXPERTS), 0), 1.0, 0.0).astype(BF16)
    start = jnp.dot(below, jnp.broadcast_to(units, (N_EXPERTS, 128)).astype(BF16), preferred_element_type=F32)
    pos = rank + SUBLANES * start[:, 0:1]
    for k in range(TOP_K):
        gate_ref[k:k + 1, rows] = es[k] / den
        lpos_ref[k:k + 1, rows] = jnp.sum(jnp.where(hots[k], pos, 0.0), axis=0, keepdims=True).astype(jnp.int32)
    tcnt_ref[tile] = jnp.broadcast_to(cnt, (N_EXPERTS, 128)).astype(jnp.int32)


RUN_CHUNK = 64
WAIT_CHUNK = 256


def _run_copies(i, lo_ref, go_ref, n8_ref, make_copy):
    def expert(e, carry):
        idx = i * N_EXPERTS + e
        lo = lo_ref[idx]
        go = go_ref[idx]
        n8 = n8_ref[idx]
        nbig = lax.shift_right_logical(n8, 3)

        def big(j, c):
            make_copy(pl.multiple_of(lo + j * RUN_CHUNK, SUBLANES), pl.multiple_of(go + j * RUN_CHUNK, SUBLANES),
                      RUN_CHUNK).start()
            return c

        lax.fori_loop(0, nbig, big, 0)
        off = nbig * RUN_CHUNK
        for rows in (32, 16, 8):
            has = (n8 & (rows // SUBLANES)) != 0

            @pl.when(has)
            def _(off=off, rows=rows):
                make_copy(pl.multiple_of(lo + off, SUBLANES), pl.multiple_of(go + off, SUBLANES), rows).start()

            off = off + jnp.where(has, rows, 0)
        return carry

    lax.fori_loop(0, N_EXPERTS, expert, 0)


def _wait_rows(n8, make_wait):
    def big(j, c):
        make_wait(WAIT_CHUNK).wait()
        return c

    lax.fori_loop(0, lax.shift_right_logical(n8, 5), big, 0)
    for rows in (128, 64, 32, 16, 8):
        @pl.when((n8 & (rows // SUBLANES)) != 0)
        def _(rows=rows):
            make_wait(rows).wait()


def _dispatch_kernel(lo_ref, go_ref, n8_ref, nt_ref, zs_ref, zn_ref, xn_ref, lpos_ref, xs_hbm, loc, zbuf, sem,
                     sem_z):
    i = pl.program_id(0)
    last = pl.num_programs(0) - 1
    slot = lax.rem(i, 2)
    nloc, t = loc.shape[1], xn_ref.shape[0]

    def row_copy(sl, lo, go, rows):
        return pltpu.make_async_copy(loc.at[sl, pl.ds(lo, rows), :], xs_hbm.at[pl.ds(go, rows), :], sem.at[sl])

    def wait_copies(sl, count):
        _wait_rows(count, lambda rows: row_copy(sl, 0, 0, rows))

    @pl.when(i >= 2)
    def _():
        wait_copies(slot, nt_ref[jnp.maximum(i - 2, 0)])

    mc = 256
    liota = lax.broadcasted_iota(jnp.int32, (mc, t), 0)
    xn = xn_ref[...]
    for r in range(nloc // mc):
        pick = jnp.zeros((mc, t), F32)
        for k in range(TOP_K):
            pick = jnp.where(liota == lpos_ref[k:k + 1, :] - r * mc, 1.0, pick)
        loc[slot, r * mc:(r + 1) * mc, :] = jnp.dot(pick.astype(BF16), xn, preferred_element_type=F32)
    _run_copies(i, lo_ref, go_ref, n8_ref, lambda lo, go, rows: row_copy(slot, lo, go, rows))

    @pl.when(i == last)
    def _():
        wait_copies(slot, nt_ref[i])

        @pl.when(i >= 1)
        def _():
            wait_copies(1 - slot, nt_ref[jnp.maximum(i - 1, 0)])

        zbuf[...] = jnp.zeros_like(zbuf)

        def zero_copy(go):
            return pltpu.make_async_copy(zbuf.at[pl.ds(0, SUBLANES), :], xs_hbm.at[pl.ds(go, SUBLANES), :], sem_z)

        def expert(e, total):
            def chunk(j, carry):
                zero_copy(pl.multiple_of(zs_ref[e] + j * SUBLANES, SUBLANES)).start()
                return carry
            lax.fori_loop(0, zn_ref[e], chunk, 0)
            return total + zn_ref[e]

        total = lax.fori_loop(0, N_EXPERTS, expert, 0)

        def wait_zero(j, carry):
            zero_copy(0).wait()
            return carry

        lax.fori_loop(0, total, wait_zero, 0)

        def block_copy(go):
            return pltpu.make_async_copy(zbuf, xs_hbm.at[pl.ds(go, TM_MOE), :], sem_z)

        def start_block(j, carry):
            block_copy(pl.multiple_of(zs_ref[N_EXPERTS] + j * TM_MOE, TM_MOE)).start()
            return carry

        def wait_block(j, carry):
            block_copy(0).wait()
            return carry

        lax.fori_loop(0, zn_ref[N_EXPERTS], start_block, 0)
        lax.fori_loop(0, zn_ref[N_EXPERTS], wait_block, 0)


def _dispatch(tabs, xn, lpos_t, n_slots, nloc):
    n = xn.shape[0]
    t = min(T_TILE, n)
    grid_spec = pltpu.PrefetchScalarGridSpec(
        num_scalar_prefetch=6,
        grid=(n // t,),
        in_specs=[pl.BlockSpec((t, D_MODEL), lambda i, *_: (i, 0)),
                  pl.BlockSpec((TOP_K, t), lambda i, *_: (0, i))],
        out_specs=pl.BlockSpec(memory_space=pl.ANY),
        scratch_shapes=[pltpu.VMEM((2, nloc, D_MODEL), F32), pltpu.VMEM((TM_MOE, D_MODEL), F32),
                        pltpu.SemaphoreType.DMA((2,)), pltpu.SemaphoreType.DMA],
    )
    return pl.pallas_call(
        _dispatch_kernel,
        out_shape=jax.ShapeDtypeStruct((n_slots, D_MODEL), F32),
        grid_spec=grid_spec,
        compiler_params=pltpu.CompilerParams(dimension_semantics=("arbitrary",), vmem_limit_bytes=VMEM_LIMIT),
        name="dispatch",
    )(tabs["lo"], tabs["go"], tabs["n8"], tabs["nt"], tabs["zs"], tabs["zn"], xn, lpos_t)


def _moe_kernel(bexp_ref, nused_ref, nexp_ref, nrow_ref, x_ref, wgu_hbm, bgu_ref, wd_hbm, bd_ref, o_ref,
                wgu_f, wd_f, wgu_s, wd_s, sem_w):
    b = pl.program_id(0)
    prev = bexp_ref[jnp.maximum(b - 1, 0)]
    fresh = (b == 0) | (bexp_ref[b] != prev)

    def fetch(ex):
        return (pltpu.make_async_copy(wgu_hbm.at[ex], wgu_f, sem_w.at[0]),
                pltpu.make_async_copy(wd_hbm.at[ex], wd_f, sem_w.at[1]))

    @pl.when(b == 0)
    def _():
        for cp in fetch(bexp_ref[0]):
            cp.start()

    @pl.when(fresh & (b < nused_ref[0]))
    def _():
        for cp in fetch(bexp_ref[b]):
            cp.wait()
        rows = 128
        for r0 in range(0, D_MODEL, rows):
            wgu_s[r0:r0 + rows, :] = wgu_f[r0:r0 + rows, :].astype(BF16)
        for r0 in range(0, D_FF, rows):
            wd_s[r0:r0 + rows, :] = wd_f[r0:r0 + rows, :].astype(BF16)

        @pl.when(nexp_ref[b] >= 0)
        def _():
            for cp in fetch(nexp_ref[b]):
                cp.start()

    def mlp(x):
        gu = jnp.dot(x.astype(BF16), wgu_s[...], preferred_element_type=F32) + bgu_ref[0]
        g = jnp.minimum(gu[:, :D_FF], SWIGLU_LIMIT)
        u = jnp.clip(gu[:, D_FF:], -SWIGLU_LIMIT, SWIGLU_LIMIT)
        h = (u + 1.0) * (g * _sigmoid(SWIGLU_ALPHA * g))
        return jnp.dot(h.astype(BF16), wd_s[...], preferred_element_type=F32) + bd_ref[0]

    used = b < nused_ref[0]
    part = TM_MOE // MOE_PARTS
    nrow = nrow_ref[b]
    for k in range(1, MOE_PARTS + 1):
        lo_ok = nrow > (k - 1) * part
        hi_ok = (nrow <= k * part) if k < MOE_PARTS else True

        @pl.when(used & lo_ok & hi_ok)
        def _(k=k):
            o_ref[0:k * part, :] = mlp(x_ref[0:k * part, :])
            if k < MOE_PARTS:
                o_ref[k * part:, :] = jnp.zeros((TM_MOE - k * part, D_MODEL), F32)

    @pl.when(jnp.logical_not(used))
    def _():
        o_ref[...] = jnp.zeros_like(o_ref)


def _moe(block_exp, n_used, next_exp, block_rows, xs, w_gu, b_gu, w_down, b_down):
    n_slots = xs.shape[0]
    nb = n_slots // TM_MOE
    bspec = lambda shape: pl.BlockSpec(shape, lambda b, be, *_: (be[b], 0, 0))
    grid_spec = pltpu.PrefetchScalarGridSpec(
        num_scalar_prefetch=4,
        grid=(nb,),
        in_specs=[
            pl.BlockSpec((TM_MOE, D_MODEL), lambda b, be, nu, *_: (jnp.minimum(b, nu[0] - 1), 0)),
            pl.BlockSpec(memory_space=pl.ANY), bspec((1, 1, 2 * D_FF)),
            pl.BlockSpec(memory_space=pl.ANY), bspec((1, 1, D_MODEL)),
        ],
        out_specs=pl.BlockSpec((TM_MOE, D_MODEL), lambda b, *_: (b, 0)),
        scratch_shapes=[pltpu.VMEM((D_MODEL, 2 * D_FF), F32), pltpu.VMEM((D_FF, D_MODEL), F32),
                        pltpu.VMEM((D_MODEL, 2 * D_FF), BF16), pltpu.VMEM((D_FF, D_MODEL), BF16),
                        pltpu.SemaphoreType.DMA((2,))],
    )
    return pl.pallas_call(
        _moe_kernel,
        out_shape=jax.ShapeDtypeStruct((n_slots, D_MODEL), F32),
        grid_spec=grid_spec,
        compiler_params=pltpu.CompilerParams(dimension_semantics=("arbitrary",), vmem_limit_bytes=VMEM_LIMIT),
        name="moe",
    )(block_exp, n_used, next_exp, block_rows, xs, w_gu, b_gu, w_down, b_down)


def _combine_kernel(lo_ref, go_ref, n8_ref, nt_ref, ys_hbm, h_ref, lpos_ref, gate_ref, g_ref, o_ref, loc, lpb,
                    gtb, sem):
    i = pl.program_id(0)
    slot = lax.rem(i, 2)
    nloc, t = loc.shape[1], h_ref.shape[0]

    def row_copy(sl, lo, go, rows):
        return pltpu.make_async_copy(ys_hbm.at[pl.ds(go, rows), :], loc.at[sl, pl.ds(lo, rows), :], sem.at[sl])

    def fetch(tile, sl):
        _run_copies(tile, lo_ref, go_ref, n8_ref, lambda lo, go, rows: row_copy(sl, lo, go, rows))

        def zero_rows(j, carry):
            r0 = pl.multiple_of((nt_ref[tile] + j) * SUBLANES, SUBLANES)
            loc[sl, pl.ds(r0, SUBLANES), :] = jnp.zeros((SUBLANES, D_MODEL), F32)
            return carry

        lax.fori_loop(0, nloc // SUBLANES - nt_ref[tile], zero_rows, 0)

    @pl.when(i == 0)
    def _():
        fetch(0, 0)

    @pl.when(i + 1 < pl.num_programs(0))
    def _():
        fetch(i + 1, 1 - slot)

    _wait_rows(nt_ref[i], lambda rows: row_copy(slot, 0, 0, rows))
    kc = 256
    lane = lax.broadcasted_iota(jnp.int32, (t, 128), 1)
    for k in range(TOP_K):
        lpb[k] = jnp.broadcast_to(lpos_ref[:, k:k + 1], (t, 128))
        gtb[k] = jnp.broadcast_to(gate_ref[:, k:k + 1], (t, 128))
    acc = h_ref[...]
    for c in range(nloc // kc):
        pieces = []
        for j in range(kc // 128):
            li = lane + (c * kc + j * 128)
            w = jnp.zeros((t, 128), F32)
            for k in range(TOP_K):
                w = jnp.where(li == lpb[k], gtb[k], w)
            pieces.append(w)
        wc = jnp.concatenate(pieces, axis=1).astype(BF16)
        acc = acc + jnp.dot(wc, loc[slot, c * kc:(c + 1) * kc, :].astype(BF16), preferred_element_type=F32)
    o_ref[...] = _rms(acc, g_ref[...])


def _combine(tabs, ys, h2, lpos_tok, gates_tok, g, nloc):
    n = h2.shape[0]
    t = min(T_TILE, n)
    grid_spec = pltpu.PrefetchScalarGridSpec(
        num_scalar_prefetch=4,
        grid=(n // t,),
        in_specs=[pl.BlockSpec(memory_space=pl.ANY),
                  pl.BlockSpec((t, D_MODEL), lambda i, *_: (i, 0)),
                  pl.BlockSpec((t, TOP_K), lambda i, *_: (i, 0)),
                  pl.BlockSpec((t, TOP_K), lambda i, *_: (i, 0)),
                  pl.BlockSpec((1, D_MODEL), lambda i, *_: (0, 0))],
        out_specs=pl.BlockSpec((t, D_MODEL), lambda i, *_: (i, 0)),
        scratch_shapes=[pltpu.VMEM((2, nloc, D_MODEL), F32), pltpu.VMEM((TOP_K, t, 128), jnp.int32),
                        pltpu.VMEM((TOP_K, t, 128), F32), pltpu.SemaphoreType.DMA((2,))],
    )
    return pl.pallas_call(
        _combine_kernel,
        out_shape=jax.ShapeDtypeStruct((n, D_MODEL), F32),
        grid_spec=grid_spec,
        compiler_params=pltpu.CompilerParams(dimension_semantics=("arbitrary",), vmem_limit_bytes=VMEM_LIMIT),
        name="combine",
    )(tabs["lo"], tabs["go"], tabs["n8"], tabs["nt"], ys, h2, lpos_tok, gates_tok, g)


def _cmul(a, b):
    return a[0] * b[0] - a[1] * b[1], a[0] * b[1] + a[1] * b[0]


def _prep_mixer_params(sub_len, norm_mix_g, w_in, conv_w, conv_b, rg_wa, rg_ba, rg_wx, rg_bx, rg_lambda,
                       s5_lambda_re, s5_lambda_im, s5_b_re, s5_b_im, s5_c_re, s5_c_im, s5_d, s5_log_dt,
                       s5_glu_w, s5_glu_b, w_out):
    row = lambda v: v.reshape(1, -1).astype(F32)
    dt = jnp.exp(s5_log_dt)[:, None]
    mag = jnp.exp(dt * s5_lambda_re)
    abar_re = mag * jnp.cos(dt * s5_lambda_im)
    abar_im = mag * jnp.sin(dt * s5_lambda_im)
    den = s5_lambda_re * s5_lambda_re + s5_lambda_im * s5_lambda_im
    num_re = abar_re - 1.0
    coef_re = (num_re * s5_lambda_re + abar_im * s5_lambda_im) / den
    coef_im = (abar_im * s5_lambda_re - num_re * s5_lambda_im) / den
    bbar_re = coef_re[..., None] * s5_b_re - coef_im[..., None] * s5_b_im
    bbar_im = coef_re[..., None] * s5_b_im + coef_im[..., None] * s5_b_re
    a1 = (abar_re.reshape(-1), abar_im.reshape(-1))
    aq = a1
    for _ in range(int(math.log2(sub_len))):
        aq = _cmul(aq, aq)
    pw = [aq]
    for _ in range(SUBLANES - 1):
        pw.append(_cmul(pw[-1], aq))
    sub = jnp.arange(SUBLANES)[:, None]
    tabs = []
    for d in (1, 2, 4):
        keep = (sub >= d).astype(F32)
        tabs += [keep * pw[d - 1][0][None, :], keep * pw[d - 1][1][None, :]]
    tabs += [jnp.stack([p_[0] for p_ in pw]), jnp.stack([p_[1] for p_ in pw])]
    tabs += [jnp.broadcast_to(a1[0][None, :], (SUBLANES, N_STATE)),
             jnp.broadcast_to(a1[1][None, :], (SUBLANES, N_STATE))]
    s5tab = jnp.stack(tabs).astype(F32)
    gh = S5_GROUPS // 2
    bmat, cmat = [], []
    for hf in range(2):
        gs = slice(hf * gh, (hf + 1) * gh)
        b_re = _block_diag(jnp.transpose(bbar_re[gs], (0, 2, 1)))
        b_im = _block_diag(jnp.transpose(bbar_im[gs], (0, 2, 1)))
        bmat.append(jnp.concatenate([b_re, b_im], axis=1))
        cr = _block_diag(jnp.transpose(s5_c_re[gs], (0, 2, 1)))
        ci = _block_diag(jnp.transpose(s5_c_im[gs], (0, 2, 1)))
        cmat.append(jnp.concatenate([cr, -ci], axis=0))
    return {
        "g_mix": row(norm_mix_g), "w_in": w_in.astype(BF16), "conv_w": conv_w.astype(F32), "conv_b": row(conv_b),
        "w_gates": jnp.concatenate([_block_diag(rg_wa), _block_diag(rg_wx)], axis=1).astype(BF16),
        "b_gates": jnp.concatenate([rg_ba, rg_bx]).reshape(1, -1).astype(F32),
        "sp": row(jax.nn.softplus(-rg_lambda)),
        "bmat": jnp.stack(bmat).astype(BF16), "s5tab": s5tab, "cmat": jnp.stack(cmat).astype(BF16),
        "d_skip": row(s5_d), "glu_w": _block_diag(s5_glu_w).astype(BF16), "glu_b": row(s5_glu_b),
        "w_out": w_out.astype(BF16),
    }


def _slot_tables(tcnt, n_slots):
    i32 = lambda v: v.astype(jnp.int32)
    run = ((tcnt + SUBLANES - 1) // SUBLANES) * SUBLANES
    rows = jnp.sum(run, axis=0)
    blocks = (rows + TM_MOE - 1) // TM_MOE
    bend = jnp.cumsum(blocks)
    estart = (bend - blocks) * TM_MOE
    nb = n_slots // TM_MOE
    n_used = i32(bend[-1])
    eidx = jnp.arange(N_EXPERTS)
    bidx = jnp.arange(nb)[:, None]
    owner = (bidx >= (bend - blocks)[None, :]) & (bidx < bend[None, :])
    pick = lambda per_expert: jnp.sum(jnp.where(owner, per_expert[None, :], 0), axis=1)
    last_exp = jnp.max(jnp.where(blocks > 0, eidx, 0))
    block_exp = jnp.where(jnp.arange(nb) < n_used, pick(eidx), last_exp)
    later = (eidx[None, :] > eidx[:, None]) & (blocks[None, :] > 0)
    next_of = jnp.min(jnp.where(later, eidx[None, :], N_EXPERTS), axis=1)
    next_exp = jnp.where(jnp.arange(nb) < n_used, pick(jnp.where(next_of < N_EXPERTS, next_of, -1)), -1)
    block_rows = jnp.clip(pick(rows + (bend - blocks) * TM_MOE) - jnp.arange(nb) * TM_MOE, 0, TM_MOE)
    tabs = {
        "lo": i32(jnp.cumsum(run, axis=1) - run).reshape(-1),
        "go": i32(estart[None, :] + jnp.cumsum(run, axis=0) - run).reshape(-1),
        "n8": i32(run // SUBLANES).reshape(-1),
        "nt": i32(jnp.sum(run, axis=1) // SUBLANES),
        "zs": i32(jnp.concatenate([estart + rows, bend[-1:] * TM_MOE])),
        "zn": i32(jnp.concatenate([(blocks * TM_MOE - rows) // SUBLANES, nb - bend[-1:]])),
    }
    return i32(block_exp), n_used.reshape(1), i32(next_exp), i32(block_rows), tabs


def kernel(x, mem, norm_mix_g, w_in, conv_w, conv_b, rg_wa, rg_ba, rg_wx, rg_bx, rg_lambda, s5_lambda_re,
           s5_lambda_im, s5_b_re, s5_b_im, s5_c_re, s5_c_im, s5_d, s5_log_dt, s5_glu_w, s5_glu_b, w_out,
           norm_xa_g, mem_norm_g, xa_wq, xa_wkv, xa_wo, norm_ffn_g, router_w, router_b, exp_w_gu, exp_b_gu,
           exp_w_down, exp_b_down, final_norm_g):
    bsz, seq, d = x.shape
    mlen = mem.shape[1]
    n = bsz * seq
    row = lambda v: v.reshape(1, -1).astype(F32)

    p = _prep_mixer_params(min(T_MIX, seq) // SUBLANES, norm_mix_g, w_in, conv_w, conv_b, rg_wa, rg_ba, rg_wx,
                           rg_bx, rg_lambda, s5_lambda_re, s5_lambda_im, s5_b_re, s5_b_im, s5_c_re, s5_c_im,
                           s5_d, s5_log_dt, s5_glu_w, s5_glu_b, w_out)
    h1 = _mixer(x.reshape(n, d), bsz, seq, p)

    k, v = _kv(mem.reshape(bsz * mlen, d), bsz, mlen, row(mem_norm_g), xa_wkv.astype(BF16))
    wr_t = router_w.T.astype(F32)
    wr_hi = wr_t.astype(BF16)
    wr_lo = (wr_t - wr_hi.astype(F32)).astype(BF16)
    h2, xn, lpos_t, gate_t, tcnt = _attn_route(
        h1, bsz, seq, mlen, row(norm_xa_g), xa_wq.astype(BF16), k, v, xa_wo.astype(BF16),
        row(norm_ffn_g), jnp.concatenate([wr_hi, wr_lo], axis=0), router_b.reshape(-1, 1).astype(F32))
    tile = min(T_TILE, min(T_ATT, seq))
    n_tiles = n // tile
    run_pad = N_EXPERTS * (SUBLANES - 1)
    nloc = -(-(TOP_K * tile + run_pad) // 128) * 128
    n_slots = -(-(TOP_K * n + n_tiles * run_pad + N_EXPERTS * (TM_MOE - SUBLANES)) // TM_MOE) * TM_MOE
    block_exp, n_used, next_exp, block_rows, tabs = _slot_tables(tcnt[:, :, 0], n_slots)
    xs = _dispatch(tabs, xn, lpos_t, n_slots, nloc)
    ys = _moe(block_exp, n_used, next_exp, block_rows, xs, exp_w_gu, exp_b_gu.reshape(N_EXPERTS, 1, -1),
              exp_w_down, exp_b_down.reshape(N_EXPERTS, 1, -1))
    out = _combine(tabs, ys, h2, lpos_t.T, gate_t.T, row(final_norm_g), nloc)
    return out.reshape(bsz, seq, d)
```
